```python
import jax, jax.numpy as jnp
from jax import lax
import numpy as np

D_MODEL = 2048
BATCH = 4
SEQ = 8192
DEPTH = 1
DEC_BATCH = 16
DEC_SEQ = 32
PAST_LEN = 1024

CHUNK = 64
HEAD_DIM = 128
FOX_WIDTH = D_MODEL // 2
N_FOX_HEADS = FOX_WIDTH // HEAD_DIM
POOL_WIDTH = D_MODEL - FOX_WIDTH
POOL_WINDOWS = (2, 4, 8, 16)
N_POOL_GROUPS = len(POOL_WINDOWS)
POOL_GROUP_DIM = POOL_WIDTH // N_POOL_GROUPS
POOL_HIST = max(POOL_WINDOWS) - 1
MIX_WIDTH = FOX_WIDTH + POOL_WIDTH
IN_WIDTH = 3 * FOX_WIDTH + N_FOX_HEADS + POOL_WIDTH
D_FF = 256 * ((8 * D_MODEL + 3 * 256 - 1) // (3 * 256))
Q_BLOCK = 128
EPS = 1e-6
ATTN_SCALE = HEAD_DIM ** -0.5

kernel_name = 'fox_pool_macaron_stream_step'


def _rms_norm(x, g):
    xf = x.astype(jnp.float32)
    y = xf * lax.rsqrt(jnp.mean(xf * xf, axis=-1, keepdims=True) + EPS)
    return (y * g.astype(jnp.float32)).astype(x.dtype)


def _swiglu(x, w_gate, w_up, w_down):
    return (jax.nn.silu(x @ w_gate) * (x @ w_up)) @ w_down


def _split_groups(u, b_f):
    B, T, _ = u.shape
    hs = (B, T, N_FOX_HEADS, HEAD_DIM)
    q = u[..., :FOX_WIDTH].reshape(hs)
    k = u[..., FOX_WIDTH:2 * FOX_WIDTH].reshape(hs)
    v = u[..., 2 * FOX_WIDTH:3 * FOX_WIDTH].reshape(hs)
    o = 3 * FOX_WIDTH
    logf = jax.nn.log_sigmoid(u[..., o:o + N_FOX_HEADS].astype(jnp.float32) + b_f.astype(jnp.float32))
    p = u[..., o + N_FOX_HEADS:]
    return q, k, v, logf, p


def _fox_block(q, k, v, c_q, c_k, q_pos, k_pos):
    s = jnp.einsum('bqhd,bkhd->bhqk', q, k).astype(jnp.float32) * ATTN_SCALE
    s = s + c_q[..., :, None] - c_k[:, :, None, :]
    s = jnp.where(k_pos[None, :] <= q_pos[:, None], s, -jnp.inf)
    p = jax.nn.softmax(s, axis=-1).astype(v.dtype)
    return jnp.einsum('bhqk,bkhd->bqhd', p, v)


def _fox_prompt(q, k, v, logf):
    B, S, H, Dh = q.shape
    nb = S // Q_BLOCK
    c = jnp.cumsum(logf, axis=1).transpose(0, 2, 1)
    qb = q.reshape(B, nb, Q_BLOCK, H, Dh).swapaxes(0, 1)
    cqb = c.reshape(B, H, nb, Q_BLOCK).transpose(2, 0, 1, 3)
    pos = jnp.arange(S, dtype=jnp.int32)
    qposb = pos.reshape(nb, Q_BLOCK)
    out = lax.map(lambda blk: _fox_block(blk[0], k, v, blk[1], c, blk[2], pos), (qb, cqb, qposb))
    return out.swapaxes(0, 1).reshape(B, S, H * Dh)


def _fox_sample(q, k, v, logf, cache_k, cache_v, cache_logf):
    B, T, H, Dh = q.shape
    P = cache_k.shape[1]
    k_all = jnp.concatenate([cache_k.astype(k.dtype), k], axis=1)
    v_all = jnp.concatenate([cache_v.astype(v.dtype), v], axis=1)
    lf_all = jnp.concatenate([cache_logf.astype(jnp.float32), logf], axis=1)
    c = jnp.cumsum(lf_all, axis=1).transpose(0, 2, 1)
    k_pos = jnp.arange(P + T, dtype=jnp.int32)
    q_pos = P + jnp.arange(T, dtype=jnp.int32)
    out = _fox_block(q, k_all, v_all, c[:, :, P:], c, q_pos, k_pos)
    return out.reshape(B, T, H * Dh)


def _pool_mix(ext, pos_out, w_pool, pool_scale):
    L = ext.shape[1]
    n = pos_out.shape[0]
    start = L - n
    xf = ext.astype(jnp.float32)
    cs = jnp.cumsum(xf, axis=1)
    cs0 = jnp.concatenate([jnp.zeros_like(cs[:, :1]), cs], axis=1)
    outs = []
    for g, w in enumerate(POOL_WINDOWS):
        sl = slice(g * POOL_GROUP_DIM, (g + 1) * POOL_GROUP_DIM)
        tot = cs0[:, start + 1:L + 1, sl] - cs0[:, start + 1 - w:L + 1 - w, sl]
        cnt = jnp.minimum(pos_out + 1, w).astype(jnp.float32)[None, :, None]
        d = (tot / cnt - xf[:, start:, sl]).astype(ext.dtype)
        outs.append(d @ w_pool[g])
    return jnp.concatenate(outs, axis=-1) * pool_scale


def _pre_mix(x, g_ffn1, w1_gate, w1_up, w1_down, g_mix, w_in, b_f):
    h = x + 0.5 * _swiglu(_rms_norm(x, g_ffn1), w1_gate, w1_up, w1_down)
    u = _rms_norm(h, g_mix) @ w_in
    q, k, v, logf, p = _split_groups(u, b_f)
    return h, q, k, v, logf, p


def _post_mix(h, fox_out, pool_out, w_o, g_ffn2, w2_gate, w2_up, w2_down):
    h = h + jnp.concatenate([fox_out, pool_out.astype(fox_out.dtype)], axis=-1) @ w_o
    return h + 0.5 * _swiglu(_rms_norm(h, g_ffn2), w2_gate, w2_up, w2_down)


def setup_inputs(seed: int = 0) -> dict:
    key = jax.random.key(seed)
    ks = jax.random.split(key, 24)

    def nrm(k, shape, scale=1.0):
        return jax.random.normal(k, shape, jnp.float32) * scale

    Ld = DEPTH
    inp = {}
    inp['x_prompt'] = nrm(ks[0], (BATCH, SEQ, D_MODEL))
    inp['x_sample'] = nrm(ks[1], (DEC_BATCH, DEC_SEQ, D_MODEL))
    inp['cache_k'] = nrm(ks[2], (Ld, DEC_BATCH, PAST_LEN, N_FOX_HEADS, HEAD_DIM))
    inp['cache_v'] = nrm(ks[3], (Ld, DEC_BATCH, PAST_LEN, N_FOX_HEADS, HEAD_DIM))
    inp['cache_logf'] = jax.nn.log_sigmoid(
        jax.random.uniform(ks[4], (Ld, DEC_BATCH, PAST_LEN, N_FOX_HEADS), jnp.float32, 1.0, 5.0)
        + nrm(ks[5], (Ld, DEC_BATCH, PAST_LEN, N_FOX_HEADS), 0.5))
    inp['state_pool'] = nrm(ks[6], (Ld, DEC_BATCH, POOL_HIST, POOL_WIDTH))
    inp['g_ffn1'] = 1.0 + nrm(ks[7], (Ld, D_MODEL), 0.02)
    inp['w1_gate'] = nrm(ks[8], (Ld, D_MODEL, D_FF), D_MODEL ** -0.5)
    inp['w1_up'] = nrm(ks[9], (Ld, D_MODEL, D_FF), D_MODEL ** -0.5)
    inp['w1_down'] = nrm(ks[10], (Ld, D_FF, D_MODEL), D_FF ** -0.5)
    inp['g_mix'] = 1.0 + nrm(ks[11], (Ld, D_MODEL), 0.02)
    inp['w_in'] = nrm(ks[12], (Ld, D_MODEL, IN_WIDTH), D_MODEL ** -0.5)
    inp['b_f'] = jax.random.uniform(ks[13], (Ld, N_FOX_HEADS), jnp.float32, 1.0, 5.0)
    inp['w_pool'] = nrm(ks[14], (Ld, N_POOL_GROUPS, POOL_GROUP_DIM, POOL_GROUP_DIM), POOL_GROUP_DIM ** -0.5)
    inp['pool_scale'] = 1.0 + nrm(ks[15], (Ld, POOL_WIDTH), 0.1)
    inp['w_o'] = nrm(ks[16], (Ld, MIX_WIDTH, D_MODEL), MIX_WIDTH ** -0.5)
    inp['g_ffn2'] = 1.0 + nrm(ks[17], (Ld, D_MODEL), 0.02)
    inp['w2_gate'] = nrm(ks[18], (Ld, D_MODEL, D_FF), D_MODEL ** -0.5)
    inp['w2_up'] = nrm(ks[19], (Ld, D_MODEL, D_FF), D_MODEL ** -0.5)
    inp['w2_down'] = nrm(ks[20], (Ld, D_FF, D_MODEL), D_FF ** -0.5)
    inp['g_final'] = 1.0 + nrm(ks[21], (D_MODEL,), 0.02)
    return inp


def reference(x_prompt, x_sample, cache_k, cache_v, cache_logf, state_pool,
              g_ffn1, w1_gate, w1_up, w1_down, g_mix, w_in, b_f, w_pool, pool_scale,
              w_o, g_ffn2, w2_gate, w2_up, w2_down, g_final):
    B, S, _ = x_prompt.shape
    Tn = x_sample.shape[1]
    P = cache_k.shape[2]
    pos_prompt = jnp.arange(S, dtype=jnp.int32)
    pos_sample = P + jnp.arange(Tn, dtype=jnp.int32)

    hp, hs = x_prompt, x_sample
    kp_l, vp_l, lfp_l, pp_l = [], [], [], []
    ks_l, vs_l, lfs_l, ps_l = [], [], [], []
    for l in range(DEPTH):
        h, q, k, v, logf, p = _pre_mix(hp, g_ffn1[l], w1_gate[l], w1_up[l], w1_down[l], g_mix[l], w_in[l], b_f[l])
        fox_out = _fox_prompt(q, k, v, logf)
        ext = jnp.concatenate([jnp.zeros((B, POOL_HIST, POOL_WIDTH), p.dtype), p], axis=1)
        pool_out = _pool_mix(ext, pos_prompt, w_pool[l], pool_scale[l])
        hp = _post_mix(h, fox_out, pool_out, w_o[l], g_ffn2[l], w2_gate[l], w2_up[l], w2_down[l])
        kp_l.append(k); vp_l.append(v); lfp_l.append(logf); pp_l.append(ext[:, -POOL_HIST:])

        h, q, k, v, logf, p = _pre_mix(hs, g_ffn1[l], w1_gate[l], w1_up[l], w1_down[l], g_mix[l], w_in[l], b_f[l])
        fox_out = _fox_sample(q, k, v, logf, cache_k[l], cache_v[l], cache_logf[l])
        ext = jnp.concatenate([state_pool[l].astype(p.dtype), p], axis=1)
        pool_out = _pool_mix(ext, pos_sample, w_pool[l], pool_scale[l])
        hs = _post_mix(h, fox_out, pool_out, w_o[l], g_ffn2[l], w2_gate[l], w2_up[l], w2_down[l])
        ks_l.append(k); vs_l.append(v); lfs_l.append(logf); ps_l.append(ext[:, -POOL_HIST:])

    y_prompt = _rms_norm(hp, g_final)
    y_sample = _rms_norm(hs, g_final)
    k_prompt = jnp.stack(kp_l)
    v_prompt = jnp.stack(vp_l)
    logf_prompt = jnp.stack(lfp_l)
    pool_prompt = jnp.stack(pp_l)
    k_sample = jnp.stack(ks_l)
    v_sample = jnp.stack(vs_l)
    logf_sample = jnp.stack(lfs_l)
    pool_sample = jnp.stack(ps_l)
    return (y_prompt, y_sample, k_prompt, v_prompt, logf_prompt, pool_prompt, k_sample, v_sample, logf_sample, pool_sample)
```

```python
import functools

import jax
import jax.numpy as jnp
from jax import lax
from jax.experimental import pallas as pl
from jax.experimental.pallas import tpu as pltpu

F32 = jnp.float32
BF16 = jnp.bfloat16

EPS = 1e-6
HEAD_DIM = 128
LANES = 128
POOL_WINDOWS = (2, 4, 8, 16)
POOL_HIST = max(POOL_WINDOWS) - 1
HIST_ROWS = 16
N_SPLIT = 3
VMEM_LIMIT = 56 * 1024 * 1024


def _cparams(sem):
    return pltpu.CompilerParams(dimension_semantics=sem, vmem_limit_bytes=VMEM_LIMIT)


def _pick_tile(length, cap, mult):
    best = None
    for t in range(mult, min(cap, length) + 1, mult):
        if length % t == 0:
            best = t
    assert best is not None, (length, cap, mult)
    return best


def _rms(x, g):
    return x * lax.rsqrt(jnp.mean(x * x, axis=-1, keepdims=True) + EPS) * g


def _split3(x):
    hi = x.astype(BF16).astype(F32)
    r1 = x - hi
    mid = r1.astype(BF16).astype(F32)
    lo = (r1 - mid).astype(BF16).astype(F32)
    return hi, mid, lo


def _ffn_kernel(x_ref, g_ref, wg_ref, wu_ref, wd_ref, gf_ref, o_ref, xn_ref, acc_ref, *, final_norm):
    f = pl.program_id(1)

    @pl.when(f == 0)
    def _():
        xn_ref[...] = _rms(x_ref[...], g_ref[...]).astype(BF16)
        acc_ref[...] = jnp.zeros_like(acc_ref)

    xn = xn_ref[...]
    gate = jnp.dot(xn, wg_ref[...], preferred_element_type=F32)
    up = jnp.dot(xn, wu_ref[...], preferred_element_type=F32)
    a = (gate * jax.nn.sigmoid(gate) * up).astype(BF16)
    acc_ref[...] += jnp.dot(a, wd_ref[...], preferred_element_type=F32)

    @pl.when(f == pl.num_programs(1) - 1)
    def _():
        y = x_ref[...] + 0.5 * acc_ref[...]
        if final_norm:
            y = _rms(y, gf_ref[...])
        o_ref[...] = y


def _ffn(x, g, wg, wu, wd, gf, *, final_norm, name):
    n, d = x.shape
    dff = wg.shape[1]
    tm = min(512, n)
    tf = 512
    assert n % tm == 0 and dff % tf == 0
    return pl.pallas_call(
        functools.partial(_ffn_kernel, final_norm=final_norm),
        grid=(n // tm, dff // tf),
        in_specs=[
            pl.BlockSpec((tm, d), lambda i, f: (i, 0)),
            pl.BlockSpec((1, d), lambda i, f: (0, 0)),
            pl.BlockSpec((d, tf), lambda i, f: (0, f)),
            pl.BlockSpec((d, tf), lambda i, f: (0, f)),
            pl.BlockSpec((tf, d), lambda i, f: (f, 0)),
            pl.BlockSpec((1, d), lambda i, f: (0, 0)),
        ],
        out_specs=pl.BlockSpec((tm, d), lambda i, f: (i, 0)),
        out_shape=jax.ShapeDtypeStruct((n, d), F32),
        scratch_shapes=[pltpu.VMEM((tm, d), BF16), pltpu.VMEM((tm, d), F32)],
        compiler_params=_cparams(("parallel", "arbitrary")),
        name=name,
    )(x, g, wg, wu, wd, gf)


def _inproj_kernel(h_ref, g_ref, w_ref, wf_ref, bf_ref,
                   q_ref, k_ref, v_ref, kb_ref, vb_ref, p_ref, lf_ref, hn_ref, *, scale):
    j = pl.program_id(1)

    @pl.when(j == 0)
    def _():
        hn = _rms(h_ref[...], g_ref[...]).astype(BF16)
        hn_ref[...] = hn
        z = jnp.dot(hn, wf_ref[...], preferred_element_type=F32) + bf_ref[...]
        lf_ref[...] = jax.nn.log_sigmoid(z)

    u = jnp.dot(hn_ref[...], w_ref[...], preferred_element_type=F32)

    @pl.when(j == 0)
    def _():
        q_ref[...] = (u * scale).astype(BF16)

    @pl.when(j == 1)
    def _():
        k_ref[...] = u
        kb_ref[...] = u.astype(BF16)

    @pl.when(j == 2)
    def _():
        v_ref[...] = u
        vb_ref[...] = u.astype(BF16)

    @pl.when(j == 3)
    def _():
        p_ref[...] = u


def _inproj(h, g, w_qkvp, wf, bf, *, name):
    n, d = h.shape
    w = w_qkvp.shape[1] // 4
    tm = min(512, n)
    assert n % tm == 0
    row = lambda i, j: (i, 0)
    return pl.pallas_call(
        functools.partial(_inproj_kernel, scale=HEAD_DIM ** -0.5),
        grid=(n // tm, 4),
        in_specs=[
            pl.BlockSpec((tm, d), row),
            pl.BlockSpec((1, d), lambda i, j: (0, 0)),
            pl.BlockSpec((d, w), lambda i, j: (0, j)),
            pl.BlockSpec((d, LANES), lambda i, j: (0, 0)),
            pl.BlockSpec((1, LANES), lambda i, j: (0, 0)),
        ],
        out_specs=[
            pl.BlockSpec((tm, w), row), pl.BlockSpec((tm, w), row), pl.BlockSpec((tm, w), row),
            pl.BlockSpec((tm, w), row), pl.BlockSpec((tm, w), row), pl.BlockSpec((tm, w), row),
            pl.BlockSpec((tm, LANES), row),
        ],
        out_shape=[
            jax.ShapeDtypeStruct((n, w), BF16),
            jax.ShapeDtypeStruct((n, w), F32),
            jax.ShapeDtypeStruct((n, w), F32),
            jax.ShapeDtypeStruct((n, w), BF16),
            jax.ShapeDtypeStruct((n, w), BF16),
            jax.ShapeDtypeStruct((n, w), F32),
            jax.ShapeDtypeStruct((n, LANES), F32),
        ],
        scratch_shapes=[pltpu.VMEM((tm, d), BF16)],
        compiler_params=_cparams(("parallel", "arbitrary")),
        name=name,
    )(h, g, w_qkvp, wf, bf)


def _aug_kernel(lf_ref, tri_ref, selq_ref, selk_ref, qx_ref, kx_ref, carry_ref, *, n_heads):
    t = pl.program_id(1)

    @pl.when(t == 0)
    def _():
        carry_ref[...] = jnp.zeros_like(carry_ref)

    lf = lf_ref[0]
    lane = lax.broadcasted_iota(jnp.int32, lf.shape, 1)

    def pick(parts, extra=None):
        out = jnp.zeros_like(lf) if extra is None else extra
        for j in reversed(range(N_SPLIT)):
            out = jnp.where((lane >= j * n_heads) & (lane < (j + 1) * n_heads), parts[j], out)
        return out

    cs = jnp.dot(tri_ref[...], pick(_split3(lf)).astype(BF16), preferred_element_type=F32)
    c = cs
    for j in range(1, N_SPLIT):
        c = c + pltpu.roll(cs, LANES - j * n_heads, axis=1)
    c = c + carry_ref[...]
    carry_ref[...] = c[c.shape[0] - 1:, :]

    crep = c
    for j in range(1, N_SPLIT):
        crep = jnp.where((lane >= j * n_heads) & (lane < (j + 1) * n_heads),
                         pltpu.roll(c, j * n_heads, axis=1), crep)
    ones = jnp.where(lane == N_SPLIT * n_heads, 1.0, 0.0).astype(F32)
    pieces = pick(_split3(crep), extra=ones).astype(BF16)
    qx_ref[0] = jnp.dot(pieces, selq_ref[...], preferred_element_type=F32).astype(BF16)
    kx_ref[0] = jnp.dot(pieces, selk_ref[...], preferred_element_type=F32).astype(BF16)


def _aug_constants(tm, n_heads):
    r = jnp.arange(tm)
    tri = (r[:, None] >= r[None, :]).astype(BF16)
    rows = jnp.arange(LANES)[:, None]
    cols = jnp.arange(n_heads * HEAD_DIM)[None, :]
    head, lane = cols // HEAD_DIM, cols % HEAD_DIM
    one_row = N_SPLIT * n_heads
    is_piece = lambda off: (lane >= off) & (lane < off + N_SPLIT) & (rows == (lane - off) * n_heads + head)
    is_one = lambda off: (lane >= off) & (lane < off + N_SPLIT) & (rows == one_row)
    selq = (is_piece(0) | is_one(N_SPLIT)).astype(BF16)
    selk = is_one(0).astype(BF16) - is_piece(N_SPLIT).astype(BF16)
    return tri, selq, selk


def _aug(lf, *, n_heads, name):
    nseq, length, _ = lf.shape
    tm = _pick_tile(length, 512, 16)
    assert (N_SPLIT * n_heads) < LANES
    w = n_heads * HEAD_DIM
    tri, selq, selk = _aug_constants(tm, n_heads)
    return pl.pallas_call(
        functools.partial(_aug_kernel, n_heads=n_heads),
        grid=(nseq, length // tm),
        in_specs=[
            pl.BlockSpec((1, tm, LANES), lambda b, t: (b, t, 0)),
            pl.BlockSpec((tm, tm), lambda b, t: (0, 0)),
            pl.BlockSpec((LANES, w), lambda b, t: (0, 0)),
            pl.BlockSpec((LANES, w), lambda b, t: (0, 0)),
        ],
        out_specs=[pl.BlockSpec((1, tm, w), lambda b, t: (b, t, 0)),
                   pl.BlockSpec((1, tm, w), lambda b, t: (b, t, 0))],
        out_shape=[jax.ShapeDtypeStruct((nseq, length, w), BF16),
                   jax.ShapeDtypeStruct((nseq, length, w), BF16)],
        scratch_shapes=[pltpu.VMEM((1, LANES), F32)],
        compiler_params=_cparams(("parallel", "arbitrary")),
        name=name,
    )(lf, tri, selq, selk)


def _attn_kernel(q_ref, qx_ref, k_ref, kx_ref, v_ref, o_ref, m_ref, l_ref, acc_ref, *, tq, tk, q_off):
    qi = pl.program_id(2)
    q = jnp.concatenate([q_ref[0], qx_ref[0]], axis=1)
    q_pos0 = q_off + qi * tq
    n_kv = (q_pos0 + tq + tk - 1) // tk

    m_ref[...] = jnp.full_like(m_ref, -jnp.inf)
    l_ref[...] = jnp.zeros_like(l_ref)
    acc_ref[...] = jnp.zeros_like(acc_ref)
    row_pos = q_pos0 + lax.broadcasted_iota(jnp.int32, (tq, tk), 0)
    col_iota = lax.broadcasted_iota(jnp.int32, (tq, tk), 1)

    def body(j, carry):
        start = pl.multiple_of(j * tk, tk)
        k = jnp.concatenate([k_ref[0, pl.ds(start, tk), :], kx_ref[0, pl.ds(start, tk), :]], axis=1)
        s = lax.dot_general(q, k, (((1,), (1,)), ((), ())), preferred_element_type=F32)
        s = jnp.where(col_iota + start <= row_pos, s, -jnp.inf)
        m_prev = m_ref[...]
        m_new = jnp.maximum(m_prev, jnp.max(s, axis=1, keepdims=True))
        alpha = jnp.exp(m_prev - m_new)
        p = jnp.exp(s - m_new)
        l_ref[...] = alpha * l_ref[...] + jnp.sum(p, axis=1, keepdims=True)
        acc_ref[...] = alpha * acc_ref[...] + jnp.dot(
            p.astype(BF16), v_ref[0, pl.ds(start, tk), :], preferred_element_type=F32)
        m_ref[...] = m_new
        return carry

    lax.fori_loop(0, n_kv, body, 0)
    o_ref[0] = (acc_ref[...] / l_ref[...]).astype(o_ref.dtype)


def _attn(q, qx, k, kx, v, *, q_off, name):
    nb, sq, w = q.shape
    length = k.shape[1]
    n_heads = w // HEAD_DIM
    tq = _pick_tile(sq, 512, 16)
    tk = _pick_tile(length, 512, LANES)
    assert q_off % tq == 0 and q_off + sq <= length
    qoff_blk = q_off // tq
    kv_spec = pl.BlockSpec((1, length, HEAD_DIM), lambda b, h, i: (b, 0, h))
    return pl.pallas_call(
        functools.partial(_attn_kernel, tq=tq, tk=tk, q_off=q_off),
        grid=(nb, n_heads, sq // tq),
        in_specs=[
            pl.BlockSpec((1, tq, HEAD_DIM), lambda b, h, i: (b, i, h)),
            pl.BlockSpec((1, tq, HEAD_DIM), lambda b, h, i: (b, i + qoff_blk, h)),
            kv_spec, kv_spec, kv_spec,
        ],
        out_specs=pl.BlockSpec((1, tq, HEAD_DIM), lambda b, h, i: (b, i, h)),
        out_shape=jax.ShapeDtypeStruct((nb, sq, w), BF16),
        scratch_shapes=[pltpu.VMEM((tq, 1), F32), pltpu.VMEM((tq, 1), F32), pltpu.VMEM((tq, HEAD_DIM), F32)],
        compiler_params=_cparams(("parallel", "parallel", "arbitrary")),
        name=name,
    )(q, qx, k, kx, v)


def _pool_kernel(p_ref, hist_ref, w_ref, sc_ref, o_ref, ext_ref, *, tp, pos0):
    t = pl.program_id(1)

    @pl.when(t == 0)
    def _():
        ext_ref[0:HIST_ROWS, :] = hist_ref[0]

    ext_ref[HIST_ROWS:, :] = p_ref[0]
    pos = pos0 + t * tp + lax.broadcasted_iota(jnp.int32, (tp, 1), 0)
    gd = w_ref.shape[1]
    for g, win in enumerate(POOL_WINDOWS):
        cols = slice(g * gd, (g + 1) * gd)
        cur = ext_ref[HIST_ROWS:, cols]
        tot = cur
        for back in range(1, win):
            tot = tot + ext_ref[HIST_ROWS - back:HIST_ROWS - back + tp, cols]
        cnt = jnp.minimum(pos + 1, win).astype(F32)
        d = (tot / cnt - cur).astype(BF16)
        y = jnp.dot(d, w_ref[g], preferred_element_type=F32) * sc_ref[:, cols]
        o_ref[0, :, cols] = y.astype(o_ref.dtype)
    ext_ref[0:HIST_ROWS, :] = ext_ref[tp:tp + HIST_ROWS, :]


def _pool(p, hist, w_pool, scale, *, pos0, name):
    nseq, length, w = p.shape
    tp = min(512, length)
    assert length % tp == 0 and tp >= HIST_ROWS
    return pl.pallas_call(
        functools.partial(_pool_kernel, tp=tp, pos0=pos0),
        grid=(nseq, length // tp),
        in_specs=[
            pl.BlockSpec((1, tp, w), lambda b, t: (b, t, 0)),
            pl.BlockSpec((1, HIST_ROWS, w), lambda b, t: (b, 0, 0)),
            pl.BlockSpec(w_pool.shape, lambda b, t: (0, 0, 0)),
            pl.BlockSpec((1, w), lambda b, t: (0, 0)),
        ],
        out_specs=pl.BlockSpec((1, tp, w), lambda b, t: (b, t, 0)),
        out_shape=jax.ShapeDtypeStruct((nseq, length, w), BF16),
        scratch_shapes=[pltpu.VMEM((HIST_ROWS + tp, w), F32)],
        compiler_params=_cparams(("parallel", "arbitrary")),
        name=name,
    )(p, hist, w_pool, scale)


def _oproj_kernel(h_ref, fox_ref, pool_ref, wo_ref, o_ref):
    mix = jnp.concatenate([fox_ref[...], pool_ref[...]], axis=1)
    o_ref[...] = h_ref[...] + jnp.dot(mix, wo_ref[...], preferred_element_type=F32)


def _oproj(h, fox, pool, wo, *, name):
    n, d = h.shape
    tm = min(512, n)
    assert n % tm == 0
    return pl.pallas_call(
        _oproj_kernel,
        grid=(n // tm,),
        in_specs=[
            pl.BlockSpec((tm, d), lambda i: (i, 0)),
            pl.BlockSpec((tm, fox.shape[1]), lambda i: (i, 0)),
            pl.BlockSpec((tm, pool.shape[1]), lambda i: (i, 0)),
            pl.BlockSpec(wo.shape, lambda i: (0, 0)),
        ],
        out_specs=pl.BlockSpec((tm, d), lambda i: (i, 0)),
        out_shape=jax.ShapeDtypeStruct((n, d), F32),
        compiler_params=_cparams(("parallel",)),
        name=name,
    )(h, fox, pool, wo)


def _round_up(x, m):
    return (x + m - 1) // m * m


def kernel(x_prompt, x_sample, cache_k, cache_v, cache_logf, state_pool, g_ffn1, w1_gate, w1_up, w1_down, g_mix, w_in, b_f, w_pool, pool_scale, w_o, g_ffn2, w2_gate, w2_up, w2_down, g_final):
    B, S, D = x_prompt.shape
    Bs, Tn, _ = x_sample.shape
    depth, _, P, H, Dh = cache_k.shape
    assert depth == 1 and Dh == HEAD_DIM
    fw = H * Dh
    pw = w_pool.shape[1] * w_pool.shape[2]
    assert w_in.shape[2] == 3 * fw + H + pw and state_pool.shape[2] == POOL_HIST

    row = lambda a: a.reshape(1, -1).astype(F32)
    w1g, w1u, w1d = w1_gate[0].astype(BF16), w1_up[0].astype(BF16), w1_down[0].astype(BF16)
    w2g, w2u, w2d = w2_gate[0].astype(BF16), w2_up[0].astype(BF16), w2_down[0].astype(BF16)
    wi = w_in[0]
    w_qkvp = jnp.concatenate([wi[:, :3 * fw], wi[:, 3 * fw + H:]], axis=1).astype(BF16)
    wf_cols = jnp.tile(wi[:, 3 * fw:3 * fw + H], (1, N_SPLIT))
    wf = jnp.pad(wf_cols, ((0, 0), (0, LANES - N_SPLIT * H))).astype(BF16)
    bf = jnp.pad(jnp.tile(b_f[0].astype(F32), N_SPLIT), (0, LANES - N_SPLIT * H)).reshape(1, LANES)
    wo = w_o[0].astype(BF16)
    wp = w_pool[0].astype(BF16)
    g1, gm, g2, gfin, psc = row(g_ffn1[0]), row(g_mix[0]), row(g_ffn2[0]), row(g_final), row(pool_scale[0])

    def pre(x2d, tag):
        h = _ffn(x2d, g1, w1g, w1u, w1d, gfin, final_norm=False, name=f"ffn1_{tag}")
        return (h,) + tuple(_inproj(h, gm, w_qkvp, wf, bf, name=f"inproj_{tag}"))

    def post(h, fox, pool, tag):
        h2 = _oproj(h, fox, pool, wo, name=f"oproj_{tag}")
        return _ffn(h2, g2, w2g, w2u, w2d, gfin, final_norm=True, name=f"ffn2_{tag}")

    h, q, k, v, kb, vb, p, lf = pre(x_prompt.reshape(B * S, D), "prompt")
    qx, kx = _aug(lf.reshape(B, S, LANES), n_heads=H, name="aug_prompt")
    fox = _attn(q.reshape(B, S, fw), qx, kb.reshape(B, S, fw), kx, vb.reshape(B, S, fw),
                q_off=0, name="attn_prompt")
    p3 = p.reshape(B, S, pw)
    pool = _pool(p3, jnp.zeros((B, HIST_ROWS, pw), F32), wp, psc, pos0=0, name="pool_prompt")
    y_prompt = post(h, fox.reshape(B * S, fw), pool.reshape(B * S, pw), "prompt").reshape(B, S, D)
    k_prompt = k.reshape(1, B, S, H, Dh)
    v_prompt = v.reshape(1, B, S, H, Dh)
    logf_prompt = lf[:, :H].reshape(1, B, S, H)
    pool_prompt = p3[:, S - POOL_HIST:][None]

    h, q, k, v, kb, vb, p, lf = pre(x_sample.reshape(Bs * Tn, D), "sample")
    L = _round_up(P + Tn, LANES)
    tail = L - P - Tn
    lf_cache = jnp.pad(jnp.tile(cache_logf[0].astype(F32), (1, 1, N_SPLIT)),
                       ((0, 0), (0, 0), (0, LANES - N_SPLIT * H)))
    lf_all = jnp.concatenate([lf_cache, lf.reshape(Bs, Tn, LANES), jnp.zeros((Bs, tail, LANES), F32)], axis=1)
    qx, kx = _aug(lf_all, n_heads=H, name="aug_sample")
    k_all = jnp.concatenate([cache_k[0].reshape(Bs, P, fw).astype(BF16), kb.reshape(Bs, Tn, fw),
                             jnp.zeros((Bs, tail, fw), BF16)], axis=1)
    v_all = jnp.concatenate([cache_v[0].reshape(Bs, P, fw).astype(BF16), vb.reshape(Bs, Tn, fw),
                             jnp.zeros((Bs, tail, fw), BF16)], axis=1)
    fox = _attn(q.reshape(Bs, Tn, fw), qx, k_all, kx, v_all, q_off=P, name="attn_sample")
    p3 = p.reshape(Bs, Tn, pw)
    hist = jnp.concatenate([jnp.zeros((Bs, HIST_ROWS - POOL_HIST, pw), F32), state_pool[0].astype(F32)], axis=1)
    pool = _pool(p3, hist, wp, psc, pos0=P, name="pool_sample")
    y_sample = post(h, fox.reshape(Bs * Tn, fw), pool.reshape(Bs * Tn, pw), "sample").reshape(Bs, Tn, D)
    k_sample = k.reshape(1, Bs, Tn, H, Dh)
    v_sample = v.reshape(1, Bs, Tn, H, Dh)
    logf_sample = lf[:, :H].reshape(1, Bs, Tn, H)
    ext_tail = jnp.concatenate([state_pool[0].astype(F32), p3], axis=1)[:, Tn:]
    pool_sample = ext_tail[None]

    return (y_prompt, y_sample, k_prompt, v_prompt, logf_prompt, pool_prompt,
            k_sample, v_sample, logf_sample, pool_sample)
```

```python
import functools

import jax
import jax.numpy as jnp
from jax import lax
from jax.experimental import pallas as pl
from jax.experimental.pallas import tpu as pltpu

F32 = jnp.float32
BF16 = jnp.bfloat16

EPS = 1e-6
HEAD_DIM = 128
LANES = 128
POOL_WINDOWS = (2, 4, 8, 16)
POOL_HIST = max(POOL_WINDOWS) - 1
HIST_ROWS = 16
N_SPLIT = 3
VT_ROWS = HEAD_DIM + 16
LOG2E = 1.4426950408889634
VMEM_LIMIT = 56 * 1024 * 1024


def _cparams(sem):
    return pltpu.CompilerParams(dimension_semantics=sem, vmem_limit_bytes=VMEM_LIMIT)


def _pick_tile(length, cap, mult):
    best = None
    for t in range(mult, min(cap, length) + 1, mult):
        if length % t == 0:
            best = t
    assert best is not None, (length, cap, mult)
    return best


def _rms(x, g):
    return x * lax.rsqrt(jnp.mean(x * x, axis=-1, keepdims=True) + EPS) * g


def _split3(x):
    hi = x.astype(BF16).astype(F32)
    r1 = x - hi
    mid = r1.astype(BF16).astype(F32)
    lo = (r1 - mid).astype(BF16).astype(F32)
    return hi, mid, lo


def _ffn_kernel(x_ref, g_ref, wg_ref, wu_ref, wd_ref, gf_ref, o_ref, xn_ref, acc_ref, *, final_norm):
    f = pl.program_id(1)

    @pl.when(f == 0)
    def _():
        xn_ref[...] = _rms(x_ref[...], g_ref[...]).astype(BF16)
        acc_ref[...] = jnp.zeros_like(acc_ref)

    xn = xn_ref[...]
    gate = jnp.dot(xn, wg_ref[...], preferred_element_type=F32)
    up = jnp.dot(xn, wu_ref[...], preferred_element_type=F32)
    a = (gate * jax.nn.sigmoid(gate) * up).astype(BF16)
    acc_ref[...] += jnp.dot(a, wd_ref[...], preferred_element_type=F32)

    @pl.when(f == pl.num_programs(1) - 1)
    def _():
        y = x_ref[...] + 0.5 * acc_ref[...]
        if final_norm:
            y = _rms(y, gf_ref[...])
        o_ref[...] = y


def _ffn(x, g, wg, wu, wd, gf, *, final_norm, name):
    n, d = x.shape
    dff = wg.shape[1]
    tm = min(512, n)
    tf = 512
    assert n % tm == 0 and dff % tf == 0
    return pl.pallas_call(
        functools.partial(_ffn_kernel, final_norm=final_norm),
        grid=(n // tm, dff // tf),
        in_specs=[
            pl.BlockSpec((tm, d), lambda i, f: (i, 0)),
            pl.BlockSpec((1, d), lambda i, f: (0, 0)),
            pl.BlockSpec((d, tf), lambda i, f: (0, f)),
            pl.BlockSpec((d, tf), lambda i, f: (0, f)),
            pl.BlockSpec((tf, d), lambda i, f: (f, 0)),
            pl.BlockSpec((1, d), lambda i, f: (0, 0)),
        ],
        out_specs=pl.BlockSpec((tm, d), lambda i, f: (i, 0)),
        out_shape=jax.ShapeDtypeStruct((n, d), F32),
        scratch_shapes=[pltpu.VMEM((tm, d), BF16), pltpu.VMEM((tm, d), F32)],
        compiler_params=_cparams(("parallel", "arbitrary")),
        name=name,
    )(x, g, wg, wu, wd, gf)


def _inproj_kernel(h_ref, g_ref, w_ref, wf_ref, bf_ref,
                   q_ref, k_ref, v_ref, kb_ref, vb_ref, p_ref, lf_ref, hn_ref, *, scale):
    j = pl.program_id(1)

    @pl.when(j == 0)
    def _():
        hn = _rms(h_ref[...], g_ref[...]).astype(BF16)
        hn_ref[...] = hn
        z = jnp.dot(hn, wf_ref[...], preferred_element_type=F32) + bf_ref[...]
        lf_ref[...] = jax.nn.log_sigmoid(z)

    u = jnp.dot(hn_ref[...], w_ref[...], preferred_element_type=F32)

    @pl.when(j == 0)
    def _():
        q_ref[...] = (u * scale).astype(BF16)

    @pl.when(j == 1)
    def _():
        k_ref[...] = u
        kb_ref[...] = u.astype(BF16)

    @pl.when(j == 2)
    def _():
        v_ref[...] = u
        vb_ref[...] = u.astype(BF16)

    @pl.when(j == 3)
    def _():
        p_ref[...] = u


def _inproj(h, g, w_qkvp, wf, bf, *, name):
    n, d = h.shape
    w = w_qkvp.shape[1] // 4
    tm = min(512, n)
    assert n % tm == 0
    row = lambda i, j: (i, 0)
    return pl.pallas_call(
        functools.partial(_inproj_kernel, scale=HEAD_DIM ** -0.5 * LOG2E),
        grid=(n // tm, 4),
        in_specs=[
            pl.BlockSpec((tm, d), row),
            pl.BlockSpec((1, d), lambda i, j: (0, 0)),
            pl.BlockSpec((d, w), lambda i, j: (0, j)),
            pl.BlockSpec((d, LANES), lambda i, j: (0, 0)),
            pl.BlockSpec((1, LANES), lambda i, j: (0, 0)),
        ],
        out_specs=[
            pl.BlockSpec((tm, w), row), pl.BlockSpec((tm, w), row), pl.BlockSpec((tm, w), row),
            pl.BlockSpec((tm, w), row), pl.BlockSpec((tm, w), row), pl.BlockSpec((tm, w), row),
            pl.BlockSpec((tm, LANES), row),
        ],
        out_shape=[
            jax.ShapeDtypeStruct((n, w), BF16),
            jax.ShapeDtypeStruct((n, w), F32),
            jax.ShapeDtypeStruct((n, w), F32),
            jax.ShapeDtypeStruct((n, w), BF16),
            jax.ShapeDtypeStruct((n, w), BF16),
            jax.ShapeDtypeStruct((n, w), F32),
            jax.ShapeDtypeStruct((n, LANES), F32),
        ],
        scratch_shapes=[pltpu.VMEM((tm, d), BF16)],
        compiler_params=_cparams(("parallel", "arbitrary")),
        name=name,
    )(h, g, w_qkvp, wf, bf)


def _aug_kernel(lf_ref, tri_ref, selq_ref, selk_ref, qx_ref, kx_ref, carry_ref, *, n_heads):
    t = pl.program_id(1)

    @pl.when(t == 0)
    def _():
        carry_ref[...] = jnp.zeros_like(carry_ref)

    lf = lf_ref[0]
    lane = lax.broadcasted_iota(jnp.int32, lf.shape, 1)

    def pick(parts, extra=None):
        out = jnp.zeros_like(lf) if extra is None else extra
        for j in reversed(range(N_SPLIT)):
            out = jnp.where((lane >= j * n_heads) & (lane < (j + 1) * n_heads), parts[j], out)
        return out

    cs = jnp.dot(tri_ref[...], pick(_split3(lf)).astype(BF16), preferred_element_type=F32)
    c = cs
    for j in range(1, N_SPLIT):
        c = c + pltpu.roll(cs, LANES - j * n_heads, axis=1)
    c = c + carry_ref[...]
    carry_ref[...] = c[c.shape[0] - 1:, :]

    crep = c
    for j in range(1, N_SPLIT):
        crep = jnp.where((lane >= j * n_heads) & (lane < (j + 1) * n_heads),
                         pltpu.roll(c, j * n_heads, axis=1), crep)
    ones = jnp.where(lane == N_SPLIT * n_heads, 1.0, 0.0).astype(F32)
    pieces = pick(_split3(crep * LOG2E), extra=ones).astype(BF16)
    qx_ref[0] = jnp.dot(pieces, selq_ref[...], preferred_element_type=F32).astype(BF16)
    kx_ref[0] = jnp.dot(pieces, selk_ref[...], preferred_element_type=F32).astype(BF16)


def _aug_constants(tm, n_heads):
    r = jnp.arange(tm)
    tri = (r[:, None] >= r[None, :]).astype(BF16)
    rows = jnp.arange(LANES)[:, None]
    cols = jnp.arange(n_heads * HEAD_DIM)[None, :]
    head, lane = cols // HEAD_DIM, cols % HEAD_DIM
    one_row = N_SPLIT * n_heads
    is_piece = lambda off: (lane >= off) & (lane < off + N_SPLIT) & (rows == (lane - off) * n_heads + head)
    is_one = lambda off: (lane >= off) & (lane < off + N_SPLIT) & (rows == one_row)
    selq = (is_piece(0) | is_one(N_SPLIT)).astype(BF16)
    selk = is_one(0).astype(BF16) - is_piece(N_SPLIT).astype(BF16)
    return tri, selq, selk


def _aug(lf, *, n_heads, name):
    nseq, length, _ = lf.shape
    tm = _pick_tile(length, 512, 16)
    assert (N_SPLIT * n_heads) < LANES
    w = n_heads * HEAD_DIM
    tri, selq, selk = _aug_constants(tm, n_heads)
    return pl.pallas_call(
        functools.partial(_aug_kernel, n_heads=n_heads),
        grid=(nseq, length // tm),
        in_specs=[
            pl.BlockSpec((1, tm, LANES), lambda b, t: (b, t, 0)),
            pl.BlockSpec((tm, tm), lambda b, t: (0, 0)),
            pl.BlockSpec((LANES, w), lambda b, t: (0, 0)),
            pl.BlockSpec((LANES, w), lambda b, t: (0, 0)),
        ],
        out_specs=[pl.BlockSpec((1, tm, w), lambda b, t: (b, t, 0)),
                   pl.BlockSpec((1, tm, w), lambda b, t: (b, t, 0))],
        out_shape=[jax.ShapeDtypeStruct((nseq, length, w), BF16),
                   jax.ShapeDtypeStruct((nseq, length, w), BF16)],
        scratch_shapes=[pltpu.VMEM((1, LANES), F32)],
        compiler_params=_cparams(("parallel", "arbitrary")),
        name=name,
    )(lf, tri, selq, selk)


def _attn_kernel(q_ref, qx_ref, k_ref, kx_ref, v_ref, o_ref, vt_ref, m_ref, acc_ref, s0_ref, s1_ref, *, tq, tk, q_off, length):
    qi = pl.program_id(2)

    @pl.when(qi == 0)
    def _():
        def transpose_chunk(c, carry):
            st = pl.multiple_of(c * tk, tk)
            vt_ref[0:HEAD_DIM, pl.ds(st, tk)] = v_ref[0, pl.ds(st, tk), :].T
            return carry

        lax.fori_loop(0, length // tk, transpose_chunk, 0)
        sub = lax.broadcasted_iota(jnp.int32, (VT_ROWS - HEAD_DIM, length), 0)
        vt_ref[HEAD_DIM:, :] = jnp.where(sub == 0, 1.0, 0.0).astype(BF16)

    q = jnp.concatenate([q_ref[0], qx_ref[0]], axis=1)
    q_pos0 = q_off + qi * tq
    j_diag = q_pos0 // tk
    m_ref[...] = jnp.full_like(m_ref, -jnp.inf)
    acc_ref[...] = jnp.zeros_like(acc_ref)

    def scores(j):
        st = pl.multiple_of(j * tk, tk)
        k = jnp.concatenate([k_ref[0, pl.ds(st, tk), :], kx_ref[0, pl.ds(st, tk), :]], axis=1)
        return lax.dot_general(k, q, (((1,), (1,)), ((), ())), preferred_element_type=F32)

    def accumulate(j, s, masked):
        st = pl.multiple_of(j * tk, tk)
        if masked:
            key_pos = st + lax.broadcasted_iota(jnp.int32, (tk, tq), 0)
            qry_pos = q_pos0 + lax.broadcasted_iota(jnp.int32, (tk, tq), 1)
            s = jnp.where(key_pos <= qry_pos, s, -jnp.inf)
        m_prev = m_ref[...]
        m_new = jnp.maximum(m_prev, jnp.max(s, axis=0, keepdims=True))
        alpha = jnp.exp2(m_prev - m_new)
        p = jnp.exp2(s - m_new).astype(BF16)
        acc_ref[...] = alpha * acc_ref[...] + jnp.dot(vt_ref[:, pl.ds(st, tk)], p, preferred_element_type=F32)
        m_ref[...] = m_new

    s0_ref[...] = scores(0)

    def pair(i, carry):
        j = 2 * i
        s1_ref[...] = scores(j + 1)
        accumulate(j, s0_ref[...], False)
        s0_ref[...] = scores(j + 2)
        accumulate(j + 1, s1_ref[...], False)
        return carry

    lax.fori_loop(0, j_diag // 2, pair, 0)

    @pl.when(j_diag % 2 == 1)
    def _():
        s1_ref[...] = scores(j_diag)
        accumulate(j_diag - 1, s0_ref[...], False)
        accumulate(j_diag, s1_ref[...], True)

    @pl.when(j_diag % 2 == 0)
    def _():
        accumulate(j_diag, s0_ref[...], True)

    acc = acc_ref[...]
    out_t = acc[0:HEAD_DIM] / acc[HEAD_DIM:HEAD_DIM + 1]
    o_ref[0] = out_t.T.astype(o_ref.dtype)


def _attn(q, qx, k, kx, v, *, q_off, name):
    nb, sq, w = q.shape
    length = k.shape[1]
    n_heads = w // HEAD_DIM
    tq = _pick_tile(sq, 512, LANES)
    tk = _pick_tile(length, 512, LANES)
    assert q_off % tq == 0 and tk % tq == 0 and q_off + sq <= length
    qoff_blk = q_off // tq
    kv_spec = pl.BlockSpec((1, length, HEAD_DIM), lambda b, h, i: (b, 0, h))
    return pl.pallas_call(
        functools.partial(_attn_kernel, tq=tq, tk=tk, q_off=q_off, length=length),
        grid=(nb, n_heads, sq // tq),
        in_specs=[
            pl.BlockSpec((1, tq, HEAD_DIM), lambda b, h, i: (b, i, h)),
            pl.BlockSpec((1, tq, HEAD_DIM), lambda b, h, i: (b, i + qoff_blk, h)),
            kv_spec, kv_spec, kv_spec,
        ],
        out_specs=pl.BlockSpec((1, tq, HEAD_DIM), lambda b, h, i: (b, i, h)),
        out_shape=jax.ShapeDtypeStruct((nb, sq, w), BF16),
        scratch_shapes=[pltpu.VMEM((VT_ROWS, length), BF16), pltpu.VMEM((1, tq), F32),
                        pltpu.VMEM((VT_ROWS, tq), F32), pltpu.VMEM((tk, tq), F32), pltpu.VMEM((tk, tq), F32)],
        compiler_params=_cparams(("parallel", "parallel", "arbitrary")),
        name=name,
    )(q, qx, k, kx, v)


def _pool_kernel(p_ref, hist_ref, w_ref, sc_ref, o_ref, ext_ref, *, tp, pos0):
    t = pl.program_id(1)

    @pl.when(t == 0)
    def _():
        ext_ref[0:HIST_ROWS, :] = hist_ref[0]

    ext_ref[HIST_ROWS:, :] = p_ref[0]
    pos = pos0 + t * tp + lax.broadcasted_iota(jnp.int32, (tp, 1), 0)
    gd = w_ref.shape[1]
    for g, win in enumerate(POOL_WINDOWS):
        cols = slice(g * gd, (g + 1) * gd)
        cur = ext_ref[HIST_ROWS:, cols]
        tot = cur
        for back in range(1, win):
            tot = tot + ext_ref[HIST_ROWS - back:HIST_ROWS - back + tp, cols]
        cnt = jnp.minimum(pos + 1, win).astype(F32)
        d = (tot / cnt - cur).astype(BF16)
        y = jnp.dot(d, w_ref[g], preferred_element_type=F32) * sc_ref[:, cols]
        o_ref[0, :, cols] = y.astype(o_ref.dtype)
    ext_ref[0:HIST_ROWS, :] = ext_ref[tp:tp + HIST_ROWS, :]


def _pool(p, hist, w_pool, scale, *, pos0, name):
    nseq, length, w = p.shape
    tp = min(512, length)
    assert length % tp == 0 and tp >= HIST_ROWS
    return pl.pallas_call(
        functools.partial(_pool_kernel, tp=tp, pos0=pos0),
        grid=(nseq, length // tp),
        in_specs=[
            pl.BlockSpec((1, tp, w), lambda b, t: (b, t, 0)),
            pl.BlockSpec((1, HIST_ROWS, w), lambda b, t: (b, 0, 0)),
            pl.BlockSpec(w_pool.shape, lambda b, t: (0, 0, 0)),
            pl.BlockSpec((1, w), lambda b, t: (0, 0)),
        ],
        out_specs=pl.BlockSpec((1, tp, w), lambda b, t: (b, t, 0)),
        out_shape=jax.ShapeDtypeStruct((nseq, length, w), BF16),
        scratch_shapes=[pltpu.VMEM((HIST_ROWS + tp, w), F32)],
        compiler_params=_cparams(("parallel", "arbitrary")),
        name=name,
    )(p, hist, w_pool, scale)


def _oproj_kernel(h_ref, fox_ref, pool_ref, wo_ref, o_ref):
    mix = jnp.concatenate([fox_ref[...], pool_ref[...]], axis=1)
    o_ref[...] = h_ref[...] + jnp.dot(mix, wo_ref[...], preferred_element_type=F32)


def _oproj(h, fox, pool, wo, *, name):
    n, d = h.shape
    tm = min(512, n)
    assert n % tm == 0
    return pl.pallas_call(
        _oproj_kernel,
        grid=(n // tm,),
        in_specs=[
            pl.BlockSpec((tm, d), lambda i: (i, 0)),
            pl.BlockSpec((tm, fox.shape[1]), lambda i: (i, 0)),
            pl.BlockSpec((tm, pool.shape[1]), lambda i: (i, 0)),
            pl.BlockSpec(wo.shape, lambda i: (0, 0)),
        ],
        out_specs=pl.BlockSpec((tm, d), lambda i: (i, 0)),
        out_shape=jax.ShapeDtypeStruct((n, d), F32),
        compiler_params=_cparams(("parallel",)),
        name=name,
    )(h, fox, pool, wo)


def _round_up(x, m):
    return (x + m - 1) // m * m


def kernel(x_prompt, x_sample, cache_k, cache_v, cache_logf, state_pool, g_ffn1, w1_gate, w1_up, w1_down, g_mix, w_in, b_f, w_pool, pool_scale, w_o, g_ffn2, w2_gate, w2_up, w2_down, g_final):
    B, S, D = x_prompt.shape
    Bs, Tn, _ = x_sample.shape
    depth, _, P, H, Dh = cache_k.shape
    assert depth == 1 and Dh == HEAD_DIM
    fw = H * Dh
    pw = w_pool.shape[1] * w_pool.shape[2]
    assert w_in.shape[2] == 3 * fw + H + pw and state_pool.shape[2] == POOL_HIST

    row = lambda a: a.reshape(1, -1).astype(F32)
    w1g, w1u, w1d = w1_gate[0].astype(BF16), w1_up[0].astype(BF16), w1_down[0].astype(BF16)
    w2g, w2u, w2d = w2_gate[0].astype(BF16), w2_up[0].astype(BF16), w2_down[0].astype(BF16)
    wi = w_in[0]
    w_qkvp = jnp.concatenate([wi[:, :3 * fw], wi[:, 3 * fw + H:]], axis=1).astype(BF16)
    wf_cols = jnp.tile(wi[:, 3 * fw:3 * fw + H], (1, N_SPLIT))
    wf = jnp.pad(wf_cols, ((0, 0), (0, LANES - N_SPLIT * H))).astype(BF16)
    bf = jnp.pad(jnp.tile(b_f[0].astype(F32), N_SPLIT), (0, LANES - N_SPLIT * H)).reshape(1, LANES)
    wo = w_o[0].astype(BF16)
    wp = w_pool[0].astype(BF16)
    g1, gm, g2, gfin, psc = row(g_ffn1[0]), row(g_mix[0]), row(g_ffn2[0]), row(g_final), row(pool_scale[0])

    def pre(x2d, tag):
        h = _ffn(x2d, g1, w1g, w1u, w1d, gfin, final_norm=False, name=f"ffn1_{tag}")
        return (h,) + tuple(_inproj(h, gm, w_qkvp, wf, bf, name=f"inproj_{tag}"))

    def post(h, fox, pool, tag):
        h2 = _oproj(h, fox, pool, wo, name=f"oproj_{tag}")
        return _ffn(h2, g2, w2g, w2u, w2d, gfin, final_norm=True, name=f"ffn2_{tag}")

    h, q, k, v, kb, vb, p, lf = pre(x_prompt.reshape(B * S, D), "prompt")
    qx, kx = _aug(lf.reshape(B, S, LANES), n_heads=H, name="aug_prompt")
    fox = _attn(q.reshape(B, S, fw), qx, kb.reshape(B, S, fw), kx, vb.reshape(B, S, fw),
                q_off=0, name="attn_prompt")
    p3 = p.reshape(B, S, pw)
    pool = _pool(p3, jnp.zeros((B, HIST_ROWS, pw), F32), wp, psc, pos0=0, name="pool_prompt")
    y_prompt = post(h, fox.reshape(B * S, fw), pool.reshape(B * S, pw), "prompt").reshape(B, S, D)
    k_prompt = k.reshape(1, B, S, H, Dh)
    v_prompt = v.reshape(1, B, S, H, Dh)
    logf_prompt = lf[:, :H].reshape(1, B, S, H)
    pool_prompt = p3[:, S - POOL_HIST:][None]

    h, q, k, v, kb, vb, p, lf = pre(x_sample.reshape(Bs * Tn, D), "sample")
    L = _round_up(P + Tn, LANES)
    tail = L - P - Tn
    lf_cache = jnp.pad(jnp.tile(cache_logf[0].astype(F32), (1, 1, N_SPLIT)),
                       ((0, 0), (0, 0), (0, LANES - N_SPLIT * H)))
    lf_all = jnp.concatenate([lf_cache, lf.reshape(Bs, Tn, LANES), jnp.zeros((Bs, tail, LANES), F32)], axis=1)
    qx, kx = _aug(lf_all, n_heads=H, name="aug_sample")
    k_all = jnp.concatenate([cache_k[0].reshape(Bs, P, fw).astype(BF16), kb.reshape(Bs, Tn, fw),
                             jnp.zeros((Bs, tail, fw), BF16)], axis=1)
    v_all = jnp.concatenate([cache_v[0].reshape(Bs, P, fw).astype(BF16), vb.reshape(Bs, Tn, fw),
                             jnp.zeros((Bs, tail, fw), BF16)], axis=1)
    q_pad = jnp.pad(q.reshape(Bs, Tn, fw), ((0, 0), (0, _round_up(Tn, LANES) - Tn), (0, 0)))
    fox = _attn(q_pad, qx, k_all, kx, v_all, q_off=P, name="attn_sample")[:, :Tn]
    p3 = p.reshape(Bs, Tn, pw)
    hist = jnp.concatenate([jnp.zeros((Bs, HIST_ROWS - POOL_HIST, pw), F32), state_pool[0].astype(F32)], axis=1)
    pool = _pool(p3, hist, wp, psc, pos0=P, name="pool_sample")
    y_sample = post(h, fox.reshape(Bs * Tn, fw), pool.reshape(Bs * Tn, pw), "sample").reshape(Bs, Tn, D)
    k_sample = k.reshape(1, Bs, Tn, H, Dh)
    v_sample = v.reshape(1, Bs, Tn, H, Dh)
    logf_sample = lf[:, :H].reshape(1, Bs, Tn, H)
    ext_tail = jnp.concatenate([state_pool[0].astype(F32), p3], axis=1)[:, Tn:]
    pool_sample = ext_tail[None]

    return (y_prompt, y_sample, k_prompt, v_prompt, logf_prompt, pool_prompt,
            k_sample, v_sample, logf_sample, pool_sample)
```

```python
import functools

import jax
import jax.numpy as jnp
from jax import lax
from jax.experimental import pallas as pl
from jax.experimental.pallas import tpu as pltpu

F32 = jnp.float32
BF16 = jnp.bfloat16

EPS = 1e-6
HEAD_DIM = 128
LANES = 128
POOL_WINDOWS = (2, 4, 8, 16)
POOL_HIST = max(POOL_WINDOWS) - 1
HIST_ROWS = 16
N_SPLIT = 3
VT_ROWS = HEAD_DIM + 16
LOG2E = 1.4426950408889634
VMEM_LIMIT = 56 * 1024 * 1024
ATTN_UNROLL = 4
FFN_ROWS = 1024
FFN_COLS = 256


def _cparams(sem):
    return pltpu.CompilerParams(dimension_semantics=sem, vmem_limit_bytes=VMEM_LIMIT)


def _pick_tile(length, cap, mult):
    best = None
    for t in range(mult, min(cap, length) + 1, mult):
        if length % t == 0:
            best = t
    assert best is not None, (length, cap, mult)
    return best


def _rms(x, g):
    return x * lax.rsqrt(jnp.mean(x * x, axis=-1, keepdims=True) + EPS) * g


def _split3(x):
    hi = x.astype(BF16).astype(F32)
    r1 = x - hi
    mid = r1.astype(BF16).astype(F32)
    lo = (r1 - mid).astype(BF16).astype(F32)
    return hi, mid, lo


def _ffn_kernel(x_ref, g_ref, wg_ref, wu_ref, wd_ref, gf_ref, o_ref, xn_ref, *, final_norm):
    f = pl.program_id(1)

    @pl.when(f == 0)
    def _():
        xn_ref[...] = _rms(x_ref[...], g_ref[...]).astype(BF16)
        o_ref[...] = jnp.zeros_like(o_ref)

    xn = xn_ref[...]
    gate = jnp.dot(xn, wg_ref[...], preferred_element_type=F32)
    up = jnp.dot(xn, wu_ref[...], preferred_element_type=F32)
    a = (gate * jax.nn.sigmoid(gate) * up).astype(BF16)
    o_ref[...] += jnp.dot(a, wd_ref[...], preferred_element_type=F32)

    @pl.when(f == pl.num_programs(1) - 1)
    def _():
        y = x_ref[...] + 0.5 * o_ref[...]
        if final_norm:
            y = _rms(y, gf_ref[...])
        o_ref[...] = y


def _ffn(x, g, wg, wu, wd, gf, *, final_norm, name):
    n, d = x.shape
    dff = wg.shape[1]
    tm = min(FFN_ROWS, n)
    tf = FFN_COLS
    assert n % tm == 0 and dff % tf == 0
    return pl.pallas_call(
        functools.partial(_ffn_kernel, final_norm=final_norm),
        grid=(n // tm, dff // tf),
        in_specs=[
            pl.BlockSpec((tm, d), lambda i, f: (i, 0)),
            pl.BlockSpec((1, d), lambda i, f: (0, 0)),
            pl.BlockSpec((d, tf), lambda i, f: (0, f)),
            pl.BlockSpec((d, tf), lambda i, f: (0, f)),
            pl.BlockSpec((tf, d), lambda i, f: (f, 0)),
            pl.BlockSpec((1, d), lambda i, f: (0, 0)),
        ],
        out_specs=pl.BlockSpec((tm, d), lambda i, f: (i, 0)),
        out_shape=jax.ShapeDtypeStruct((n, d), F32),
        scratch_shapes=[pltpu.VMEM((tm, d), BF16)],
        compiler_params=_cparams(("parallel", "arbitrary")),
        name=name,
    )(x, g, wg, wu, wd, gf)


def _inproj_kernel(h_ref, g_ref, w_ref, wf_ref, bf_ref,
                   q_ref, k_ref, v_ref, kb_ref, vb_ref, p_ref, lf_ref, hn_ref, *, scale):
    j = pl.program_id(1)

    @pl.when(j == 0)
    def _():
        hn = _rms(h_ref[...], g_ref[...]).astype(BF16)
        hn_ref[...] = hn
        z = jnp.dot(hn, wf_ref[...], preferred_element_type=F32) + bf_ref[...]
        lf_ref[...] = jax.nn.log_sigmoid(z)

    u = jnp.dot(hn_ref[...], w_ref[...], preferred_element_type=F32)

    @pl.when(j == 0)
    def _():
        q_ref[...] = (u * scale).astype(BF16)

    def store_heads(o_ref):
        n_heads = u.shape[1] // HEAD_DIM
        for hd in range(n_heads):
            o_ref[pl.ds(hd, u.shape[0], stride=n_heads), :] = u[:, hd * HEAD_DIM:(hd + 1) * HEAD_DIM]

    @pl.when(j == 1)
    def _():
        store_heads(k_ref)
        kb_ref[...] = u.astype(BF16)

    @pl.when(j == 2)
    def _():
        store_heads(v_ref)
        vb_ref[...] = u.astype(BF16)

    @pl.when(j == 3)
    def _():
        p_ref[...] = u


def _inproj(h, g, w_qkvp, wf, bf, *, name):
    n, d = h.shape
    w = w_qkvp.shape[1] // 4
    tm = min(512, n)
    assert n % tm == 0 and w % HEAD_DIM == 0
    n_heads = w // HEAD_DIM
    row = lambda i, j: (i, 0)
    return pl.pallas_call(
        functools.partial(_inproj_kernel, scale=HEAD_DIM ** -0.5 * LOG2E),
        grid=(n // tm, 4),
        in_specs=[
            pl.BlockSpec((tm, d), row),
            pl.BlockSpec((1, d), lambda i, j: (0, 0)),
            pl.BlockSpec((d, w), lambda i, j: (0, j)),
            pl.BlockSpec((d, LANES), lambda i, j: (0, 0)),
            pl.BlockSpec((1, LANES), lambda i, j: (0, 0)),
        ],
        out_specs=[
            pl.BlockSpec((tm, w), row),
            pl.BlockSpec((tm * n_heads, HEAD_DIM), row), pl.BlockSpec((tm * n_heads, HEAD_DIM), row),
            pl.BlockSpec((tm, w), row), pl.BlockSpec((tm, w), row), pl.BlockSpec((tm, w), row),
            pl.BlockSpec((tm, LANES), row),
        ],
        out_shape=[
            jax.ShapeDtypeStruct((n, w), BF16),
            jax.ShapeDtypeStruct((n * n_heads, HEAD_DIM), F32),
            jax.ShapeDtypeStruct((n * n_heads, HEAD_DIM), F32),
            jax.ShapeDtypeStruct((n, w), BF16),
            jax.ShapeDtypeStruct((n, w), BF16),
            jax.ShapeDtypeStruct((n, w), F32),
            jax.ShapeDtypeStruct((n, LANES), F32),
        ],
        scratch_shapes=[pltpu.VMEM((tm, d), BF16)],
        compiler_params=_cparams(("parallel", "arbitrary")),
        name=name,
    )(h, g, w_qkvp, wf, bf)


def _aug_kernel(lf_ref, tri_ref, selq_ref, selk_ref, qx_ref, kx_ref, carry_ref, *, n_heads):
    t = pl.program_id(1)

    @pl.when(t == 0)
    def _():
        carry_ref[...] = jnp.zeros_like(carry_ref)

    lf = lf_ref[0]
    lane = lax.broadcasted_iota(jnp.int32, lf.shape, 1)

    def pick(parts, extra=None):
        out = jnp.zeros_like(lf) if extra is None else extra
        for j in reversed(range(N_SPLIT)):
            out = jnp.where((lane >= j * n_heads) & (lane < (j + 1) * n_heads), parts[j], out)
        return out

    cs = jnp.dot(tri_ref[...], pick(_split3(lf)).astype(BF16), preferred_element_type=F32)
    c = cs
    for j in range(1, N_SPLIT):
        c = c + pltpu.roll(cs, LANES - j * n_heads, axis=1)
    c = c + carry_ref[...]
    carry_ref[...] = c[c.shape[0] - 1:, :]

    crep = c
    for j in range(1, N_SPLIT):
        crep = jnp.where((lane >= j * n_heads) & (lane < (j + 1) * n_heads),
                         pltpu.roll(c, j * n_heads, axis=1), crep)
    ones = jnp.where(lane == N_SPLIT * n_heads, 1.0, 0.0).astype(F32)
    pieces = pick(_split3(crep * LOG2E), extra=ones).astype(BF16)
    qx_ref[0] = jnp.dot(pieces, selq_ref[...], preferred_element_type=F32).astype(BF16)
    kx_ref[0] = jnp.dot(pieces, selk_ref[...], preferred_element_type=F32).astype(BF16)


def _aug_constants(tm, n_heads):
    r = jnp.arange(tm)
    tri = (r[:, None] >= r[None, :]).astype(BF16)
    rows = jnp.arange(LANES)[:, None]
    cols = jnp.arange(n_heads * HEAD_DIM)[None, :]
    head, lane = cols // HEAD_DIM, cols % HEAD_DIM
    one_row = N_SPLIT * n_heads
    is_piece = lambda off: (lane >= off) & (lane < off + N_SPLIT) & (rows == (lane - off) * n_heads + head)
    is_one = lambda off: (lane >= off) & (lane < off + N_SPLIT) & (rows == one_row)
    selq = (is_piece(0) | is_one(N_SPLIT)).astype(BF16)
    selk = is_one(0).astype(BF16) - is_piece(N_SPLIT).astype(BF16)
    return tri, selq, selk


def _aug(lf, *, n_heads, name):
    nseq, length, _ = lf.shape
    tm = _pick_tile(length, 512, 16)
    assert (N_SPLIT * n_heads) < LANES
    w = n_heads * HEAD_DIM
    tri, selq, selk = _aug_constants(tm, n_heads)
    return pl.pallas_call(
        functools.partial(_aug_kernel, n_heads=n_heads),
        grid=(nseq, length // tm),
        in_specs=[
            pl.BlockSpec((1, tm, LANES), lambda b, t: (b, t, 0)),
            pl.BlockSpec((tm, tm), lambda b, t: (0, 0)),
            pl.BlockSpec((LANES, w), lambda b, t: (0, 0)),
            pl.BlockSpec((LANES, w), lambda b, t: (0, 0)),
        ],
        out_specs=[pl.BlockSpec((1, tm, w), lambda b, t: (b, t, 0)),
                   pl.BlockSpec((1, tm, w), lambda b, t: (b, t, 0))],
        out_shape=[jax.ShapeDtypeStruct((nseq, length, w), BF16),
                   jax.ShapeDtypeStruct((nseq, length, w), BF16)],
        scratch_shapes=[pltpu.VMEM((1, LANES), F32)],
        compiler_params=_cparams(("parallel", "arbitrary")),
        name=name,
    )(lf, tri, selq, selk)


def _attn_kernel(q_ref, qx_ref, k_ref, kx_ref, v_ref, o_ref, vt_ref, m_ref, acc_ref, s0_ref, s1_ref, *, tq, tk, q_off, length):
    qi = pl.program_id(2)

    @pl.when(qi == 0)
    def _():
        def transpose_chunk(c, carry):
            st = pl.multiple_of(c * tk, tk)
            vt_ref[0:HEAD_DIM, pl.ds(st, tk)] = v_ref[0, pl.ds(st, tk), :].T
            return carry

        lax.fori_loop(0, length // tk, transpose_chunk, 0)
        sub = lax.broadcasted_iota(jnp.int32, (VT_ROWS - HEAD_DIM, length), 0)
        vt_ref[HEAD_DIM:, :] = jnp.where(sub == 0, 1.0, 0.0).astype(BF16)

    q = jnp.concatenate([q_ref[0], qx_ref[0]], axis=1)
    q_pos0 = q_off + qi * tq
    j_diag = q_pos0 // tk
    m_ref[...] = jnp.full_like(m_ref, -jnp.inf)
    acc_ref[...] = jnp.zeros_like(acc_ref)

    def scores(j):
        st = pl.multiple_of(j * tk, tk)
        k = jnp.concatenate([k_ref[0, pl.ds(st, tk), :], kx_ref[0, pl.ds(st, tk), :]], axis=1)
        return lax.dot_general(k, q, (((1,), (1,)), ((), ())), preferred_element_type=F32)

    def accumulate(j, s, masked):
        st = pl.multiple_of(j * tk, tk)
        if masked:
            key_pos = st + lax.broadcasted_iota(jnp.int32, (tk, tq), 0)
            qry_pos = q_pos0 + lax.broadcasted_iota(jnp.int32, (tk, tq), 1)
            s = jnp.where(key_pos <= qry_pos, s, -jnp.inf)
        m_prev = m_ref[...]
        m_new = jnp.maximum(m_prev, jnp.max(s, axis=0, keepdims=True))
        alpha = jnp.exp2(m_prev - m_new)
        p = jnp.exp2(s - m_new).astype(BF16)
        acc_ref[...] = alpha * acc_ref[...] + jnp.dot(vt_ref[:, pl.ds(st, tk)], p, preferred_element_type=F32)
        m_ref[...] = m_new

    bufs = (s0_ref, s1_ref)

    def run_full(j0, n):
        for u in range(n):
            bufs[(u + 1) % 2][...] = scores(j0 + u + 1)
            accumulate(j0 + u, bufs[u % 2][...], False)

    s0_ref[...] = scores(0)

    def group(i, carry):
        run_full(ATTN_UNROLL * i, ATTN_UNROLL)
        return carry

    lax.fori_loop(0, j_diag // ATTN_UNROLL, group, 0)
    rem = j_diag % ATTN_UNROLL
    for r in range(ATTN_UNROLL):
        @pl.when(rem == r)
        def _(r=r):
            run_full(j_diag - r, r)
            accumulate(j_diag, bufs[r % 2][...], True)

    acc = acc_ref[...]
    out_t = acc[0:HEAD_DIM] / acc[HEAD_DIM:HEAD_DIM + 1]
    o_ref[0] = out_t.T.astype(o_ref.dtype)


def _attn(q, qx, k, kx, v, *, q_off, name):
    nb, sq, w = q.shape
    length = k.shape[1]
    n_heads = w // HEAD_DIM
    tq = _pick_tile(sq, 512, LANES)
    tk = _pick_tile(length, 512, LANES)
    assert q_off % tq == 0 and tk % tq == 0 and q_off + sq <= length
    qoff_blk = q_off // tq
    kv_spec = pl.BlockSpec((1, length, HEAD_DIM), lambda b, h, i: (b, 0, h))
    return pl.pallas_call(
        functools.partial(_attn_kernel, tq=tq, tk=tk, q_off=q_off, length=length),
        grid=(nb, n_heads, sq // tq),
        in_specs=[
            pl.BlockSpec((1, tq, HEAD_DIM), lambda b, h, i: (b, i, h)),
            pl.BlockSpec((1, tq, HEAD_DIM), lambda b, h, i: (b, i + qoff_blk, h)),
            kv_spec, kv_spec, kv_spec,
        ],
        out_specs=pl.BlockSpec((1, tq, HEAD_DIM), lambda b, h, i: (b, i, h)),
        out_shape=jax.ShapeDtypeStruct((nb, sq, w), BF16),
        scratch_shapes=[pltpu.VMEM((VT_ROWS, length), BF16), pltpu.VMEM((1, tq), F32),
                        pltpu.VMEM((VT_ROWS, tq), F32), pltpu.VMEM((tk, tq), F32), pltpu.VMEM((tk, tq), F32)],
        compiler_params=_cparams(("parallel", "parallel", "arbitrary")),
        name=name,
    )(q, qx, k, kx, v)


def _pool_kernel(p_ref, hist_ref, w_ref, sc_ref, o_ref, ext_ref, *, tp, pos0):
    t = pl.program_id(1)

    @pl.when(t == 0)
    def _():
        ext_ref[0:HIST_ROWS, :] = hist_ref[0]

    ext_ref[HIST_ROWS:, :] = p_ref[0]
    pos = pos0 + t * tp + lax.broadcasted_iota(jnp.int32, (tp, 1), 0)
    gd = w_ref.shape[1]
    for g, win in enumerate(POOL_WINDOWS):
        cols = slice(g * gd, (g + 1) * gd)
        cur = ext_ref[HIST_ROWS:, cols]
        tot = cur
        for back in range(1, win):
            tot = tot + ext_ref[HIST_ROWS - back:HIST_ROWS - back + tp, cols]
        cnt = jnp.minimum(pos + 1, win).astype(F32)
        d = (tot / cnt - cur).astype(BF16)
        y = jnp.dot(d, w_ref[g], preferred_element_type=F32) * sc_ref[:, cols]
        o_ref[0, :, cols] = y.astype(o_ref.dtype)
    ext_ref[0:HIST_ROWS, :] = ext_ref[tp:tp + HIST_ROWS, :]


def _pool(p, hist, w_pool, scale, *, pos0, name):
    nseq, length, w = p.shape
    tp = min(512, length)
    assert length % tp == 0 and tp >= HIST_ROWS
    return pl.pallas_call(
        functools.partial(_pool_kernel, tp=tp, pos0=pos0),
        grid=(nseq, length // tp),
        in_specs=[
            pl.BlockSpec((1, tp, w), lambda b, t: (b, t, 0)),
            pl.BlockSpec((1, HIST_ROWS, w), lambda b, t: (b, 0, 0)),
            pl.BlockSpec(w_pool.shape, lambda b, t: (0, 0, 0)),
            pl.BlockSpec((1, w), lambda b, t: (0, 0)),
        ],
        out_specs=pl.BlockSpec((1, tp, w), lambda b, t: (b, t, 0)),
        out_shape=jax.ShapeDtypeStruct((nseq, length, w), BF16),
        scratch_shapes=[pltpu.VMEM((HIST_ROWS + tp, w), F32)],
        compiler_params=_cparams(("parallel", "arbitrary")),
        name=name,
    )(p, hist, w_pool, scale)


def _oproj_kernel(h_ref, fox_ref, pool_ref, wo_ref, o_ref):
    mix = jnp.concatenate([fox_ref[...], pool_ref[...]], axis=1)
    o_ref[...] = h_ref[...] + jnp.dot(mix, wo_ref[...], preferred_element_type=F32)


def _oproj(h, fox, pool, wo, *, name):
    n, d = h.shape
    tm = min(512, n)
    assert n % tm == 0
    return pl.pallas_call(
        _oproj_kernel,
        grid=(n // tm,),
        in_specs=[
            pl.BlockSpec((tm, d), lambda i: (i, 0)),
            pl.BlockSpec((tm, fox.shape[1]), lambda i: (i, 0)),
            pl.BlockSpec((tm, pool.shape[1]), lambda i: (i, 0)),
            pl.BlockSpec(wo.shape, lambda i: (0, 0)),
        ],
        out_specs=pl.BlockSpec((tm, d), lambda i: (i, 0)),
        out_shape=jax.ShapeDtypeStruct((n, d), F32),
        compiler_params=_cparams(("parallel",)),
        name=name,
    )(h, fox, pool, wo)


def _round_up(x, m):
    return (x + m - 1) // m * m


def kernel(x_prompt, x_sample, cache_k, cache_v, cache_logf, state_pool, g_ffn1, w1_gate, w1_up, w1_down, g_mix, w_in, b_f, w_pool, pool_scale, w_o, g_ffn2, w2_gate, w2_up, w2_down, g_final):
    B, S, D = x_prompt.shape
    Bs, Tn, _ = x_sample.shape
    depth, _, P, H, Dh = cache_k.shape
    assert depth == 1 and Dh == HEAD_DIM
    fw = H * Dh
    pw = w_pool.shape[1] * w_pool.shape[2]
    assert w_in.shape[2] == 3 * fw + H + pw and state_pool.shape[2] == POOL_HIST

    row = lambda a: a.reshape(1, -1).astype(F32)
    w1g, w1u, w1d = w1_gate[0].astype(BF16), w1_up[0].astype(BF16), w1_down[0].astype(BF16)
    w2g, w2u, w2d = w2_gate[0].astype(BF16), w2_up[0].astype(BF16), w2_down[0].astype(BF16)
    wi = w_in[0]
    w_qkvp = jnp.concatenate([wi[:, :3 * fw], wi[:, 3 * fw + H:]], axis=1).astype(BF16)
    wf_cols = jnp.tile(wi[:, 3 * fw:3 * fw + H], (1, N_SPLIT))
    wf = jnp.pad(wf_cols, ((0, 0), (0, LANES - N_SPLIT * H))).astype(BF16)
    bf = jnp.pad(jnp.tile(b_f[0].astype(F32), N_SPLIT), (0, LANES - N_SPLIT * H)).reshape(1, LANES)
    wo = w_o[0].astype(BF16)
    wp = w_pool[0].astype(BF16)
    g1, gm, g2, gfin, psc = row(g_ffn1[0]), row(g_mix[0]), row(g_ffn2[0]), row(g_final), row(pool_scale[0])

    def pre(x2d, tag):
        h = _ffn(x2d, g1, w1g, w1u, w1d, gfin, final_norm=False, name=f"ffn1_{tag}")
        return (h,) + tuple(_inproj(h, gm, w_qkvp, wf, bf, name=f"inproj_{tag}"))

    def post(h, fox, pool, tag):
        h2 = _oproj(h, fox, pool, wo, name=f"oproj_{tag}")
        return _ffn(h2, g2, w2g, w2u, w2d, gfin, final_norm=True, name=f"ffn2_{tag}")

    h, q, k, v, kb, vb, p, lf = pre(x_prompt.reshape(B * S, D), "prompt")
    qx, kx = _aug(lf.reshape(B, S, LANES), n_heads=H, name="aug_prompt")
    fox = _attn(q.reshape(B, S, fw), qx, kb.reshape(B, S, fw), kx, vb.reshape(B, S, fw),
                q_off=0, name="attn_prompt")
    p3 = p.reshape(B, S, pw)
    pool = _pool(p3, jnp.zeros((B, HIST_ROWS, pw), F32), wp, psc, pos0=0, name="pool_prompt")
    y_prompt = post(h, fox.reshape(B * S, fw), pool.reshape(B * S, pw), "prompt").reshape(B, S, D)
    k_prompt = k.reshape(1, B, S, H, Dh)
    v_prompt = v.reshape(1, B, S, H, Dh)
    logf_prompt = lf[:, :H].reshape(1, B, S, H)
    pool_prompt = p3[:, S - POOL_HIST:][None]

    h, q, k, v, kb, vb, p, lf = pre(x_sample.reshape(Bs * Tn, D), "sample")
    L = _round_up(P + Tn, LANES)
    tail = L - P - Tn
    lf_cache = jnp.pad(jnp.tile(cache_logf[0].astype(F32), (1, 1, N_SPLIT)),
                       ((0, 0), (0, 0), (0, LANES - N_SPLIT * H)))
    lf_all = jnp.concatenate([lf_cache, lf.reshape(Bs, Tn, LANES), jnp.zeros((Bs, tail, LANES), F32)], axis=1)
    qx, kx = _aug(lf_all, n_heads=H, name="aug_sample")
    k_all = jnp.concatenate([cache_k[0].reshape(Bs, P, fw).astype(BF16), kb.reshape(Bs, Tn, fw),
                             jnp.zeros((Bs, tail, fw), BF16)], axis=1)
    v_all = jnp.concatenate([cache_v[0].reshape(Bs, P, fw).astype(BF16), vb.reshape(Bs, Tn, fw),
                             jnp.zeros((Bs, tail, fw), BF16)], axis=1)
    q_pad = jnp.pad(q.reshape(Bs, Tn, fw), ((0, 0), (0, _round_up(Tn, LANES) - Tn), (0, 0)))
    fox = _attn(q_pad, qx, k_all, kx, v_all, q_off=P, name="attn_sample")[:, :Tn]
    p3 = p.reshape(Bs, Tn, pw)
    hist = jnp.concatenate([jnp.zeros((Bs, HIST_ROWS - POOL_HIST, pw), F32), state_pool[0].astype(F32)], axis=1)
    pool = _pool(p3, hist, wp, psc, pos0=P, name="pool_sample")
    y_sample = post(h, fox.reshape(Bs * Tn, fw), pool.reshape(Bs * Tn, pw), "sample").reshape(Bs, Tn, D)
    k_sample = k.reshape(1, Bs, Tn, H, Dh)
    v_sample = v.reshape(1, Bs, Tn, H, Dh)
    logf_sample = lf[:, :H].reshape(1, Bs, Tn, H)
    ext_tail = jnp.concatenate([state_pool[0].astype(F32), p3], axis=1)[:, Tn:]
    pool_sample = ext_tail[None]

    return (y_prompt, y_sample, k_prompt, v_prompt, logf_prompt, pool_prompt,
            k_sample, v_sample, logf_sample, pool_sample)
```

```python
import functools

import jax
import jax.numpy as jnp
from jax import lax
from jax.experimental import pallas as pl
from jax.experimental.pallas import tpu as pltpu

F32 = jnp.float32
BF16 = jnp.bfloat16

EPS = 1e-6
HEAD_DIM = 128
LANES = 128
POOL_WINDOWS = (2, 4, 8, 16)
POOL_HIST = max(POOL_WINDOWS) - 1
HIST_ROWS = 16
N_SPLIT = 3
VT_ROWS = HEAD_DIM + 16
LOG2E = 1.4426950408889634
VMEM_LIMIT = 56 * 1024 * 1024
ATTN_UNROLL = 4
FFN_ROWS = 1024
FFN_COLS = 256


def _cparams(sem):
    return pltpu.CompilerParams(dimension_semantics=sem, vmem_limit_bytes=VMEM_LIMIT)


def _pick_tile(length, cap, mult):
    best = None
    for t in range(mult, min(cap, length) + 1, mult):
        if length % t == 0:
            best = t
    assert best is not None, (length, cap, mult)
    return best


def _rms(x, g):
    return x * lax.rsqrt(jnp.mean(x * x, axis=-1, keepdims=True) + EPS) * g


def _split3(x):
    hi = x.astype(BF16).astype(F32)
    r1 = x - hi
    mid = r1.astype(BF16).astype(F32)
    lo = (r1 - mid).astype(BF16).astype(F32)
    return hi, mid, lo


def _ffn_kernel(x_ref, g_ref, wg_ref, wu_ref, wd_ref, gf_ref, o_ref, xn_ref, *, final_norm):
    f = pl.program_id(1)

    @pl.when(f == 0)
    def _():
        xn_ref[...] = _rms(x_ref[...], g_ref[...]).astype(BF16)
        o_ref[...] = jnp.zeros_like(o_ref)

    xn = xn_ref[...]
    gate = jnp.dot(xn, wg_ref[...], preferred_element_type=F32)
    up = jnp.dot(xn, wu_ref[...], preferred_element_type=F32)
    a = (gate * jax.nn.sigmoid(gate) * up).astype(BF16)
    o_ref[...] += jnp.dot(a, wd_ref[...], preferred_element_type=F32)

    @pl.when(f == pl.num_programs(1) - 1)
    def _():
        y = x_ref[...] + 0.5 * o_ref[...]
        if final_norm:
            y = _rms(y, gf_ref[...])
        o_ref[...] = y


def _ffn(x, g, wg, wu, wd, gf, *, final_norm, name):
    n, d = x.shape
    dff = wg.shape[1]
    tm = min(FFN_ROWS, n)
    tf = FFN_COLS
    assert n % tm == 0 and dff % tf == 0
    return pl.pallas_call(
        functools.partial(_ffn_kernel, final_norm=final_norm),
        grid=(n // tm, dff // tf),
        in_specs=[
            pl.BlockSpec((tm, d), lambda i, f: (i, 0)),
            pl.BlockSpec((1, d), lambda i, f: (0, 0)),
            pl.BlockSpec((d, tf), lambda i, f: (0, f)),
            pl.BlockSpec((d, tf), lambda i, f: (0, f)),
            pl.BlockSpec((tf, d), lambda i, f: (f, 0)),
            pl.BlockSpec((1, d), lambda i, f: (0, 0)),
        ],
        out_specs=pl.BlockSpec((tm, d), lambda i, f: (i, 0)),
        out_shape=jax.ShapeDtypeStruct((n, d), F32),
        scratch_shapes=[pltpu.VMEM((tm, d), BF16)],
        compiler_params=_cparams(("parallel", "arbitrary")),
        name=name,
    )(x, g, wg, wu, wd, gf)


def _inproj_kernel(h_ref, g_ref, w_ref, wf_ref, bf_ref,
                   q_ref, k_ref, v_ref, kb_ref, vb_ref, p_ref, lf_ref, hn_ref, *, scale):
    j = pl.program_id(1)

    @pl.when(j == 0)
    def _():
        hn = _rms(h_ref[...], g_ref[...]).astype(BF16)
        hn_ref[...] = hn
        z = jnp.dot(hn, wf_ref[...], preferred_element_type=F32) + bf_ref[...]
        lf_ref[...] = jax.nn.log_sigmoid(z)

    u = jnp.dot(hn_ref[...], w_ref[...], preferred_element_type=F32)

    @pl.when(j == 0)
    def _():
        q_ref[...] = (u * scale).astype(BF16)

    def store_heads(o_ref):
        n_heads = u.shape[1] // HEAD_DIM
        for hd in range(n_heads):
            o_ref[pl.ds(hd, u.shape[0], stride=n_heads), :] = u[:, hd * HEAD_DIM:(hd + 1) * HEAD_DIM]

    @pl.when(j == 1)
    def _():
        store_heads(k_ref)
        kb_ref[...] = u.astype(BF16)

    @pl.when(j == 2)
    def _():
        store_heads(v_ref)
        vb_ref[...] = u.astype(BF16)

    @pl.when(j == 3)
    def _():
        p_ref[...] = u


def _inproj(h, g, w_qkvp, wf, bf, *, name):
    n, d = h.shape
    w = w_qkvp.shape[1] // 4
    tm = min(512, n)
    assert n % tm == 0 and w % HEAD_DIM == 0
    n_heads = w // HEAD_DIM
    row = lambda i, j: (i, 0)
    return pl.pallas_call(
        functools.partial(_inproj_kernel, scale=HEAD_DIM ** -0.5 * LOG2E),
        grid=(n // tm, 4),
        in_specs=[
            pl.BlockSpec((tm, d), row),
            pl.BlockSpec((1, d), lambda i, j: (0, 0)),
            pl.BlockSpec((d, w), lambda i, j: (0, j)),
            pl.BlockSpec((d, LANES), lambda i, j: (0, 0)),
            pl.BlockSpec((1, LANES), lambda i, j: (0, 0)),
        ],
        out_specs=[
            pl.BlockSpec((tm, w), row),
            pl.BlockSpec((tm * n_heads, HEAD_DIM), row), pl.BlockSpec((tm * n_heads, HEAD_DIM), row),
            pl.BlockSpec((tm, w), row), pl.BlockSpec((tm, w), row), pl.BlockSpec((tm, w), row),
            pl.BlockSpec((tm, LANES), row),
        ],
        out_shape=[
            jax.ShapeDtypeStruct((n, w), BF16),
            jax.ShapeDtypeStruct((n * n_heads, HEAD_DIM), F32),
            jax.ShapeDtypeStruct((n * n_heads, HEAD_DIM), F32),
            jax.ShapeDtypeStruct((n, w), BF16),
            jax.ShapeDtypeStruct((n, w), BF16),
            jax.ShapeDtypeStruct((n, w), F32),
            jax.ShapeDtypeStruct((n, LANES), F32),
        ],
        scratch_shapes=[pltpu.VMEM((tm, d), BF16)],
        compiler_params=_cparams(("parallel", "arbitrary")),
        name=name,
    )(h, g, w_qkvp, wf, bf)


def _aug_kernel(lf_ref, tri_ref, selq_ref, selk_ref, qx_ref, kx_ref, carry_ref, *, n_heads):
    t = pl.program_id(1)

    @pl.when(t == 0)
    def _():
        carry_ref[...] = jnp.zeros_like(carry_ref)

    lf = lf_ref[0]
    lane = lax.broadcasted_iota(jnp.int32, lf.shape, 1)

    def pick(parts, extra=None):
        out = jnp.zeros_like(lf) if extra is None else extra
        for j in reversed(range(N_SPLIT)):
            out = jnp.where((lane >= j * n_heads) & (lane < (j + 1) * n_heads), parts[j], out)
        return out

    cs = jnp.dot(tri_ref[...], pick(_split3(lf)).astype(BF16), preferred_element_type=F32)
    c = cs
    for j in range(1, N_SPLIT):
        c = c + pltpu.roll(cs, LANES - j * n_heads, axis=1)
    c = c + carry_ref[...]
    carry_ref[...] = c[c.shape[0] - 1:, :]

    crep = c
    for j in range(1, N_SPLIT):
        crep = jnp.where((lane >= j * n_heads) & (lane < (j + 1) * n_heads),
                         pltpu.roll(c, j * n_heads, axis=1), crep)
    ones = jnp.where(lane == N_SPLIT * n_heads, 1.0, 0.0).astype(F32)
    pieces = pick(_split3(crep * LOG2E), extra=ones).astype(BF16)
    qx_ref[0] = jnp.dot(pieces, selq_ref[...], preferred_element_type=F32).astype(BF16)
    kx_ref[0] = jnp.dot(pieces, selk_ref[...], preferred_element_type=F32).astype(BF16)


def _aug_constants(tm, n_heads):
    r = jnp.arange(tm)
    tri = (r[:, None] >= r[None, :]).astype(BF16)
    rows = jnp.arange(LANES)[:, None]
    cols = jnp.arange(n_heads * HEAD_DIM)[None, :]
    head, lane = cols // HEAD_DIM, cols % HEAD_DIM
    one_row = N_SPLIT * n_heads
    is_piece = lambda off: (lane >= off) & (lane < off + N_SPLIT) & (rows == (lane - off) * n_heads + head)
    is_one = lambda off: (lane >= off) & (lane < off + N_SPLIT) & (rows == one_row)
    selq = (is_piece(0) | is_one(N_SPLIT)).astype(BF16)
    selk = is_one(0).astype(BF16) - is_piece(N_SPLIT).astype(BF16)
    return tri, selq, selk


def _aug(lf, *, n_heads, name):
    nseq, length, _ = lf.shape
    tm = _pick_tile(length, 512, 16)
    assert (N_SPLIT * n_heads) < LANES
    w = n_heads * HEAD_DIM
    tri, selq, selk = _aug_constants(tm, n_heads)
    return pl.pallas_call(
        functools.partial(_aug_kernel, n_heads=n_heads),
        grid=(nseq, length // tm),
        in_specs=[
            pl.BlockSpec((1, tm, LANES), lambda b, t: (b, t, 0)),
            pl.BlockSpec((tm, tm), lambda b, t: (0, 0)),
            pl.BlockSpec((LANES, w), lambda b, t: (0, 0)),
            pl.BlockSpec((LANES, w), lambda b, t: (0, 0)),
        ],
        out_specs=[pl.BlockSpec((1, tm, w), lambda b, t: (b, t, 0)),
                   pl.BlockSpec((1, tm, w), lambda b, t: (b, t, 0))],
        out_shape=[jax.ShapeDtypeStruct((nseq, length, w), BF16),
                   jax.ShapeDtypeStruct((nseq, length, w), BF16)],
        scratch_shapes=[pltpu.VMEM((1, LANES), F32)],
        compiler_params=_cparams(("parallel", "arbitrary")),
        name=name,
    )(lf, tri, selq, selk)


def _attn_kernel(q_ref, qx_ref, k_ref, kx_ref, v_ref, o_ref, vt_ref, m_ref, acc_ref, s0_ref, s1_ref, *, tq, tk, q_off, length):
    qi = pl.program_id(2)

    @pl.when(qi == 0)
    def _():
        def transpose_chunk(c, carry):
            st = pl.multiple_of(c * tk, tk)
            vt_ref[0:HEAD_DIM, pl.ds(st, tk)] = v_ref[0, pl.ds(st, tk), :].T
            return carry

        lax.fori_loop(0, length // tk, transpose_chunk, 0)
        sub = lax.broadcasted_iota(jnp.int32, (VT_ROWS - HEAD_DIM, length), 0)
        vt_ref[HEAD_DIM:, :] = jnp.where(sub == 0, 1.0, 0.0).astype(BF16)

    q = jnp.concatenate([q_ref[0], qx_ref[0]], axis=1)
    q_pos0 = q_off + qi * tq
    j_diag = q_pos0 // tk
    m_ref[...] = jnp.full_like(m_ref, -jnp.inf)
    acc_ref[...] = jnp.zeros_like(acc_ref)

    def scores(j):
        st = pl.multiple_of(j * tk, tk)
        k = jnp.concatenate([k_ref[0, pl.ds(st, tk), :], kx_ref[0, pl.ds(st, tk), :]], axis=1)
        return lax.dot_general(k, q, (((1,), (1,)), ((), ())), preferred_element_type=F32)

    def accumulate(j, s, masked):
        st = pl.multiple_of(j * tk, tk)
        if masked:
            key_pos = st + lax.broadcasted_iota(jnp.int32, (tk, tq), 0)
            qry_pos = q_pos0 + lax.broadcasted_iota(jnp.int32, (tk, tq), 1)
            s = jnp.where(key_pos <= qry_pos, s, -jnp.inf)
        m_prev = m_ref[...]
        m_new = jnp.maximum(m_prev, jnp.max(s, axis=0, keepdims=True))
        alpha = jnp.exp2(m_prev - m_new)
        p = jnp.exp2(s - m_new).astype(BF16)
        acc_ref[...] = alpha * acc_ref[...] + jnp.dot(vt_ref[:, pl.ds(st, tk)], p, preferred_element_type=F32)
        m_ref[...] = m_new

    bufs = (s0_ref, s1_ref)

    def run_full(j0, n):
        for u in range(n):
            bufs[(u + 1) % 2][...] = scores(j0 + u + 1)
            accumulate(j0 + u, bufs[u % 2][...], False)

    s0_ref[...] = scores(0)

    def group(i, carry):
        run_full(ATTN_UNROLL * i, ATTN_UNROLL)
        return carry

    lax.fori_loop(0, j_diag // ATTN_UNROLL, group, 0)
    rem = j_diag % ATTN_UNROLL
    for r in range(ATTN_UNROLL):
        @pl.when(rem == r)
        def _(r=r):
            run_full(j_diag - r, r)
            accumulate(j_diag, bufs[r % 2][...], True)

    acc = acc_ref[...]
    out_t = acc[0:HEAD_DIM] / acc[HEAD_DIM:HEAD_DIM + 1]
    o_ref[0] = out_t.T.astype(o_ref.dtype)


def _attn(q, qx, k, kx, v, *, q_off, name):
    nb, sq, w = q.shape
    length = k.shape[1]
    n_heads = w // HEAD_DIM
    tq = _pick_tile(sq, 512, LANES)
    tk = _pick_tile(length, 512, LANES)
    assert q_off % tq == 0 and tk % tq == 0 and q_off + sq <= length
    qoff_blk = q_off // tq
    kv_spec = pl.BlockSpec((1, length, HEAD_DIM), lambda b, h, i: (b, 0, h))
    return pl.pallas_call(
        functools.partial(_attn_kernel, tq=tq, tk=tk, q_off=q_off, length=length),
        grid=(nb, n_heads, sq // tq),
        in_specs=[
            pl.BlockSpec((1, tq, HEAD_DIM), lambda b, h, i: (b, i, h)),
            pl.BlockSpec((1, tq, HEAD_DIM), lambda b, h, i: (b, i + qoff_blk, h)),
            kv_spec, kv_spec, kv_spec,
        ],
        out_specs=pl.BlockSpec((1, tq, HEAD_DIM), lambda b, h, i: (b, i, h)),
        out_shape=jax.ShapeDtypeStruct((nb, sq, w), BF16),
        scratch_shapes=[pltpu.VMEM((VT_ROWS, length), BF16), pltpu.VMEM((1, tq), F32),
                        pltpu.VMEM((VT_ROWS, tq), F32), pltpu.VMEM((tk, tq), F32), pltpu.VMEM((tk, tq), F32)],
        compiler_params=_cparams(("parallel", "parallel", "arbitrary")),
        name=name,
    )(q, qx, k, kx, v)


def _attn_cached_kernel(q_ref, qx_ref, kx_ref, ck_ref, cv_ref, kn_ref, vn_ref, o_ref, *, n_heads, past, tn):
    pad = LANES - tn
    causal = (lax.broadcasted_iota(jnp.int32, (tn, LANES), 1) <= lax.broadcasted_iota(jnp.int32, (tn, LANES), 0))
    nt = (((1,), (1,)), ((), ()))
    for hd in range(n_heads):
        cols = slice(hd * HEAD_DIM, (hd + 1) * HEAD_DIM)
        q = jnp.concatenate([q_ref[:, cols], qx_ref[0, :, cols]], axis=1)
        kc = jnp.concatenate([ck_ref[pl.ds(hd, past, stride=n_heads), :].astype(BF16),
                              kx_ref[0, 0:past, cols]], axis=1)
        kn = jnp.concatenate([kn_ref[:, cols], kx_ref[0, past:past + tn, cols]], axis=1)
        kn = jnp.concatenate([kn, jnp.zeros((pad, kn.shape[1]), BF16)], axis=0)
        s_c = lax.dot_general(q, kc, nt, preferred_element_type=F32)
        s_n = jnp.where(causal, lax.dot_general(q, kn, nt, preferred_element_type=F32), -jnp.inf)
        m = jnp.maximum(jnp.max(s_c, axis=1, keepdims=True), jnp.max(s_n, axis=1, keepdims=True))
        p_c = jnp.exp2(s_c - m)
        p_n = jnp.exp2(s_n - m)
        denom = jnp.sum(p_c, axis=1, keepdims=True) + jnp.sum(p_n, axis=1, keepdims=True)
        vc = cv_ref[pl.ds(hd, past, stride=n_heads), :].astype(BF16)
        vn = jnp.concatenate([vn_ref[:, cols], jnp.zeros((pad, HEAD_DIM), BF16)], axis=0)
        out = (jnp.dot(p_c.astype(BF16), vc, preferred_element_type=F32)
               + jnp.dot(p_n.astype(BF16), vn, preferred_element_type=F32))
        o_ref[:, cols] = (out / denom).astype(o_ref.dtype)


def _attn_cached(q, qx, kx, cache_k, cache_v, k_new, v_new, *, n_seq, past, name):
    n, w = q.shape
    tn = n // n_seq
    n_heads = w // HEAD_DIM
    assert tn <= LANES and tn % 16 == 0 and past % tn == 0 and kx.shape[1] >= past + tn
    ck = cache_k.reshape(n_seq * past * n_heads, HEAD_DIM)
    cv = cache_v.reshape(n_seq * past * n_heads, HEAD_DIM)
    rows = pl.BlockSpec((tn, w), lambda b: (b, 0))
    cached = pl.BlockSpec((past * n_heads, HEAD_DIM), lambda b: (b, 0))
    return pl.pallas_call(
        functools.partial(_attn_cached_kernel, n_heads=n_heads, past=past, tn=tn),
        grid=(n_seq,),
        in_specs=[
            rows,
            pl.BlockSpec((1, tn, w), lambda b: (b, past // tn, 0)),
            pl.BlockSpec((1, kx.shape[1], w), lambda b: (b, 0, 0)),
            cached, cached, rows, rows,
        ],
        out_specs=rows,
        out_shape=jax.ShapeDtypeStruct((n, w), BF16),
        compiler_params=_cparams(("parallel",)),
        name=name,
    )(q, qx, kx, ck, cv, k_new, v_new)


def _pool_kernel(p_ref, hist_ref, w_ref, sc_ref, o_ref, ext_ref, *, tp, pos0):
    t = pl.program_id(1)

    @pl.when(t == 0)
    def _():
        ext_ref[0:HIST_ROWS, :] = hist_ref[0]

    ext_ref[HIST_ROWS:, :] = p_ref[0]
    pos = pos0 + t * tp + lax.broadcasted_iota(jnp.int32, (tp, 1), 0)
    gd = w_ref.shape[1]
    for g, win in enumerate(POOL_WINDOWS):
        cols = slice(g * gd, (g + 1) * gd)
        cur = ext_ref[HIST_ROWS:, cols]
        tot = cur
        for back in range(1, win):
            tot = tot + ext_ref[HIST_ROWS - back:HIST_ROWS - back + tp, cols]
        cnt = jnp.minimum(pos + 1, win).astype(F32)
        d = (tot / cnt - cur).astype(BF16)
        y = jnp.dot(d, w_ref[g], preferred_element_type=F32) * sc_ref[:, cols]
        o_ref[0, :, cols] = y.astype(o_ref.dtype)
    ext_ref[0:HIST_ROWS, :] = ext_ref[tp:tp + HIST_ROWS, :]


def _pool(p, hist, w_pool, scale, *, pos0, name):
    nseq, length, w = p.shape
    tp = min(512, length)
    assert length % tp == 0 and tp >= HIST_ROWS
    return pl.pallas_call(
        functools.partial(_pool_kernel, tp=tp, pos0=pos0),
        grid=(nseq, length // tp),
        in_specs=[
            pl.BlockSpec((1, tp, w), lambda b, t: (b, t, 0)),
            pl.BlockSpec((1, HIST_ROWS, w), lambda b, t: (b, 0, 0)),
            pl.BlockSpec(w_pool.shape, lambda b, t: (0, 0, 0)),
            pl.BlockSpec((1, w), lambda b, t: (0, 0)),
        ],
        out_specs=pl.BlockSpec((1, tp, w), lambda b, t: (b, t, 0)),
        out_shape=jax.ShapeDtypeStruct((nseq, length, w), BF16),
        scratch_shapes=[pltpu.VMEM((HIST_ROWS + tp, w), F32)],
        compiler_params=_cparams(("parallel", "arbitrary")),
        name=name,
    )(p, hist, w_pool, scale)


def _oproj_kernel(h_ref, fox_ref, pool_ref, wo_ref, o_ref):
    mix = jnp.concatenate([fox_ref[...], pool_ref[...]], axis=1)
    o_ref[...] = h_ref[...] + jnp.dot(mix, wo_ref[...], preferred_element_type=F32)


def _oproj(h, fox, pool, wo, *, name):
    n, d = h.shape
    tm = min(512, n)
    assert n % tm == 0
    return pl.pallas_call(
        _oproj_kernel,
        grid=(n // tm,),
        in_specs=[
            pl.BlockSpec((tm, d), lambda i: (i, 0)),
            pl.BlockSpec((tm, fox.shape[1]), lambda i: (i, 0)),
            pl.BlockSpec((tm, pool.shape[1]), lambda i: (i, 0)),
            pl.BlockSpec(wo.shape, lambda i: (0, 0)),
        ],
        out_specs=pl.BlockSpec((tm, d), lambda i: (i, 0)),
        out_shape=jax.ShapeDtypeStruct((n, d), F32),
        compiler_params=_cparams(("parallel",)),
        name=name,
    )(h, fox, pool, wo)


def _round_up(x, m):
    return (x + m - 1) // m * m


def kernel(x_prompt, x_sample, cache_k, cache_v, cache_logf, state_pool, g_ffn1, w1_gate, w1_up, w1_down, g_mix, w_in, b_f, w_pool, pool_scale, w_o, g_ffn2, w2_gate, w2_up, w2_down, g_final):
    B, S, D = x_prompt.shape
    Bs, Tn, _ = x_sample.shape
    depth, _, P, H, Dh = cache_k.shape
    assert depth == 1 and Dh == HEAD_DIM
    fw = H * Dh
    pw = w_pool.shape[1] * w_pool.shape[2]
    assert w_in.shape[2] == 3 * fw + H + pw and state_pool.shape[2] == POOL_HIST

    row = lambda a: a.reshape(1, -1).astype(F32)
    w1g, w1u, w1d = w1_gate[0].astype(BF16), w1_up[0].astype(BF16), w1_down[0].astype(BF16)
    w2g, w2u, w2d = w2_gate[0].astype(BF16), w2_up[0].astype(BF16), w2_down[0].astype(BF16)
    wi = w_in[0]
    w_qkvp = jnp.concatenate([wi[:, :3 * fw], wi[:, 3 * fw + H:]], axis=1).astype(BF16)
    wf_cols = jnp.tile(wi[:, 3 * fw:3 * fw + H], (1, N_SPLIT))
    wf = jnp.pad(wf_cols, ((0, 0), (0, LANES - N_SPLIT * H))).astype(BF16)
    bf = jnp.pad(jnp.tile(b_f[0].astype(F32), N_SPLIT), (0, LANES - N_SPLIT * H)).reshape(1, LANES)
    wo = w_o[0].astype(BF16)
    wp = w_pool[0].astype(BF16)
    g1, gm, g2, gfin, psc = row(g_ffn1[0]), row(g_mix[0]), row(g_ffn2[0]), row(g_final), row(pool_scale[0])

    def pre(x2d, tag):
        h = _ffn(x2d, g1, w1g, w1u, w1d, gfin, final_norm=False, name=f"ffn1_{tag}")
        return (h,) + tuple(_inproj(h, gm, w_qkvp, wf, bf, name=f"inproj_{tag}"))

    def post(h, fox, pool, tag):
        h2 = _oproj(h, fox, pool, wo, name=f"oproj_{tag}")
        return _ffn(h2, g2, w2g, w2u, w2d, gfin, final_norm=True, name=f"ffn2_{tag}")

    h, q, k, v, kb, vb, p, lf = pre(x_prompt.reshape(B * S, D), "prompt")
    qx, kx = _aug(lf.reshape(B, S, LANES), n_heads=H, name="aug_prompt")
    fox = _attn(q.reshape(B, S, fw), qx, kb.reshape(B, S, fw), kx, vb.reshape(B, S, fw),
                q_off=0, name="attn_prompt")
    p3 = p.reshape(B, S, pw)
    pool = _pool(p3, jnp.zeros((B, HIST_ROWS, pw), F32), wp, psc, pos0=0, name="pool_prompt")
    y_prompt = post(h, fox.reshape(B * S, fw), pool.reshape(B * S, pw), "prompt").reshape(B, S, D)
    k_prompt = k.reshape(1, B, S, H, Dh)
    v_prompt = v.reshape(1, B, S, H, Dh)
    logf_prompt = lf[:, :H].reshape(1, B, S, H)
    pool_prompt = p3[:, S - POOL_HIST:][None]

    h, q, k, v, kb, vb, p, lf = pre(x_sample.reshape(Bs * Tn, D), "sample")
    L = _round_up(P + Tn, LANES)
    tail = L - P - Tn
    lf_cache = jnp.pad(jnp.tile(cache_logf[0].astype(F32), (1, 1, N_SPLIT)),
                       ((0, 0), (0, 0), (0, LANES - N_SPLIT * H)))
    lf_all = jnp.concatenate([lf_cache, lf.reshape(Bs, Tn, LANES), jnp.zeros((Bs, tail, LANES), F32)], axis=1)
    qx, kx = _aug(lf_all, n_heads=H, name="aug_sample")
    fox = _attn_cached(q, qx, kx, cache_k[0], cache_v[0], kb, vb, n_seq=Bs, past=P, name="attn_sample")
    p3 = p.reshape(Bs, Tn, pw)
    hist = jnp.concatenate([jnp.zeros((Bs, HIST_ROWS - POOL_HIST, pw), F32), state_pool[0].astype(F32)], axis=1)
    pool = _pool(p3, hist, wp, psc, pos0=P, name="pool_sample")
    y_sample = post(h, fox, pool.reshape(Bs * Tn, pw), "sample").reshape(Bs, Tn, D)
    k_sample = k.reshape(1, Bs, Tn, H, Dh)
    v_sample = v.reshape(1, Bs, Tn, H, Dh)
    logf_sample = lf[:, :H].reshape(1, Bs, Tn, H)
    ext_tail = jnp.concatenate([state_pool[0].astype(F32), p3], axis=1)[:, Tn:]
    pool_sample = ext_tail[None]

    return (y_prompt, y_sample, k_prompt, v_prompt, logf_prompt, pool_prompt,
            k_sample, v_sample, logf_sample, pool_sample)
```

```python
import functools

import jax
import jax.numpy as jnp
from jax import lax
from jax.experimental import pallas as pl
from jax.experimental.pallas import tpu as pltpu

F32 = jnp.float32
BF16 = jnp.bfloat16

EPS = 1e-6
HEAD_DIM = 128
LANES = 128
POOL_WINDOWS = (2, 4, 8, 16)
POOL_HIST = max(POOL_WINDOWS) - 1
HIST_ROWS = 16
N_SPLIT = 3
VT_ROWS = HEAD_DIM + 16
LOG2E = 1.4426950408889634
VMEM_LIMIT = 56 * 1024 * 1024
ATTN_UNROLL = 4
FFN_ROWS = 1024
FFN_COLS = 256


def _cparams(sem):
    return pltpu.CompilerParams(dimension_semantics=sem, vmem_limit_bytes=VMEM_LIMIT)


def _pick_tile(length, cap, mult):
    best = None
    for t in range(mult, min(cap, length) + 1, mult):
        if length % t == 0:
            best = t
    assert best is not None, (length, cap, mult)
    return best


def _rms(x, g):
    return x * lax.rsqrt(jnp.mean(x * x, axis=-1, keepdims=True) + EPS) * g


def _split3(x):
    hi = x.astype(BF16).astype(F32)
    r1 = x - hi
    mid = r1.astype(BF16).astype(F32)
    lo = (r1 - mid).astype(BF16).astype(F32)
    return hi, mid, lo


def _ffn_kernel(x_ref, g_ref, wg_ref, wu_ref, wd_ref, gf_ref, o_ref, xn_ref, *, final_norm):
    f = pl.program_id(1)
    last = pl.num_programs(1) - 1

    def chunk(xn):
        gate = jnp.dot(xn, wg_ref[...], preferred_element_type=F32)
        up = jnp.dot(xn, wu_ref[...], preferred_element_type=F32)
        a = (gate * jax.nn.sigmoid(gate) * up).astype(BF16)
        return jnp.dot(a, wd_ref[...], preferred_element_type=F32)

    @pl.when(f == 0)
    def _():
        xn = _rms(x_ref[...], g_ref[...]).astype(BF16)
        xn_ref[...] = xn
        o_ref[...] = chunk(xn)

    @pl.when((f > 0) & (f < last))
    def _():
        o_ref[...] += chunk(xn_ref[...])

    @pl.when(f == last)
    def _():
        y = x_ref[...] + 0.5 * (o_ref[...] + chunk(xn_ref[...]))
        if final_norm:
            y = _rms(y, gf_ref[...])
        o_ref[...] = y


def _ffn(x, g, wg, wu, wd, gf, *, final_norm, name):
    n, d = x.shape
    dff = wg.shape[1]
    tm = min(FFN_ROWS, n)
    tf = FFN_COLS
    assert n % tm == 0 and dff % tf == 0 and dff // tf >= 2
    return pl.pallas_call(
        functools.partial(_ffn_kernel, final_norm=final_norm),
        grid=(n // tm, dff // tf),
        in_specs=[
            pl.BlockSpec((tm, d), lambda i, f: (i, 0)),
            pl.BlockSpec((1, d), lambda i, f: (0, 0)),
            pl.BlockSpec((d, tf), lambda i, f: (0, f)),
            pl.BlockSpec((d, tf), lambda i, f: (0, f)),
            pl.BlockSpec((tf, d), lambda i, f: (f, 0)),
            pl.BlockSpec((1, d), lambda i, f: (0, 0)),
        ],
        out_specs=pl.BlockSpec((tm, d), lambda i, f: (i, 0)),
        out_shape=jax.ShapeDtypeStruct((n, d), F32),
        scratch_shapes=[pltpu.VMEM((tm, d), BF16)],
        compiler_params=_cparams(("parallel", "arbitrary")),
        name=name,
    )(x, g, wg, wu, wd, gf)


def _inproj_kernel(h_ref, g_ref, w_ref, wf_ref, bf_ref,
                   q_ref, k_ref, v_ref, kb_ref, vb_ref, p_ref, lf_ref, *, scale, width):
    hn = _rms(h_ref[...], g_ref[...]).astype(BF16)
    n_heads = width // HEAD_DIM

    def proj(c):
        return jnp.dot(hn, w_ref[:, c * width:(c + 1) * width], preferred_element_type=F32)

    def store_heads(o_ref, u):
        for hd in range(n_heads):
            o_ref[pl.ds(hd, u.shape[0], stride=n_heads), :] = u[:, hd * HEAD_DIM:(hd + 1) * HEAD_DIM]

    lf_ref[...] = jax.nn.log_sigmoid(jnp.dot(hn, wf_ref[...], preferred_element_type=F32) + bf_ref[...])
    q_ref[...] = (proj(0) * scale).astype(BF16)
    u = proj(1)
    store_heads(k_ref, u)
    kb_ref[...] = u.astype(BF16)
    u = proj(2)
    store_heads(v_ref, u)
    vb_ref[...] = u.astype(BF16)
    p_ref[...] = proj(3)


def _inproj(h, g, w_qkvp, wf, bf, *, name):
    n, d = h.shape
    w = w_qkvp.shape[1] // 4
    tm = min(512, n)
    assert n % tm == 0 and w % HEAD_DIM == 0
    n_heads = w // HEAD_DIM
    row = lambda i: (i, 0)
    const = lambda i: (0, 0)
    resident = pl.Buffered(1)
    return pl.pallas_call(
        functools.partial(_inproj_kernel, scale=HEAD_DIM ** -0.5 * LOG2E, width=w),
        grid=(n // tm,),
        in_specs=[
            pl.BlockSpec((tm, d), row),
            pl.BlockSpec((1, d), const),
            pl.BlockSpec((d, 4 * w), const, pipeline_mode=resident),
            pl.BlockSpec((d, LANES), const),
            pl.BlockSpec((1, LANES), const),
        ],
        out_specs=[
            pl.BlockSpec((tm, w), row),
            pl.BlockSpec((tm * n_heads, HEAD_DIM), row), pl.BlockSpec((tm * n_heads, HEAD_DIM), row),
            pl.BlockSpec((tm, w), row), pl.BlockSpec((tm, w), row), pl.BlockSpec((tm, w), row),
            pl.BlockSpec((tm, LANES), row),
        ],
        out_shape=[
            jax.ShapeDtypeStruct((n, w), BF16),
            jax.ShapeDtypeStruct((n * n_heads, HEAD_DIM), F32),
            jax.ShapeDtypeStruct((n * n_heads, HEAD_DIM), F32),
            jax.ShapeDtypeStruct((n, w), BF16),
            jax.ShapeDtypeStruct((n, w), BF16),
            jax.ShapeDtypeStruct((n, w), F32),
            jax.ShapeDtypeStruct((n, LANES), F32),
        ],
        compiler_params=_cparams(("parallel",)),
        name=name,
    )(h, g, w_qkvp, wf, bf)


def _aug_kernel(lf_ref, tri_ref, selq_ref, selk_ref, qx_ref, kx_ref, carry_ref, *, n_heads):
    t = pl.program_id(1)

    @pl.when(t == 0)
    def _():
        carry_ref[...] = jnp.zeros_like(carry_ref)

    lf = lf_ref[0]
    lane = lax.broadcasted_iota(jnp.int32, lf.shape, 1)

    def pick(parts, extra=None):
        out = jnp.zeros_like(lf) if extra is None else extra
        for j in reversed(range(N_SPLIT)):
            out = jnp.where((lane >= j * n_heads) & (lane < (j + 1) * n_heads), parts[j], out)
        return out

    cs = jnp.dot(tri_ref[...], pick(_split3(lf)).astype(BF16), preferred_element_type=F32)
    c = cs
    for j in range(1, N_SPLIT):
        c = c + pltpu.roll(cs, LANES - j * n_heads, axis=1)
    c = c + carry_ref[...]
    carry_ref[...] = c[c.shape[0] - 1:, :]

    crep = c
    for j in range(1, N_SPLIT):
        crep = jnp.where((lane >= j * n_heads) & (lane < (j + 1) * n_heads),
                         pltpu.roll(c, j * n_heads, axis=1), crep)
    ones = jnp.where(lane == N_SPLIT * n_heads, 1.0, 0.0).astype(F32)
    pieces = pick(_split3(crep * LOG2E), extra=ones).astype(BF16)
    qx_ref[0] = jnp.dot(pieces, selq_ref[...], preferred_element_type=F32).astype(BF16)
    kx_ref[0] = jnp.dot(pieces, selk_ref[...], preferred_element_type=F32).astype(BF16)


def _aug_constants(tm, n_heads):
    r = jnp.arange(tm)
    tri = (r[:, None] >= r[None, :]).astype(BF16)
    rows = jnp.arange(LANES)[:, None]
    cols = jnp.arange(n_heads * HEAD_DIM)[None, :]
    head, lane = cols // HEAD_DIM, cols % HEAD_DIM
    one_row = N_SPLIT * n_heads
    is_piece = lambda off: (lane >= off) & (lane < off + N_SPLIT) & (rows == (lane - off) * n_heads + head)
    is_one = lambda off: (lane >= off) & (lane < off + N_SPLIT) & (rows == one_row)
    selq = (is_piece(0) | is_one(N_SPLIT)).astype(BF16)
    selk = is_one(0).astype(BF16) - is_piece(N_SPLIT).astype(BF16)
    return tri, selq, selk


def _aug(lf, *, n_heads, tile_cap, name):
    nseq, length, _ = lf.shape
    tm = _pick_tile(length, tile_cap, 16)
    assert (N_SPLIT * n_heads) < LANES
    w = n_heads * HEAD_DIM
    tri, selq, selk = _aug_constants(tm, n_heads)
    return pl.pallas_call(
        functools.partial(_aug_kernel, n_heads=n_heads),
        grid=(nseq, length // tm),
        in_specs=[
            pl.BlockSpec((1, tm, LANES), lambda b, t: (b, t, 0)),
            pl.BlockSpec((tm, tm), lambda b, t: (0, 0)),
            pl.BlockSpec((LANES, w), lambda b, t: (0, 0)),
            pl.BlockSpec((LANES, w), lambda b, t: (0, 0)),
        ],
        out_specs=[pl.BlockSpec((1, tm, w), lambda b, t: (b, t, 0)),
                   pl.BlockSpec((1, tm, w), lambda b, t: (b, t, 0))],
        out_shape=[jax.ShapeDtypeStruct((nseq, length, w), BF16),
                   jax.ShapeDtypeStruct((nseq, length, w), BF16)],
        scratch_shapes=[pltpu.VMEM((1, LANES), F32)],
        compiler_params=_cparams(("parallel", "arbitrary")),
        name=name,
    )(lf, tri, selq, selk)


def _attn_kernel(q_ref, qx_ref, k_ref, kx_ref, v_ref, o_ref, vt_ref, m_ref, acc_ref, s0_ref, s1_ref, *, tq, tk, q_off, length):
    qi = pl.program_id(2)

    @pl.when(qi == 0)
    def _():
        def transpose_chunk(c, carry):
            st = pl.multiple_of(c * tk, tk)
            vt_ref[0:HEAD_DIM, pl.ds(st, tk)] = v_ref[0, pl.ds(st, tk), :].T
            return carry

        lax.fori_loop(0, length // tk, transpose_chunk, 0)
        sub = lax.broadcasted_iota(jnp.int32, (VT_ROWS - HEAD_DIM, length), 0)
        vt_ref[HEAD_DIM:, :] = jnp.where(sub == 0, 1.0, 0.0).astype(BF16)

    q = jnp.concatenate([q_ref[0], qx_ref[0]], axis=1)
    q_pos0 = q_off + qi * tq
    j_diag = q_pos0 // tk
    m_ref[...] = jnp.full_like(m_ref, -jnp.inf)
    acc_ref[...] = jnp.zeros_like(acc_ref)

    def scores(j):
        st = pl.multiple_of(j * tk, tk)
        k = jnp.concatenate([k_ref[0, pl.ds(st, tk), :], kx_ref[0, pl.ds(st, tk), :]], axis=1)
        return lax.dot_general(k, q, (((1,), (1,)), ((), ())), preferred_element_type=F32)

    def accumulate(j, s, masked):
        st = pl.multiple_of(j * tk, tk)
        if masked:
            key_pos = st + lax.broadcasted_iota(jnp.int32, (tk, tq), 0)
            qry_pos = q_pos0 + lax.broadcasted_iota(jnp.int32, (tk, tq), 1)
            s = jnp.where(key_pos <= qry_pos, s, -jnp.inf)
        m_prev = m_ref[...]
        m_new = jnp.maximum(m_prev, jnp.max(s, axis=0, keepdims=True))
        alpha = jnp.exp2(m_prev - m_new)
        p = jnp.exp2(s - m_new).astype(BF16)
        acc_ref[...] = alpha * acc_ref[...] + jnp.dot(vt_ref[:, pl.ds(st, tk)], p, preferred_element_type=F32)
        m_ref[...] = m_new

    bufs = (s0_ref, s1_ref)

    def run_full(j0, n):
        for u in range(n):
            bufs[(u + 1) % 2][...] = scores(j0 + u + 1)
            accumulate(j0 + u, bufs[u % 2][...], False)

    s0_ref[...] = scores(0)

    def group(i, carry):
        run_full(ATTN_UNROLL * i, ATTN_UNROLL)
        return carry

    lax.fori_loop(0, j_diag // ATTN_UNROLL, group, 0)
    rem = j_diag % ATTN_UNROLL
    for r in range(ATTN_UNROLL):
        @pl.when(rem == r)
        def _(r=r):
            run_full(j_diag - r, r)
            accumulate(j_diag, bufs[r % 2][...], True)

    acc = acc_ref[...]
    out_t = acc[0:HEAD_DIM] / acc[HEAD_DIM:HEAD_DIM + 1]
    o_ref[0] = out_t.T.astype(o_ref.dtype)


def _attn(q, qx, k, kx, v, *, q_off, name):
    nb, sq, w = q.shape
    length = k.shape[1]
    n_heads = w // HEAD_DIM
    tq = _pick_tile(sq, 512, LANES)
    tk = _pick_tile(length, 512, LANES)
    assert q_off % tq == 0 and tk % tq == 0 and q_off + sq <= length
    qoff_blk = q_off // tq
    kv_spec = pl.BlockSpec((1, length, HEAD_DIM), lambda b, h, i: (b, 0, h))
    return pl.pallas_call(
        functools.partial(_attn_kernel, tq=tq, tk=tk, q_off=q_off, length=length),
        grid=(nb, n_heads, sq // tq),
        in_specs=[
            pl.BlockSpec((1, tq, HEAD_DIM), lambda b, h, i: (b, i, h)),
            pl.BlockSpec((1, tq, HEAD_DIM), lambda b, h, i: (b, i + qoff_blk, h)),
            kv_spec, kv_spec, kv_spec,
        ],
        out_specs=pl.BlockSpec((1, tq, HEAD_DIM), lambda b, h, i: (b, i, h)),
        out_shape=jax.ShapeDtypeStruct((nb, sq, w), BF16),
        scratch_shapes=[pltpu.VMEM((VT_ROWS, length), BF16), pltpu.VMEM((1, tq), F32),
                        pltpu.VMEM((VT_ROWS, tq), F32), pltpu.VMEM((tk, tq), F32), pltpu.VMEM((tk, tq), F32)],
        compiler_params=_cparams(("parallel", "parallel", "arbitrary")),
        name=name,
    )(q, qx, k, kx, v)


def _attn_cached_kernel(q_ref, qx_ref, kx_ref, ck_ref, cv_ref, kn_ref, vn_ref, o_ref, *, n_heads, past, tn):
    pad = LANES - tn
    causal = (lax.broadcasted_iota(jnp.int32, (tn, LANES), 1) <= lax.broadcasted_iota(jnp.int32, (tn, LANES), 0))
    nt = (((1,), (1,)), ((), ()))
    for hd in range(n_heads):
        cols = slice(hd * HEAD_DIM, (hd + 1) * HEAD_DIM)
        q = jnp.concatenate([q_ref[:, cols], qx_ref[0, :, cols]], axis=1)
        kc = jnp.concatenate([ck_ref[pl.ds(hd, past, stride=n_heads), :].astype(BF16),
                              kx_ref[0, 0:past, cols]], axis=1)
        kn = jnp.concatenate([kn_ref[:, cols], kx_ref[0, past:past + tn, cols]], axis=1)
        kn = jnp.concatenate([kn, jnp.zeros((pad, kn.shape[1]), BF16)], axis=0)
        s_c = lax.dot_general(q, kc, nt, preferred_element_type=F32)
        s_n = jnp.where(causal, lax.dot_general(q, kn, nt, preferred_element_type=F32), -jnp.inf)
        m = jnp.maximum(jnp.max(s_c, axis=1, keepdims=True), jnp.max(s_n, axis=1, keepdims=True))
        p_c = jnp.exp2(s_c - m)
        p_n = jnp.exp2(s_n - m)
        denom = jnp.sum(p_c, axis=1, keepdims=True) + jnp.sum(p_n, axis=1, keepdims=True)
        vc = cv_ref[pl.ds(hd, past, stride=n_heads), :].astype(BF16)
        vn = jnp.concatenate([vn_ref[:, cols], jnp.zeros((pad, HEAD_DIM), BF16)], axis=0)
        out = (jnp.dot(p_c.astype(BF16), vc, preferred_element_type=F32)
               + jnp.dot(p_n.astype(BF16), vn, preferred_element_type=F32))
        o_ref[:, cols] = (out / denom).astype(o_ref.dtype)


def _attn_cached(q, qx, kx, cache_k, cache_v, k_new, v_new, *, n_seq, past, name):
    n, w = q.shape
    tn = n // n_seq
    n_heads = w // HEAD_DIM
    assert tn <= LANES and tn % 16 == 0 and past % tn == 0 and kx.shape[1] >= past + tn
    ck = cache_k.reshape(n_seq * past * n_heads, HEAD_DIM)
    cv = cache_v.reshape(n_seq * past * n_heads, HEAD_DIM)
    rows = pl.BlockSpec((tn, w), lambda b: (b, 0))
    cached = pl.BlockSpec((past * n_heads, HEAD_DIM), lambda b: (b, 0))
    return pl.pallas_call(
        functools.partial(_attn_cached_kernel, n_heads=n_heads, past=past, tn=tn),
        grid=(n_seq,),
        in_specs=[
            rows,
            pl.BlockSpec((1, tn, w), lambda b: (b, past // tn, 0)),
            pl.BlockSpec((1, kx.shape[1], w), lambda b: (b, 0, 0)),
            cached, cached, rows, rows,
        ],
        out_specs=rows,
        out_shape=jax.ShapeDtypeStruct((n, w), BF16),
        compiler_params=_cparams(("parallel",)),
        name=name,
    )(q, qx, kx, ck, cv, k_new, v_new)


def _pool_kernel(p_ref, hist_ref, w_ref, sc_ref, o_ref, ext_ref, *, tp, pos0):
    t = pl.program_id(1)

    @pl.when(t == 0)
    def _():
        ext_ref[0:HIST_ROWS, :] = hist_ref[0]

    ext_ref[HIST_ROWS:, :] = p_ref[0]
    pos = pos0 + t * tp + lax.broadcasted_iota(jnp.int32, (tp, 1), 0)
    gd = w_ref.shape[1]
    for g, win in enumerate(POOL_WINDOWS):
        cols = slice(g * gd, (g + 1) * gd)
        cur = ext_ref[HIST_ROWS:, cols]
        tot = cur
        for back in range(1, win):
            tot = tot + ext_ref[HIST_ROWS - back:HIST_ROWS - back + tp, cols]
        cnt = jnp.minimum(pos + 1, win).astype(F32)
        d = (tot / cnt - cur).astype(BF16)
        y = jnp.dot(d, w_ref[g], preferred_element_type=F32) * sc_ref[:, cols]
        o_ref[0, :, cols] = y.astype(o_ref.dtype)
    ext_ref[0:HIST_ROWS, :] = ext_ref[tp:tp + HIST_ROWS, :]


def _pool(p, hist, w_pool, scale, *, pos0, name):
    nseq, length, w = p.shape
    tp = min(512, length)
    assert length % tp == 0 and tp >= HIST_ROWS
    return pl.pallas_call(
        functools.partial(_pool_kernel, tp=tp, pos0=pos0),
        grid=(nseq, length // tp),
        in_specs=[
            pl.BlockSpec((1, tp, w), lambda b, t: (b, t, 0)),
            pl.BlockSpec((1, HIST_ROWS, w), lambda b, t: (b, 0, 0)),
            pl.BlockSpec(w_pool.shape, lambda b, t: (0, 0, 0)),
            pl.BlockSpec((1, w), lambda b, t: (0, 0)),
        ],
        out_specs=pl.BlockSpec((1, tp, w), lambda b, t: (b, t, 0)),
        out_shape=jax.ShapeDtypeStruct((nseq, length, w), BF16),
        scratch_shapes=[pltpu.VMEM((HIST_ROWS + tp, w), F32)],
        compiler_params=_cparams(("parallel", "arbitrary")),
        name=name,
    )(p, hist, w_pool, scale)


def _oproj_kernel(h_ref, fox_ref, pool_ref, wo_ref, o_ref):
    mix = jnp.concatenate([fox_ref[...], pool_ref[...]], axis=1)
    o_ref[...] = h_ref[...] + jnp.dot(mix, wo_ref[...], preferred_element_type=F32)


def _oproj(h, fox, pool, wo, *, name):
    n, d = h.shape
    tm = min(512, n)
    assert n % tm == 0
    return pl.pallas_call(
        _oproj_kernel,
        grid=(n // tm,),
        in_specs=[
            pl.BlockSpec((tm, d), lambda i: (i, 0)),
            pl.BlockSpec((tm, fox.shape[1]), lambda i: (i, 0)),
            pl.BlockSpec((tm, pool.shape[1]), lambda i: (i, 0)),
            pl.BlockSpec(wo.shape, lambda i: (0, 0)),
        ],
        out_specs=pl.BlockSpec((tm, d), lambda i: (i, 0)),
        out_shape=jax.ShapeDtypeStruct((n, d), F32),
        compiler_params=_cparams(("parallel",)),
        name=name,
    )(h, fox, pool, wo)


def _round_up(x, m):
    return (x + m - 1) // m * m


def kernel(x_prompt, x_sample, cache_k, cache_v, cache_logf, state_pool, g_ffn1, w1_gate, w1_up, w1_down, g_mix, w_in, b_f, w_pool, pool_scale, w_o, g_ffn2, w2_gate, w2_up, w2_down, g_final):
    B, S, D = x_prompt.shape
    Bs, Tn, _ = x_sample.shape
    depth, _, P, H, Dh = cache_k.shape
    assert depth == 1 and Dh == HEAD_DIM
    fw = H * Dh
    pw = w_pool.shape[1] * w_pool.shape[2]
    assert w_in.shape[2] == 3 * fw + H + pw and state_pool.shape[2] == POOL_HIST

    row = lambda a: a.reshape(1, -1).astype(F32)
    w1g, w1u, w1d = w1_gate[0].astype(BF16), w1_up[0].astype(BF16), w1_down[0].astype(BF16)
    w2g, w2u, w2d = w2_gate[0].astype(BF16), w2_up[0].astype(BF16), w2_down[0].astype(BF16)
    wi = w_in[0]
    w_qkvp = jnp.concatenate([wi[:, :3 * fw], wi[:, 3 * fw + H:]], axis=1).astype(BF16)
    wf_cols = jnp.tile(wi[:, 3 * fw:3 * fw + H], (1, N_SPLIT))
    wf = jnp.pad(wf_cols, ((0, 0), (0, LANES - N_SPLIT * H))).astype(BF16)
    bf = jnp.pad(jnp.tile(b_f[0].astype(F32), N_SPLIT), (0, LANES - N_SPLIT * H)).reshape(1, LANES)
    wo = w_o[0].astype(BF16)
    wp = w_pool[0].astype(BF16)
    g1, gm, g2, gfin, psc = row(g_ffn1[0]), row(g_mix[0]), row(g_ffn2[0]), row(g_final), row(pool_scale[0])

    def pre(x2d, tag):
        h = _ffn(x2d, g1, w1g, w1u, w1d, gfin, final_norm=False, name=f"ffn1_{tag}")
        return (h,) + tuple(_inproj(h, gm, w_qkvp, wf, bf, name=f"inproj_{tag}"))

    def post(h, fox, pool, tag):
        h2 = _oproj(h, fox, pool, wo, name=f"oproj_{tag}")
        return _ffn(h2, g2, w2g, w2u, w2d, gfin, final_norm=True, name=f"ffn2_{tag}")

    h, q, k, v, kb, vb, p, lf = pre(x_prompt.reshape(B * S, D), "prompt")
    qx, kx = _aug(lf.reshape(B, S, LANES), n_heads=H, tile_cap=512, name="aug_prompt")
    fox = _attn(q.reshape(B, S, fw), qx, kb.reshape(B, S, fw), kx, vb.reshape(B, S, fw),
                q_off=0, name="attn_prompt")
    p3 = p.reshape(B, S, pw)
    pool = _pool(p3, jnp.zeros((B, HIST_ROWS, pw), F32), wp, psc, pos0=0, name="pool_prompt")
    y_prompt = post(h, fox.reshape(B * S, fw), pool.reshape(B * S, pw), "prompt").reshape(B, S, D)
    k_prompt = k.reshape(1, B, S, H, Dh)
    v_prompt = v.reshape(1, B, S, H, Dh)
    logf_prompt = lf[:, :H].reshape(1, B, S, H)
    pool_prompt = p3[:, S - POOL_HIST:][None]

    h, q, k, v, kb, vb, p, lf = pre(x_sample.reshape(Bs * Tn, D), "sample")
    L = _round_up(P + Tn, LANES)
    tail = L - P - Tn
    lf_cache = jnp.pad(jnp.tile(cache_logf[0].astype(F32), (1, 1, N_SPLIT)),
                       ((0, 0), (0, 0), (0, LANES - N_SPLIT * H)))
    lf_all = jnp.concatenate([lf_cache, lf.reshape(Bs, Tn, LANES), jnp.zeros((Bs, tail, LANES), F32)], axis=1)
    qx, kx = _aug(lf_all, n_heads=H, tile_cap=L, name="aug_sample")
    fox = _attn_cached(q, qx, kx, cache_k[0], cache_v[0], kb, vb, n_seq=Bs, past=P, name="attn_sample")
    p3 = p.reshape(Bs, Tn, pw)
    hist = jnp.concatenate([jnp.zeros((Bs, HIST_ROWS - POOL_HIST, pw), F32), state_pool[0].astype(F32)], axis=1)
    pool = _pool(p3, hist, wp, psc, pos0=P, name="pool_sample")
    y_sample = post(h, fox, pool.reshape(Bs * Tn, pw), "sample").reshape(Bs, Tn, D)
    k_sample = k.reshape(1, Bs, Tn, H, Dh)
    v_sample = v.reshape(1, Bs, Tn, H, Dh)
    logf_sample = lf[:, :H].reshape(1, Bs, Tn, H)
    ext_tail = jnp.concatenate([state_pool[0].astype(F32), p3], axis=1)[:, Tn:]
    pool_sample = ext_tail[None]

    return (y_prompt, y_sample, k_prompt, v_prompt, logf_prompt, pool_prompt,
            k_sample, v_sample, logf_sample, pool_sample)
```

```python
import functools

import jax
import jax.numpy as jnp
from jax import lax
from jax.experimental import pallas as pl
from jax.experimental.pallas import tpu as pltpu

F32 = jnp.float32
BF16 = jnp.bfloat16

EPS = 1e-6
HEAD_DIM = 128
LANES = 128
POOL_WINDOWS = (2, 4, 8, 16)
POOL_HIST = max(POOL_WINDOWS) - 1
HIST_ROWS = 16
N_SPLIT = 3
VT_ROWS = HEAD_DIM + 16
LOG2E = 1.4426950408889634
VMEM_LIMIT = 56 * 1024 * 1024
ATTN_UNROLL = 4
ATTN_TILE = 512
PRUNE_GAP = 170.0
FFN_ROWS = 1024
FFN_COLS = 256


def _cparams(sem):
    return pltpu.CompilerParams(dimension_semantics=sem, vmem_limit_bytes=VMEM_LIMIT)


def _pick_tile(length, cap, mult):
    best = None
    for t in range(mult, min(cap, length) + 1, mult):
        if length % t == 0:
            best = t
    assert best is not None, (length, cap, mult)
    return best


def _rms(x, g):
    return x * lax.rsqrt(jnp.mean(x * x, axis=-1, keepdims=True) + EPS) * g


def _split3(x):
    hi = x.astype(BF16).astype(F32)
    r1 = x - hi
    mid = r1.astype(BF16).astype(F32)
    lo = (r1 - mid).astype(BF16).astype(F32)
    return hi, mid, lo


def _ffn_kernel(x_ref, g_ref, wg_ref, wu_ref, wd_ref, gf_ref, o_ref, xn_ref, *, final_norm):
    f = pl.program_id(1)
    last = pl.num_programs(1) - 1

    def chunk(xn):
        gate = jnp.dot(xn, wg_ref[...], preferred_element_type=F32)
        up = jnp.dot(xn, wu_ref[...], preferred_element_type=F32)
        a = (gate * jax.nn.sigmoid(gate) * up).astype(BF16)
        return jnp.dot(a, wd_ref[...], preferred_element_type=F32)

    @pl.when(f == 0)
    def _():
        xn = _rms(x_ref[...], g_ref[...]).astype(BF16)
        xn_ref[...] = xn
        o_ref[...] = chunk(xn)

    @pl.when((f > 0) & (f < last))
    def _():
        o_ref[...] += chunk(xn_ref[...])

    @pl.when(f == last)
    def _():
        y = x_ref[...] + 0.5 * (o_ref[...] + chunk(xn_ref[...]))
        if final_norm:
            y = _rms(y, gf_ref[...])
        o_ref[...] = y


def _ffn(x, g, wg, wu, wd, gf, *, final_norm, name):
    n, d = x.shape
    dff = wg.shape[1]
    tm = min(FFN_ROWS, n)
    tf = FFN_COLS
    assert n % tm == 0 and dff % tf == 0 and dff // tf >= 2
    return pl.pallas_call(
        functools.partial(_ffn_kernel, final_norm=final_norm),
        grid=(n // tm, dff // tf),
        in_specs=[
            pl.BlockSpec((tm, d), lambda i, f: (i, 0)),
            pl.BlockSpec((1, d), lambda i, f: (0, 0)),
            pl.BlockSpec((d, tf), lambda i, f: (0, f)),
            pl.BlockSpec((d, tf), lambda i, f: (0, f)),
            pl.BlockSpec((tf, d), lambda i, f: (f, 0)),
            pl.BlockSpec((1, d), lambda i, f: (0, 0)),
        ],
        out_specs=pl.BlockSpec((tm, d), lambda i, f: (i, 0)),
        out_shape=jax.ShapeDtypeStruct((n, d), F32),
        scratch_shapes=[pltpu.VMEM((tm, d), BF16)],
        compiler_params=_cparams(("parallel", "arbitrary")),
        name=name,
    )(x, g, wg, wu, wd, gf)


def _inproj_kernel(h_ref, g_ref, w_ref, wf_ref, bf_ref,
                   q_ref, k_ref, v_ref, kb_ref, vb_ref, p_ref, lf_ref, qn_ref, kn_ref, *, scale, width):
    hn = _rms(h_ref[...], g_ref[...]).astype(BF16)
    n_heads = width // HEAD_DIM

    def proj(c):
        return jnp.dot(hn, w_ref[:, c * width:(c + 1) * width], preferred_element_type=F32)

    def store_heads(o_ref, u):
        for hd in range(n_heads):
            o_ref[pl.ds(hd, u.shape[0], stride=n_heads), :] = u[:, hd * HEAD_DIM:(hd + 1) * HEAD_DIM]

    def max_sq_norms(xb):
        sub = lax.broadcasted_iota(jnp.int32, (n_heads, LANES), 0)
        out = jnp.zeros((n_heads, LANES), F32)
        for hd in range(n_heads):
            x = xb[:, hd * HEAD_DIM:(hd + 1) * HEAD_DIM].astype(F32)
            top = jnp.max(jnp.sum(x * x, axis=1, keepdims=True), axis=0, keepdims=True)
            out = jnp.where(sub == hd, top, out)
        return out

    lf_ref[...] = jax.nn.log_sigmoid(jnp.dot(hn, wf_ref[...], preferred_element_type=F32) + bf_ref[...])
    qb = (proj(0) * scale).astype(BF16)
    q_ref[...] = qb
    qn_ref[...] = max_sq_norms(qb)
    u = proj(1)
    store_heads(k_ref, u)
    kb = u.astype(BF16)
    kb_ref[...] = kb
    kn_ref[...] = max_sq_norms(kb)
    u = proj(2)
    store_heads(v_ref, u)
    vb_ref[...] = u.astype(BF16)
    p_ref[...] = proj(3)


def _inproj(h, g, w_qkvp, wf, bf, *, name):
    n, d = h.shape
    w = w_qkvp.shape[1] // 4
    tm = min(ATTN_TILE, n)
    assert n % tm == 0 and w % HEAD_DIM == 0
    n_heads = w // HEAD_DIM
    assert n_heads == 8
    row = lambda i: (i, 0)
    const = lambda i: (0, 0)
    resident = pl.Buffered(1)
    return pl.pallas_call(
        functools.partial(_inproj_kernel, scale=HEAD_DIM ** -0.5 * LOG2E, width=w),
        grid=(n // tm,),
        in_specs=[
            pl.BlockSpec((tm, d), row),
            pl.BlockSpec((1, d), const),
            pl.BlockSpec((d, 4 * w), const, pipeline_mode=resident),
            pl.BlockSpec((d, LANES), const),
            pl.BlockSpec((1, LANES), const),
        ],
        out_specs=[
            pl.BlockSpec((tm, w), row),
            pl.BlockSpec((tm * n_heads, HEAD_DIM), row), pl.BlockSpec((tm * n_heads, HEAD_DIM), row),
            pl.BlockSpec((tm, w), row), pl.BlockSpec((tm, w), row), pl.BlockSpec((tm, w), row),
            pl.BlockSpec((tm, LANES), row),
            pl.BlockSpec((n_heads, LANES), row), pl.BlockSpec((n_heads, LANES), row),
        ],
        out_shape=[
            jax.ShapeDtypeStruct((n, w), BF16),
            jax.ShapeDtypeStruct((n * n_heads, HEAD_DIM), F32),
            jax.ShapeDtypeStruct((n * n_heads, HEAD_DIM), F32),
            jax.ShapeDtypeStruct((n, w), BF16),
            jax.ShapeDtypeStruct((n, w), BF16),
            jax.ShapeDtypeStruct((n, w), F32),
            jax.ShapeDtypeStruct((n, LANES), F32),
            jax.ShapeDtypeStruct((n // tm * n_heads, LANES), F32),
            jax.ShapeDtypeStruct((n // tm * n_heads, LANES), F32),
        ],
        compiler_params=_cparams(("parallel",)),
        name=name,
    )(h, g, w_qkvp, wf, bf)


def _aug_kernel(lf_ref, tri_ref, selq_ref, selk_ref, qx_ref, kx_ref, ends_ref, carry_ref, *, n_heads):
    t = pl.program_id(1)

    @pl.when(t == 0)
    def _():
        carry_ref[...] = jnp.zeros_like(carry_ref)

    lf = lf_ref[0]
    lane = lax.broadcasted_iota(jnp.int32, lf.shape, 1)

    def pick(parts, extra=None):
        out = jnp.zeros_like(lf) if extra is None else extra
        for j in reversed(range(N_SPLIT)):
            out = jnp.where((lane >= j * n_heads) & (lane < (j + 1) * n_heads), parts[j], out)
        return out

    cs = jnp.dot(tri_ref[...], pick(_split3(lf)).astype(BF16), preferred_element_type=F32)
    c = cs
    for j in range(1, N_SPLIT):
        c = c + pltpu.roll(cs, LANES - j * n_heads, axis=1)
    c = c + carry_ref[...]
    carry_ref[...] = c[c.shape[0] - 1:, :]

    crep = c
    for j in range(1, N_SPLIT):
        crep = jnp.where((lane >= j * n_heads) & (lane < (j + 1) * n_heads),
                         pltpu.roll(c, j * n_heads, axis=1), crep)
    ones = jnp.where(lane == N_SPLIT * n_heads, 1.0, 0.0).astype(F32)
    pieces = pick(_split3(crep * LOG2E), extra=ones).astype(BF16)
    qx_ref[0] = jnp.dot(pieces, selq_ref[...], preferred_element_type=F32).astype(BF16)
    kx_ref[0] = jnp.dot(pieces, selk_ref[...], preferred_element_type=F32).astype(BF16)
    sub = lax.broadcasted_iota(jnp.int32, (8, LANES), 0)
    ends_ref[0] = jnp.where(sub == 0, c[0:1, :] * LOG2E, jnp.where(sub == 1, c[c.shape[0] - 1:, :] * LOG2E, 0.0))


def _aug_constants(tm, n_heads):
    r = jnp.arange(tm)
    tri = (r[:, None] >= r[None, :]).astype(BF16)
    rows = jnp.arange(LANES)[:, None]
    cols = jnp.arange(n_heads * HEAD_DIM)[None, :]
    head, lane = cols // HEAD_DIM, cols % HEAD_DIM
    one_row = N_SPLIT * n_heads
    is_piece = lambda off: (lane >= off) & (lane < off + N_SPLIT) & (rows == (lane - off) * n_heads + head)
    is_one = lambda off: (lane >= off) & (lane < off + N_SPLIT) & (rows == one_row)
    selq = (is_piece(0) | is_one(N_SPLIT)).astype(BF16)
    selk = is_one(0).astype(BF16) - is_piece(N_SPLIT).astype(BF16)
    return tri, selq, selk


def _aug(lf, *, n_heads, tile_cap, name):
    nseq, length, _ = lf.shape
    tm = _pick_tile(length, tile_cap, 16)
    assert (N_SPLIT * n_heads) < LANES
    w = n_heads * HEAD_DIM
    tri, selq, selk = _aug_constants(tm, n_heads)
    return pl.pallas_call(
        functools.partial(_aug_kernel, n_heads=n_heads),
        grid=(nseq, length // tm),
        in_specs=[
            pl.BlockSpec((1, tm, LANES), lambda b, t: (b, t, 0)),
            pl.BlockSpec((tm, tm), lambda b, t: (0, 0)),
            pl.BlockSpec((LANES, w), lambda b, t: (0, 0)),
            pl.BlockSpec((LANES, w), lambda b, t: (0, 0)),
        ],
        out_specs=[pl.BlockSpec((1, tm, w), lambda b, t: (b, t, 0)),
                   pl.BlockSpec((1, tm, w), lambda b, t: (b, t, 0)),
                   pl.BlockSpec((1, 8, LANES), lambda b, t: (b, t, 0))],
        out_shape=[jax.ShapeDtypeStruct((nseq, length, w), BF16),
                   jax.ShapeDtypeStruct((nseq, length, w), BF16),
                   jax.ShapeDtypeStruct((nseq, length // tm * 8, LANES), F32)],
        scratch_shapes=[pltpu.VMEM((1, LANES), F32)],
        compiler_params=_cparams(("parallel", "arbitrary")),
        name=name,
    )(lf, tri, selq, selk)


def _attn_kernel(first_ref, q_ref, qx_ref, k_ref, kx_ref, v_ref, o_ref, vt_ref, m_ref, acc_ref, s0_ref, s1_ref, *,
                 t, length):
    b, hd, qi = pl.program_id(0), pl.program_id(1), pl.program_id(2)

    @pl.when(qi == 0)
    def _():
        def transpose_chunk(c, carry):
            st = pl.multiple_of(c * t, t)
            vt_ref[0:HEAD_DIM, pl.ds(st, t)] = v_ref[0, pl.ds(st, t), :].T
            return carry

        lax.fori_loop(0, length // t, transpose_chunk, 0)
        sub = lax.broadcasted_iota(jnp.int32, (VT_ROWS - HEAD_DIM, length), 0)
        vt_ref[HEAD_DIM:, :] = jnp.where(sub == 0, 1.0, 0.0).astype(BF16)

    q = jnp.concatenate([q_ref[0], qx_ref[0]], axis=1)
    n_rest = qi - first_ref[(b * pl.num_programs(1) + hd) * pl.num_programs(2) + qi]
    m_ref[...] = jnp.full_like(m_ref, -jnp.inf)
    acc_ref[...] = jnp.zeros_like(acc_ref)

    def scores(j):
        st = pl.multiple_of(jnp.maximum(j, 0) * t, t)
        k = jnp.concatenate([k_ref[0, pl.ds(st, t), :], kx_ref[0, pl.ds(st, t), :]], axis=1)
        return lax.dot_general(k, q, (((1,), (1,)), ((), ())), preferred_element_type=F32)

    def accumulate(j, s, masked):
        st = pl.multiple_of(j * t, t)
        if masked:
            key_pos = lax.broadcasted_iota(jnp.int32, (t, t), 0)
            qry_pos = lax.broadcasted_iota(jnp.int32, (t, t), 1)
            s = jnp.where(key_pos <= qry_pos, s, -jnp.inf)
        m_prev = m_ref[...]
        m_new = jnp.maximum(m_prev, jnp.max(s, axis=0, keepdims=True))
        alpha = jnp.exp2(m_prev - m_new)
        p = jnp.exp2(s - m_new).astype(BF16)
        acc_ref[...] = alpha * acc_ref[...] + jnp.dot(vt_ref[:, pl.ds(st, t)], p, preferred_element_type=F32)
        m_ref[...] = m_new

    bufs = (s0_ref, s1_ref)

    def visit(k0, n, prefetch_after):
        for u in range(n):
            k = k0[0] + u
            if u + 1 < n or prefetch_after:
                bufs[(k0[1] + u + 1) % 2][...] = scores(qi - k - 1)
            accumulate(qi - k, bufs[(k0[1] + u) % 2][...], False)

    s0_ref[...] = scores(qi)

    @pl.when(n_rest == 0)
    def _():
        accumulate(qi, s0_ref[...], True)

    @pl.when(n_rest > 0)
    def _():
        s1_ref[...] = scores(qi - 1)
        accumulate(qi, s0_ref[...], True)

    def group(g, carry):
        visit((1 + ATTN_UNROLL * g, 1), ATTN_UNROLL, True)
        return carry

    lax.fori_loop(0, n_rest // ATTN_UNROLL, group, 0)
    rem = n_rest % ATTN_UNROLL
    for r in range(1, ATTN_UNROLL):
        @pl.when(rem == r)
        def _(r=r):
            visit((n_rest - r + 1, 1), r, False)

    acc = acc_ref[...]
    out_t = acc[0:HEAD_DIM] / acc[HEAD_DIM:HEAD_DIM + 1]
    o_ref[0] = out_t.T.astype(o_ref.dtype)


def _first_needed_chunk(qn2, kn2, ends, n_heads):
    nb, nt = ends.shape[0], ends.shape[1] // 8
    qn = jnp.sqrt(qn2.reshape(nb, nt, n_heads, LANES)[..., 0])
    kn = jnp.sqrt(kn2.reshape(nb, nt, n_heads, LANES)[..., 0])
    ends = ends.reshape(nb, nt, 8, LANES)
    c_first, c_last = ends[:, :, 0, :n_heads], ends[:, :, 1, :n_heads]
    ub = qn[:, :, None, :] * kn[:, None, :, :] + c_first[:, :, None, :] - c_last[:, None, :, :]
    lb = -(qn * kn)[:, :, None, :]
    i = jnp.arange(nt)[None, :, None, None]
    j = jnp.arange(nt)[None, None, :, None]
    needed = (ub - lb >= -PRUNE_GAP) & (j < i)
    first = jnp.min(jnp.where(needed, j, i), axis=2)
    return first.transpose(0, 2, 1).reshape(-1).astype(jnp.int32)


def _attn(first, q, qx, k, kx, v, *, name):
    nb, length, w = q.shape
    n_heads = w // HEAD_DIM
    t = ATTN_TILE
    assert length % t == 0 and k.shape[1] == length
    blk = lambda b, h, i, first: (b, i, h)
    kv_spec = pl.BlockSpec((1, length, HEAD_DIM), lambda b, h, i, first: (b, 0, h))
    return pl.pallas_call(
        functools.partial(_attn_kernel, t=t, length=length),
        grid_spec=pltpu.PrefetchScalarGridSpec(
            num_scalar_prefetch=1,
            grid=(nb, n_heads, length // t),
            in_specs=[pl.BlockSpec((1, t, HEAD_DIM), blk), pl.BlockSpec((1, t, HEAD_DIM), blk),
                      kv_spec, kv_spec, kv_spec],
            out_specs=pl.BlockSpec((1, t, HEAD_DIM), blk),
            scratch_shapes=[pltpu.VMEM((VT_ROWS, length), BF16), pltpu.VMEM((1, t), F32),
                            pltpu.VMEM((VT_ROWS, t), F32), pltpu.VMEM((t, t), F32), pltpu.VMEM((t, t), F32)],
        ),
        out_shape=jax.ShapeDtypeStruct((nb, length, w), BF16),
        compiler_params=_cparams(("parallel", "parallel", "arbitrary")),
        name=name,
    )(first, q, qx, k, kx, v)


def _attn_cached_kernel(q_ref, qx_ref, kx_ref, ck_ref, cv_ref, kn_ref, vn_ref, o_ref, *, n_heads, past, tn):
    pad = LANES - tn
    causal = (lax.broadcasted_iota(jnp.int32, (tn, LANES), 1) <= lax.broadcasted_iota(jnp.int32, (tn, LANES), 0))
    nt = (((1,), (1,)), ((), ()))
    for hd in range(n_heads):
        cols = slice(hd * HEAD_DIM, (hd + 1) * HEAD_DIM)
        q = jnp.concatenate([q_ref[:, cols], qx_ref[0, :, cols]], axis=1)
        kc = jnp.concatenate([ck_ref[pl.ds(hd, past, stride=n_heads), :].astype(BF16),
                              kx_ref[0, 0:past, cols]], axis=1)
        kn = jnp.concatenate([kn_ref[:, cols], kx_ref[0, past:past + tn, cols]], axis=1)
        kn = jnp.concatenate([kn, jnp.zeros((pad, kn.shape[1]), BF16)], axis=0)
        s_c = lax.dot_general(q, kc, nt, preferred_element_type=F32)
        s_n = jnp.where(causal, lax.dot_general(q, kn, nt, preferred_element_type=F32), -jnp.inf)
        m = jnp.maximum(jnp.max(s_c, axis=1, keepdims=True), jnp.max(s_n, axis=1, keepdims=True))
        p_c = jnp.exp2(s_c - m)
        p_n = jnp.exp2(s_n - m)
        denom = jnp.sum(p_c, axis=1, keepdims=True) + jnp.sum(p_n, axis=1, keepdims=True)
        vc = cv_ref[pl.ds(hd, past, stride=n_heads), :].astype(BF16)
        vn = jnp.concatenate([vn_ref[:, cols], jnp.zeros((pad, HEAD_DIM), BF16)], axis=0)
        out = (jnp.dot(p_c.astype(BF16), vc, preferred_element_type=F32)
               + jnp.dot(p_n.astype(BF16), vn, preferred_element_type=F32))
        o_ref[:, cols] = (out / denom).astype(o_ref.dtype)


def _attn_cached(q, qx, kx, cache_k, cache_v, k_new, v_new, *, n_seq, past, name):
    n, w = q.shape
    tn = n // n_seq
    n_heads = w // HEAD_DIM
    assert tn <= LANES and tn % 16 == 0 and past % tn == 0 and kx.shape[1] >= past + tn
    ck = cache_k.reshape(n_seq * past * n_heads, HEAD_DIM)
    cv = cache_v.reshape(n_seq * past * n_heads, HEAD_DIM)
    rows = pl.BlockSpec((tn, w), lambda b: (b, 0))
    cached = pl.BlockSpec((past * n_heads, HEAD_DIM), lambda b: (b, 0))
    return pl.pallas_call(
        functools.partial(_attn_cached_kernel, n_heads=n_heads, past=past, tn=tn),
        grid=(n_seq,),
        in_specs=[
            rows,
            pl.BlockSpec((1, tn, w), lambda b: (b, past // tn, 0)),
            pl.BlockSpec((1, kx.shape[1], w), lambda b: (b, 0, 0)),
            cached, cached, rows, rows,
        ],
        out_specs=rows,
        out_shape=jax.ShapeDtypeStruct((n, w), BF16),
        compiler_params=_cparams(("parallel",)),
        name=name,
    )(q, qx, kx, ck, cv, k_new, v_new)


def _pool_kernel(p_ref, hist_ref, w_ref, sc_ref, o_ref, ext_ref, *, tp, pos0):
    t = pl.program_id(1)

    @pl.when(t == 0)
    def _():
        ext_ref[0:HIST_ROWS, :] = hist_ref[0]

    ext_ref[HIST_ROWS:, :] = p_ref[0]
    pos = pos0 + t * tp + lax.broadcasted_iota(jnp.int32, (tp, 1), 0)
    gd = w_ref.shape[1]
    for g, win in enumerate(POOL_WINDOWS):
        cols = slice(g * gd, (g + 1) * gd)
        cur = ext_ref[HIST_ROWS:, cols]
        tot = cur
        for back in range(1, win):
            tot = tot + ext_ref[HIST_ROWS - back:HIST_ROWS - back + tp, cols]
        cnt = jnp.minimum(pos + 1, win).astype(F32)
        d = (tot / cnt - cur).astype(BF16)
        y = jnp.dot(d, w_ref[g], preferred_element_type=F32) * sc_ref[:, cols]
        o_ref[0, :, cols] = y.astype(o_ref.dtype)
    ext_ref[0:HIST_ROWS, :] = ext_ref[tp:tp + HIST_ROWS, :]


def _pool(p, hist, w_pool, scale, *, pos0, name):
    nseq, length, w = p.shape
    tp = min(512, length)
    assert length % tp == 0 and tp >= HIST_ROWS
    return pl.pallas_call(
        functools.partial(_pool_kernel, tp=tp, pos0=pos0),
        grid=(nseq, length // tp),
        in_specs=[
            pl.BlockSpec((1, tp, w), lambda b, t: (b, t, 0)),
            pl.BlockSpec((1, HIST_ROWS, w), lambda b, t: (b, 0, 0)),
            pl.BlockSpec(w_pool.shape, lambda b, t: (0, 0, 0)),
            pl.BlockSpec((1, w), lambda b, t: (0, 0)),
        ],
        out_specs=pl.BlockSpec((1, tp, w), lambda b, t: (b, t, 0)),
        out_shape=jax.ShapeDtypeStruct((nseq, length, w), BF16),
        scratch_shapes=[pltpu.VMEM((HIST_ROWS + tp, w), F32)],
        compiler_params=_cparams(("parallel", "arbitrary")),
        name=name,
    )(p, hist, w_pool, scale)


def _oproj_kernel(h_ref, fox_ref, pool_ref, wo_ref, o_ref):
    mix = jnp.concatenate([fox_ref[...], pool_ref[...]], axis=1)
    o_ref[...] = h_ref[...] + jnp.dot(mix, wo_ref[...], preferred_element_type=F32)


def _oproj(h, fox, pool, wo, *, name):
    n, d = h.shape
    tm = min(512, n)
    assert n % tm == 0
    return pl.pallas_call(
        _oproj_kernel,
        grid=(n // tm,),
        in_specs=[
            pl.BlockSpec((tm, d), lambda i: (i, 0)),
            pl.BlockSpec((tm, fox.shape[1]), lambda i: (i, 0)),
            pl.BlockSpec((tm, pool.shape[1]), lambda i: (i, 0)),
            pl.BlockSpec(wo.shape, lambda i: (0, 0)),
        ],
        out_specs=pl.BlockSpec((tm, d), lambda i: (i, 0)),
        out_shape=jax.ShapeDtypeStruct((n, d), F32),
        compiler_params=_cparams(("parallel",)),
        name=name,
    )(h, fox, pool, wo)


def _round_up(x, m):
    return (x + m - 1) // m * m


def kernel(x_prompt, x_sample, cache_k, cache_v, cache_logf, state_pool, g_ffn1, w1_gate, w1_up, w1_down, g_mix, w_in, b_f, w_pool, pool_scale, w_o, g_ffn2, w2_gate, w2_up, w2_down, g_final):
    B, S, D = x_prompt.shape
    Bs, Tn, _ = x_sample.shape
    depth, _, P, H, Dh = cache_k.shape
    assert depth == 1 and Dh == HEAD_DIM
    fw = H * Dh
    pw = w_pool.shape[1] * w_pool.shape[2]
    assert w_in.shape[2] == 3 * fw + H + pw and state_pool.shape[2] == POOL_HIST

    row = lambda a: a.reshape(1, -1).astype(F32)
    w1g, w1u, w1d = w1_gate[0].astype(BF16), w1_up[0].astype(BF16), w1_down[0].astype(BF16)
    w2g, w2u, w2d = w2_gate[0].astype(BF16), w2_up[0].astype(BF16), w2_down[0].astype(BF16)
    wi = w_in[0]
    w_qkvp = jnp.concatenate([wi[:, :3 * fw], wi[:, 3 * fw + H:]], axis=1).astype(BF16)
    wf_cols = jnp.tile(wi[:, 3 * fw:3 * fw + H], (1, N_SPLIT))
    wf = jnp.pad(wf_cols, ((0, 0), (0, LANES - N_SPLIT * H))).astype(BF16)
    bf = jnp.pad(jnp.tile(b_f[0].astype(F32), N_SPLIT), (0, LANES - N_SPLIT * H)).reshape(1, LANES)
    wo = w_o[0].astype(BF16)
    wp = w_pool[0].astype(BF16)
    g1, gm, g2, gfin, psc = row(g_ffn1[0]), row(g_mix[0]), row(g_ffn2[0]), row(g_final), row(pool_scale[0])

    def pre(x2d, tag):
        h = _ffn(x2d, g1, w1g, w1u, w1d, gfin, final_norm=False, name=f"ffn1_{tag}")
        return (h,) + tuple(_inproj(h, gm, w_qkvp, wf, bf, name=f"inproj_{tag}"))

    def post(h, fox, pool, tag):
        h2 = _oproj(h, fox, pool, wo, name=f"oproj_{tag}")
        return _ffn(h2, g2, w2g, w2u, w2d, gfin, final_norm=True, name=f"ffn2_{tag}")

    h, q, k, v, kb, vb, p, lf, qn2, kn2 = pre(x_prompt.reshape(B * S, D), "prompt")
    qx, kx, ends = _aug(lf.reshape(B, S, LANES), n_heads=H, tile_cap=ATTN_TILE, name="aug_prompt")
    first = _first_needed_chunk(qn2, kn2, ends, H)
    fox = _attn(first, q.reshape(B, S, fw), qx, kb.reshape(B, S, fw), kx, vb.reshape(B, S, fw), name="attn_prompt")
    p3 = p.reshape(B, S, pw)
    pool = _pool(p3, jnp.zeros((B, HIST_ROWS, pw), F32), wp, psc, pos0=0, name="pool_prompt")
    y_prompt = post(h, fox.reshape(B * S, fw), pool.reshape(B * S, pw), "prompt").reshape(B, S, D)
    k_prompt = k.reshape(1, B, S, H, Dh)
    v_prompt = v.reshape(1, B, S, H, Dh)
    logf_prompt = lf[:, :H].reshape(1, B, S, H)
    pool_prompt = p3[:, S - POOL_HIST:][None]

    h, q, k, v, kb, vb, p, lf, _, _ = pre(x_sample.reshape(Bs * Tn, D), "sample")
    L = _round_up(P + Tn, LANES)
    tail = L - P - Tn
    lf_cache = jnp.pad(jnp.tile(cache_logf[0].astype(F32), (1, 1, N_SPLIT)),
                       ((0, 0), (0, 0), (0, LANES - N_SPLIT * H)))
    lf_all = jnp.concatenate([lf_cache, lf.reshape(Bs, Tn, LANES), jnp.zeros((Bs, tail, LANES), F32)], axis=1)
    qx, kx, _ = _aug(lf_all, n_heads=H, tile_cap=L, name="aug_sample")
    fox = _attn_cached(q, qx, kx, cache_k[0], cache_v[0], kb, vb, n_seq=Bs, past=P, name="attn_sample")
    p3 = p.reshape(Bs, Tn, pw)
    hist = jnp.concatenate([jnp.zeros((Bs, HIST_ROWS - POOL_HIST, pw), F32), state_pool[0].astype(F32)], axis=1)
    pool = _pool(p3, hist, wp, psc, pos0=P, name="pool_sample")
    y_sample = post(h, fox, pool.reshape(Bs * Tn, pw), "sample").reshape(Bs, Tn, D)
    k_sample = k.reshape(1, Bs, Tn, H, Dh)
    v_sample = v.reshape(1, Bs, Tn, H, Dh)
    logf_sample = lf[:, :H].reshape(1, Bs, Tn, H)
    ext_tail = jnp.concatenate([state_pool[0].astype(F32), p3], axis=1)[:, Tn:]
    pool_sample = ext_tail[None]

    return (y_prompt, y_sample, k_prompt, v_prompt, logf_prompt, pool_prompt,
            k_sample, v_sample, logf_sample, pool_sample)
```

```python
import functools

import jax
import jax.numpy as jnp
from jax import lax
from jax.experimental import pallas as pl
from jax.experimental.pallas import tpu as pltpu

F32 = jnp.float32
BF16 = jnp.bfloat16

EPS = 1e-6
HEAD_DIM = 128
LANES = 128
POOL_WINDOWS = (2, 4, 8, 16)
POOL_HIST = max(POOL_WINDOWS) - 1
HIST_ROWS = 16
N_SPLIT = 3
VT_ROWS = HEAD_DIM + 16
LOG2E = 1.4426950408889634
VMEM_LIMIT = 56 * 1024 * 1024
FFN_VMEM_LIMIT = 62 * 1024 * 1024
ATTN_UNROLL = 4
ATTN_TILE = 512
PRUNE_GAP = 170.0
FFN_ROWS = 1024
FFN_COLS = 256
FFN_SLOTS = 4
FFN_NORM_ROWS = 256


def _cparams(sem, vmem_limit=VMEM_LIMIT):
    return pltpu.CompilerParams(dimension_semantics=sem, vmem_limit_bytes=vmem_limit)


def _pick_tile(length, cap, mult):
    best = None
    for t in range(mult, min(cap, length) + 1, mult):
        if length % t == 0:
            best = t
    assert best is not None, (length, cap, mult)
    return best


def _rms(x, g):
    return x * lax.rsqrt(jnp.mean(x * x, axis=-1, keepdims=True) + EPS) * g


def _split3(x):
    hi = x.astype(BF16).astype(F32)
    r1 = x - hi
    mid = r1.astype(BF16).astype(F32)
    lo = (r1 - mid).astype(BF16).astype(F32)
    return hi, mid, lo


def _ffn_kernel(x_ref, g_ref, gf_ref, wg_hbm, wu_hbm, wd_hbm, o_ref, xn_ref, wg_buf, wu_buf, wd_buf, sem, *,
                final_norm, tf, nf):
    def copies(f):
        slot = f % FFN_SLOTS
        cols = pl.ds(pl.multiple_of(f * tf, tf), tf)
        return (pltpu.make_async_copy(wg_hbm.at[:, cols], wg_buf.at[slot], sem.at[0, slot]),
                pltpu.make_async_copy(wu_hbm.at[:, cols], wu_buf.at[slot], sem.at[1, slot]),
                pltpu.make_async_copy(wd_hbm.at[cols, :], wd_buf.at[slot], sem.at[2, slot]))

    def start(f):
        for c in copies(f):
            c.start()

    def wait(f):
        for c in copies(f):
            c.wait()

    def chunk(f, xn):
        slot = f % FFN_SLOTS
        gate = jnp.dot(xn, wg_buf[slot], preferred_element_type=F32)
        up = jnp.dot(xn, wu_buf[slot], preferred_element_type=F32)
        a = (gate * jax.nn.sigmoid(gate) * up).astype(BF16)
        return jnp.dot(a, wd_buf[slot], preferred_element_type=F32)

    tile = pl.program_id(0)

    @pl.when(tile == 0)
    def _():
        start(0)

    start(1)
    start(2)
    wait(0)
    xn = _rms(x_ref[...], g_ref[...]).astype(BF16)
    xn_ref[...] = xn
    o_ref[...] = chunk(0, xn)

    def pair(i, carry):
        f = 1 + 2 * i
        wait(f)
        wait(f + 1)
        start(f + 2)

        @pl.when(f + 3 < nf)
        def _():
            start(f + 3)

        xn = xn_ref[...]
        o_ref[...] += chunk(f, xn)
        o_ref[...] += chunk(f + 1, xn)
        return carry

    lax.fori_loop(0, (nf - 2) // 2, pair, 0)
    wait(nf - 1)

    @pl.when(tile + 1 < pl.num_programs(0))
    def _():
        start(0)

    o_ref[...] = x_ref[...] + 0.5 * (o_ref[...] + chunk(nf - 1, xn_ref[...]))
    if final_norm:
        rows = min(FFN_NORM_ROWS, o_ref.shape[0])
        for r in range(0, o_ref.shape[0], rows):
            o_ref[r:r + rows, :] = _rms(o_ref[r:r + rows, :], gf_ref[...])


def _ffn(x, g, wg, wu, wd, gf, *, final_norm, name):
    n, d = x.shape
    dff = wg.shape[1]
    tm = min(FFN_ROWS, n)
    tf = FFN_COLS
    nf = dff // tf
    assert n % tm == 0 and dff % tf == 0 and nf >= 4 and nf % 2 == 0
    return pl.pallas_call(
        functools.partial(_ffn_kernel, final_norm=final_norm, tf=tf, nf=nf),
        grid=(n // tm,),
        in_specs=[
            pl.BlockSpec((tm, d), lambda i: (i, 0)),
            pl.BlockSpec((1, d), lambda i: (0, 0)),
            pl.BlockSpec((1, d), lambda i: (0, 0)),
            pl.BlockSpec(memory_space=pl.ANY),
            pl.BlockSpec(memory_space=pl.ANY),
            pl.BlockSpec(memory_space=pl.ANY),
        ],
        out_specs=pl.BlockSpec((tm, d), lambda i: (i, 0)),
        out_shape=jax.ShapeDtypeStruct((n, d), F32),
        scratch_shapes=[pltpu.VMEM((tm, d), BF16),
                        pltpu.VMEM((FFN_SLOTS, d, tf), BF16), pltpu.VMEM((FFN_SLOTS, d, tf), BF16),
                        pltpu.VMEM((FFN_SLOTS, tf, d), BF16), pltpu.SemaphoreType.DMA((3, FFN_SLOTS))],
        compiler_params=_cparams(("arbitrary",), FFN_VMEM_LIMIT),
        name=name,
    )(x, g, gf, wg, wu, wd)


def _inproj_kernel(h_ref, g_ref, w_ref, wf_ref, bf_ref,
                   q_ref, k_ref, v_ref, kb_ref, vb_ref, p_ref, lf_ref, qn_ref, kn_ref, *, scale, width):
    hn = _rms(h_ref[...], g_ref[...]).astype(BF16)
    n_heads = width // HEAD_DIM

    def proj(c):
        return jnp.dot(hn, w_ref[:, c * width:(c + 1) * width], preferred_element_type=F32)

    def store_heads(o_ref, u):
        for hd in range(n_heads):
            o_ref[pl.ds(hd, u.shape[0], stride=n_heads), :] = u[:, hd * HEAD_DIM:(hd + 1) * HEAD_DIM]

    def max_sq_norms(xb):
        sub = lax.broadcasted_iota(jnp.int32, (n_heads, LANES), 0)
        out = jnp.zeros((n_heads, LANES), F32)
        for hd in range(n_heads):
            x = xb[:, hd * HEAD_DIM:(hd + 1) * HEAD_DIM].astype(F32)
            top = jnp.max(jnp.sum(x * x, axis=1, keepdims=True), axis=0, keepdims=True)
            out = jnp.where(sub == hd, top, out)
        return out

    lf_ref[...] = jax.nn.log_sigmoid(jnp.dot(hn, wf_ref[...], preferred_element_type=F32) + bf_ref[...])
    qb = (proj(0) * scale).astype(BF16)
    q_ref[...] = qb
    qn_ref[...] = max_sq_norms(qb)
    u = proj(1)
    store_heads(k_ref, u)
    kb = u.astype(BF16)
    kb_ref[...] = kb
    kn_ref[...] = max_sq_norms(kb)
    u = proj(2)
    store_heads(v_ref, u)
    vb_ref[...] = u.astype(BF16)
    p_ref[...] = proj(3)


def _inproj(h, g, w_qkvp, wf, bf, *, name):
    n, d = h.shape
    w = w_qkvp.shape[1] // 4
    tm = min(ATTN_TILE, n)
    assert n % tm == 0 and w % HEAD_DIM == 0
    n_heads = w // HEAD_DIM
    assert n_heads == 8
    row = lambda i: (i, 0)
    const = lambda i: (0, 0)
    resident = pl.Buffered(1)
    return pl.pallas_call(
        functools.partial(_inproj_kernel, scale=HEAD_DIM ** -0.5 * LOG2E, width=w),
        grid=(n // tm,),
        in_specs=[
            pl.BlockSpec((tm, d), row),
            pl.BlockSpec((1, d), const),
            pl.BlockSpec((d, 4 * w), const, pipeline_mode=resident),
            pl.BlockSpec((d, LANES), const),
            pl.BlockSpec((1, LANES), const),
        ],
        out_specs=[
            pl.BlockSpec((tm, w), row),
            pl.BlockSpec((tm * n_heads, HEAD_DIM), row), pl.BlockSpec((tm * n_heads, HEAD_DIM), row),
            pl.BlockSpec((tm, w), row), pl.BlockSpec((tm, w), row), pl.BlockSpec((tm, w), row),
            pl.BlockSpec((tm, LANES), row),
            pl.BlockSpec((n_heads, LANES), row), pl.BlockSpec((n_heads, LANES), row),
        ],
        out_shape=[
            jax.ShapeDtypeStruct((n, w), BF16),
            jax.ShapeDtypeStruct((n * n_heads, HEAD_DIM), F32),
            jax.ShapeDtypeStruct((n * n_heads, HEAD_DIM), F32),
            jax.ShapeDtypeStruct((n, w), BF16),
            jax.ShapeDtypeStruct((n, w), BF16),
            jax.ShapeDtypeStruct((n, w), F32),
            jax.ShapeDtypeStruct((n, LANES), F32),
            jax.ShapeDtypeStruct((n // tm * n_heads, LANES), F32),
            jax.ShapeDtypeStruct((n // tm * n_heads, LANES), F32),
        ],
        compiler_params=_cparams(("parallel",)),
        name=name,
    )(h, g, w_qkvp, wf, bf)


def _aug_kernel(lf_ref, tri_ref, selq_ref, selk_ref, qx_ref, kx_ref, ends_ref, carry_ref, *, n_heads):
    t = pl.program_id(1)

    @pl.when(t == 0)
    def _():
        carry_ref[...] = jnp.zeros_like(carry_ref)

    lf = lf_ref[0]
    lane = lax.broadcasted_iota(jnp.int32, lf.shape, 1)

    def pick(parts, extra=None):
        out = jnp.zeros_like(lf) if extra is None else extra
        for j in reversed(range(N_SPLIT)):
            out = jnp.where((lane >= j * n_heads) & (lane < (j + 1) * n_heads), parts[j], out)
        return out

    cs = jnp.dot(tri_ref[...], pick(_split3(lf)).astype(BF16), preferred_element_type=F32)
    c = cs
    for j in range(1, N_SPLIT):
        c = c + pltpu.roll(cs, LANES - j * n_heads, axis=1)
    c = c + carry_ref[...]
    carry_ref[...] = c[c.shape[0] - 1:, :]

    crep = c
    for j in range(1, N_SPLIT):
        crep = jnp.where((lane >= j * n_heads) & (lane < (j + 1) * n_heads),
                         pltpu.roll(c, j * n_heads, axis=1), crep)
    ones = jnp.where(lane == N_SPLIT * n_heads, 1.0, 0.0).astype(F32)
    pieces = pick(_split3(crep * LOG2E), extra=ones).astype(BF16)
    qx_ref[0] = jnp.dot(pieces, selq_ref[...], preferred_element_type=F32).astype(BF16)
    kx_ref[0] = jnp.dot(pieces, selk_ref[...], preferred_element_type=F32).astype(BF16)
    sub = lax.broadcasted_iota(jnp.int32, (8, LANES), 0)
    ends_ref[0] = jnp.where(sub == 0, c[0:1, :] * LOG2E, jnp.where(sub == 1, c[c.shape[0] - 1:, :] * LOG2E, 0.0))


def _aug_constants(tm, n_heads):
    r = jnp.arange(tm)
    tri = (r[:, None] >= r[None, :]).astype(BF16)
    rows = jnp.arange(LANES)[:, None]
    cols = jnp.arange(n_heads * HEAD_DIM)[None, :]
    head, lane = cols // HEAD_DIM, cols % HEAD_DIM
    one_row = N_SPLIT * n_heads
    is_piece = lambda off: (lane >= off) & (lane < off + N_SPLIT) & (rows == (lane - off) * n_heads + head)
    is_one = lambda off: (lane >= off) & (lane < off + N_SPLIT) & (rows == one_row)
    selq = (is_piece(0) | is_one(N_SPLIT)).astype(BF16)
    selk = is_one(0).astype(BF16) - is_piece(N_SPLIT).astype(BF16)
    return tri, selq, selk


def _aug(lf, *, n_heads, tile_cap, name):
    nseq, length, _ = lf.shape
    tm = _pick_tile(length, tile_cap, 16)
    assert (N_SPLIT * n_heads) < LANES
    w = n_heads * HEAD_DIM
    tri, selq, selk = _aug_constants(tm, n_heads)
    return pl.pallas_call(
        functools.partial(_aug_kernel, n_heads=n_heads),
        grid=(nseq, length // tm),
        in_specs=[
            pl.BlockSpec((1, tm, LANES), lambda b, t: (b, t, 0)),
            pl.BlockSpec((tm, tm), lambda b, t: (0, 0)),
            pl.BlockSpec((LANES, w), lambda b, t: (0, 0)),
            pl.BlockSpec((LANES, w), lambda b, t: (0, 0)),
        ],
        out_specs=[pl.BlockSpec((1, tm, w), lambda b, t: (b, t, 0)),
                   pl.BlockSpec((1, tm, w), lambda b, t: (b, t, 0)),
                   pl.BlockSpec((1, 8, LANES), lambda b, t: (b, t, 0))],
        out_shape=[jax.ShapeDtypeStruct((nseq, length, w), BF16),
                   jax.ShapeDtypeStruct((nseq, length, w), BF16),
                   jax.ShapeDtypeStruct((nseq, length // tm * 8, LANES), F32)],
        scratch_shapes=[pltpu.VMEM((1, LANES), F32)],
        compiler_params=_cparams(("parallel", "arbitrary")),
        name=name,
    )(lf, tri, selq, selk)


def _attn_kernel(first_ref, q_ref, qx_ref, k_ref, kx_ref, v_ref, o_ref, vt_ref, m_ref, acc_ref, s0_ref, s1_ref, *,
                 t, length):
    b, hd, qi = pl.program_id(0), pl.program_id(1), pl.program_id(2)

    @pl.when(qi == 0)
    def _():
        def transpose_chunk(c, carry):
            st = pl.multiple_of(c * t, t)
            vt_ref[0:HEAD_DIM, pl.ds(st, t)] = v_ref[0, pl.ds(st, t), :].T
            return carry

        lax.fori_loop(0, length // t, transpose_chunk, 0)
        sub = lax.broadcasted_iota(jnp.int32, (VT_ROWS - HEAD_DIM, length), 0)
        vt_ref[HEAD_DIM:, :] = jnp.where(sub == 0, 1.0, 0.0).astype(BF16)

    q = jnp.concatenate([q_ref[0], qx_ref[0]], axis=1)
    n_rest = qi - first_ref[(b * pl.num_programs(1) + hd) * pl.num_programs(2) + qi]
    m_ref[...] = jnp.full_like(m_ref, -jnp.inf)
    acc_ref[...] = jnp.zeros_like(acc_ref)

    def scores(j):
        st = pl.multiple_of(jnp.maximum(j, 0) * t, t)
        k = jnp.concatenate([k_ref[0, pl.ds(st, t), :], kx_ref[0, pl.ds(st, t), :]], axis=1)
        return lax.dot_general(k, q, (((1,), (1,)), ((), ())), preferred_element_type=F32)

    def accumulate(j, s, masked):
        st = pl.multiple_of(j * t, t)
        if masked:
            key_pos = lax.broadcasted_iota(jnp.int32, (t, t), 0)
            qry_pos = lax.broadcasted_iota(jnp.int32, (t, t), 1)
            s = jnp.where(key_pos <= qry_pos, s, -jnp.inf)
        m_prev = m_ref[...]
        m_new = jnp.maximum(m_prev, jnp.max(s, axis=0, keepdims=True))
        alpha = jnp.exp2(m_prev - m_new)
        p = jnp.exp2(s - m_new).astype(BF16)
        acc_ref[...] = alpha * acc_ref[...] + jnp.dot(vt_ref[:, pl.ds(st, t)], p, preferred_element_type=F32)
        m_ref[...] = m_new

    bufs = (s0_ref, s1_ref)

    def visit(k0, n, prefetch_after):
        for u in range(n):
            k = k0[0] + u
            if u + 1 < n or prefetch_after:
                bufs[(k0[1] + u + 1) % 2][...] = scores(qi - k - 1)
            accumulate(qi - k, bufs[(k0[1] + u) % 2][...], False)

    s0_ref[...] = scores(qi)

    @pl.when(n_rest == 0)
    def _():
        accumulate(qi, s0_ref[...], True)

    @pl.when(n_rest > 0)
    def _():
        s1_ref[...] = scores(qi - 1)
        accumulate(qi, s0_ref[...], True)

    def group(g, carry):
        visit((1 + ATTN_UNROLL * g, 1), ATTN_UNROLL, True)
        return carry

    lax.fori_loop(0, n_rest // ATTN_UNROLL, group, 0)
    rem = n_rest % ATTN_UNROLL
    for r in range(1, ATTN_UNROLL):
        @pl.when(rem == r)
        def _(r=r):
            visit((n_rest - r + 1, 1), r, False)

    acc = acc_ref[...]
    out_t = acc[0:HEAD_DIM] / acc[HEAD_DIM:HEAD_DIM + 1]
    o_ref[0] = out_t.T.astype(o_ref.dtype)


def _first_needed_chunk(qn2, kn2, ends, n_heads):
    nb, nt = ends.shape[0], ends.shape[1] // 8
    qn = jnp.sqrt(qn2.reshape(nb, nt, n_heads, LANES)[..., 0])
    kn = jnp.sqrt(kn2.reshape(nb, nt, n_heads, LANES)[..., 0])
    ends = ends.reshape(nb, nt, 8, LANES)
    c_first, c_last = ends[:, :, 0, :n_heads], ends[:, :, 1, :n_heads]
    ub = qn[:, :, None, :] * kn[:, None, :, :] + c_first[:, :, None, :] - c_last[:, None, :, :]
    lb = -(qn * kn)[:, :, None, :]
    i = jnp.arange(nt)[None, :, None, None]
    j = jnp.arange(nt)[None, None, :, None]
    needed = (ub - lb >= -PRUNE_GAP) & (j < i)
    first = jnp.min(jnp.where(needed, j, i), axis=2)
    return first.transpose(0, 2, 1).reshape(-1).astype(jnp.int32)


def _attn(first, q, qx, k, kx, v, *, name):
    nb, length, w = q.shape
    n_heads = w // HEAD_DIM
    t = ATTN_TILE
    assert length % t == 0 and k.shape[1] == length
    blk = lambda b, h, i, first: (b, i, h)
    kv_spec = pl.BlockSpec((1, length, HEAD_DIM), lambda b, h, i, first: (b, 0, h))
    return pl.pallas_call(
        functools.partial(_attn_kernel, t=t, length=length),
        grid_spec=pltpu.PrefetchScalarGridSpec(
            num_scalar_prefetch=1,
            grid=(nb, n_heads, length // t),
            in_specs=[pl.BlockSpec((1, t, HEAD_DIM), blk), pl.BlockSpec((1, t, HEAD_DIM), blk),
                      kv_spec, kv_spec, kv_spec],
            out_specs=pl.BlockSpec((1, t, HEAD_DIM), blk),
            scratch_shapes=[pltpu.VMEM((VT_ROWS, length), BF16), pltpu.VMEM((1, t), F32),
                            pltpu.VMEM((VT_ROWS, t), F32), pltpu.VMEM((t, t), F32), pltpu.VMEM((t, t), F32)],
        ),
        out_shape=jax.ShapeDtypeStruct((nb, length, w), BF16),
        compiler_params=_cparams(("parallel", "parallel", "arbitrary")),
        name=name,
    )(first, q, qx, k, kx, v)


def _attn_cached_kernel(q_ref, qx_ref, kx_ref, ck_ref, cv_ref, kn_ref, vn_ref, o_ref, *, n_heads, past, tn):
    pad = LANES - tn
    causal = (lax.broadcasted_iota(jnp.int32, (tn, LANES), 1) <= lax.broadcasted_iota(jnp.int32, (tn, LANES), 0))
    nt = (((1,), (1,)), ((), ()))
    for hd in range(n_heads):
        cols = slice(hd * HEAD_DIM, (hd + 1) * HEAD_DIM)
        q = jnp.concatenate([q_ref[:, cols], qx_ref[0, :, cols]], axis=1)
        kc = jnp.concatenate([ck_ref[pl.ds(hd, past, stride=n_heads), :].astype(BF16),
                              kx_ref[0, 0:past, cols]], axis=1)
        kn = jnp.concatenate([kn_ref[:, cols], kx_ref[0, past:past + tn, cols]], axis=1)
        kn = jnp.concatenate([kn, jnp.zeros((pad, kn.shape[1]), BF16)], axis=0)
        s_c = lax.dot_general(q, kc, nt, preferred_element_type=F32)
        s_n = jnp.where(causal, lax.dot_general(q, kn, nt, preferred_element_type=F32), -jnp.inf)
        m = jnp.maximum(jnp.max(s_c, axis=1, keepdims=True), jnp.max(s_n, axis=1, keepdims=True))
        p_c = jnp.exp2(s_c - m)
        p_n = jnp.exp2(s_n - m)
        denom = jnp.sum(p_c, axis=1, keepdims=True) + jnp.sum(p_n, axis=1, keepdims=True)
        vc = cv_ref[pl.ds(hd, past, stride=n_heads), :].astype(BF16)
        vn = jnp.concatenate([vn_ref[:, cols], jnp.zeros((pad, HEAD_DIM), BF16)], axis=0)
        out = (jnp.dot(p_c.astype(BF16), vc, preferred_element_type=F32)
               + jnp.dot(p_n.astype(BF16), vn, preferred_element_type=F32))
        o_ref[:, cols] = (out / denom).astype(o_ref.dtype)


def _attn_cached(q, qx, kx, cache_k, cache_v, k_new, v_new, *, n_seq, past, name):
    n, w = q.shape
    tn = n // n_seq
    n_heads = w // HEAD_DIM
    assert tn <= LANES and tn % 16 == 0 and past % tn == 0 and kx.shape[1] >= past + tn
    ck = cache_k.reshape(n_seq * past * n_heads, HEAD_DIM)
    cv = cache_v.reshape(n_seq * past * n_heads, HEAD_DIM)
    rows = pl.BlockSpec((tn, w), lambda b: (b, 0))
    cached = pl.BlockSpec((past * n_heads, HEAD_DIM), lambda b: (b, 0))
    return pl.pallas_call(
        functools.partial(_attn_cached_kernel, n_heads=n_heads, past=past, tn=tn),
        grid=(n_seq,),
        in_specs=[
            rows,
            pl.BlockSpec((1, tn, w), lambda b: (b, past // tn, 0)),
            pl.BlockSpec((1, kx.shape[1], w), lambda b: (b, 0, 0)),
            cached, cached, rows, rows,
        ],
        out_specs=rows,
        out_shape=jax.ShapeDtypeStruct((n, w), BF16),
        compiler_params=_cparams(("parallel",)),
        name=name,
    )(q, qx, kx, ck, cv, k_new, v_new)


def _pool_kernel(p_ref, hist_ref, w_ref, sc_ref, o_ref, ext_ref, *, tp, pos0):
    t = pl.program_id(1)

    @pl.when(t == 0)
    def _():
        ext_ref[0:HIST_ROWS, :] = hist_ref[0]

    ext_ref[HIST_ROWS:, :] = p_ref[0]
    pos = pos0 + t * tp + lax.broadcasted_iota(jnp.int32, (tp, 1), 0)
    gd = w_ref.shape[1]
    for g, win in enumerate(POOL_WINDOWS):
        cols = slice(g * gd, (g + 1) * gd)
        cur = ext_ref[HIST_ROWS:, cols]
        tot = cur
        for back in range(1, win):
            tot = tot + ext_ref[HIST_ROWS - back:HIST_ROWS - back + tp, cols]
        cnt = jnp.minimum(pos + 1, win).astype(F32)
        d = (tot / cnt - cur).astype(BF16)
        y = jnp.dot(d, w_ref[g], preferred_element_type=F32) * sc_ref[:, cols]
        o_ref[0, :, cols] = y.astype(o_ref.dtype)
    ext_ref[0:HIST_ROWS, :] = ext_ref[tp:tp + HIST_ROWS, :]


def _pool(p, hist, w_pool, scale, *, pos0, name):
    nseq, length, w = p.shape
    tp = min(512, length)
    assert length % tp == 0 and tp >= HIST_ROWS
    return pl.pallas_call(
        functools.partial(_pool_kernel, tp=tp, pos0=pos0),
        grid=(nseq, length // tp),
        in_specs=[
            pl.BlockSpec((1, tp, w), lambda b, t: (b, t, 0)),
            pl.BlockSpec((1, HIST_ROWS, w), lambda b, t: (b, 0, 0)),
            pl.BlockSpec(w_pool.shape, lambda b, t: (0, 0, 0)),
            pl.BlockSpec((1, w), lambda b, t: (0, 0)),
        ],
        out_specs=pl.BlockSpec((1, tp, w), lambda b, t: (b, t, 0)),
        out_shape=jax.ShapeDtypeStruct((nseq, length, w), BF16),
        scratch_shapes=[pltpu.VMEM((HIST_ROWS + tp, w), F32)],
        compiler_params=_cparams(("parallel", "arbitrary")),
        name=name,
    )(p, hist, w_pool, scale)


def _oproj_kernel(h_ref, fox_ref, pool_ref, wo_ref, o_ref):
    mix = jnp.concatenate([fox_ref[...], pool_ref[...]], axis=1)
    o_ref[...] = h_ref[...] + jnp.dot(mix, wo_ref[...], preferred_element_type=F32)


def _oproj(h, fox, pool, wo, *, name):
    n, d = h.shape
    tm = min(512, n)
    assert n % tm == 0
    return pl.pallas_call(
        _oproj_kernel,
        grid=(n // tm,),
        in_specs=[
            pl.BlockSpec((tm, d), lambda i: (i, 0)),
            pl.BlockSpec((tm, fox.shape[1]), lambda i: (i, 0)),
            pl.BlockSpec((tm, pool.shape[1]), lambda i: (i, 0)),
            pl.BlockSpec(wo.shape, lambda i: (0, 0)),
        ],
        out_specs=pl.BlockSpec((tm, d), lambda i: (i, 0)),
        out_shape=jax.ShapeDtypeStruct((n, d), F32),
        compiler_params=_cparams(("parallel",)),
        name=name,
    )(h, fox, pool, wo)


def _round_up(x, m):
    return (x + m - 1) // m * m


def kernel(x_prompt, x_sample, cache_k, cache_v, cache_logf, state_pool, g_ffn1, w1_gate, w1_up, w1_down, g_mix, w_in, b_f, w_pool, pool_scale, w_o, g_ffn2, w2_gate, w2_up, w2_down, g_final):
    B, S, D = x_prompt.shape
    Bs, Tn, _ = x_sample.shape
    depth, _, P, H, Dh = cache_k.shape
    assert depth == 1 and Dh == HEAD_DIM
    fw = H * Dh
    pw = w_pool.shape[1] * w_pool.shape[2]
    assert w_in.shape[2] == 3 * fw + H + pw and state_pool.shape[2] == POOL_HIST

    row = lambda a: a.reshape(1, -1).astype(F32)
    w1g, w1u, w1d = w1_gate[0].astype(BF16), w1_up[0].astype(BF16), w1_down[0].astype(BF16)
    w2g, w2u, w2d = w2_gate[0].astype(BF16), w2_up[0].astype(BF16), w2_down[0].astype(BF16)
    wi = w_in[0]
    w_qkvp = jnp.concatenate([wi[:, :3 * fw], wi[:, 3 * fw + H:]], axis=1).astype(BF16)
    wf_cols = jnp.tile(wi[:, 3 * fw:3 * fw + H], (1, N_SPLIT))
    wf = jnp.pad(wf_cols, ((0, 0), (0, LANES - N_SPLIT * H))).astype(BF16)
    bf = jnp.pad(jnp.tile(b_f[0].astype(F32), N_SPLIT), (0, LANES - N_SPLIT * H)).reshape(1, LANES)
    wo = w_o[0].astype(BF16)
    wp = w_pool[0].astype(BF16)
    g1, gm, g2, gfin, psc = row(g_ffn1[0]), row(g_mix[0]), row(g_ffn2[0]), row(g_final), row(pool_scale[0])

    def pre(x2d, tag):
        h = _ffn(x2d, g1, w1g, w1u, w1d, gfin, final_norm=False, name=f"ffn1_{tag}")
        return (h,) + tuple(_inproj(h, gm, w_qkvp, wf, bf, name=f"inproj_{tag}"))

    def post(h, fox, pool, tag):
        h2 = _oproj(h, fox, pool, wo, name=f"oproj_{tag}")
        return _ffn(h2, g2, w2g, w2u, w2d, gfin, final_norm=True, name=f"ffn2_{tag}")

    h, q, k, v, kb, vb, p, lf, qn2, kn2 = pre(x_prompt.reshape(B * S, D), "prompt")
    qx, kx, ends = _aug(lf.reshape(B, S, LANES), n_heads=H, tile_cap=ATTN_TILE, name="aug_prompt")
    first = _first_needed_chunk(qn2, kn2, ends, H)
    fox = _attn(first, q.reshape(B, S, fw), qx, kb.reshape(B, S, fw), kx, vb.reshape(B, S, fw), name="attn_prompt")
    p3 = p.reshape(B, S, pw)
    pool = _pool(p3, jnp.zeros((B, HIST_ROWS, pw), F32), wp, psc, pos0=0, name="pool_prompt")
    y_prompt = post(h, fox.reshape(B * S, fw), pool.reshape(B * S, pw), "prompt").reshape(B, S, D)
    k_prompt = k.reshape(1, B, S, H, Dh)
    v_prompt = v.reshape(1, B, S, H, Dh)
    logf_prompt = lf[:, :H].reshape(1, B, S, H)
    pool_prompt = p3[:, S - POOL_HIST:][None]

    h, q, k, v, kb, vb, p, lf, _, _ = pre(x_sample.reshape(Bs * Tn, D), "sample")
    L = _round_up(P + Tn, LANES)
    tail = L - P - Tn
    lf_cache = jnp.pad(jnp.tile(cache_logf[0].astype(F32), (1, 1, N_SPLIT)),
                       ((0, 0), (0, 0), (0, LANES - N_SPLIT * H)))
    lf_all = jnp.concatenate([lf_cache, lf.reshape(Bs, Tn, LANES), jnp.zeros((Bs, tail, LANES), F32)], axis=1)
    qx, kx, _ = _aug(lf_all, n_heads=H, tile_cap=L, name="aug_sample")
    fox = _attn_cached(q, qx, kx, cache_k[0], cache_v[0], kb, vb, n_seq=Bs, past=P, name="attn_sample")
    p3 = p.reshape(Bs, Tn, pw)
    hist = jnp.concatenate([jnp.zeros((Bs, HIST_ROWS - POOL_HIST, pw), F32), state_pool[0].astype(F32)], axis=1)
    pool = _pool(p3, hist, wp, psc, pos0=P, name="pool_sample")
    y_sample = post(h, fox, pool.reshape(Bs * Tn, pw), "sample").reshape(Bs, Tn, D)
    k_sample = k.reshape(1, Bs, Tn, H, Dh)
    v_sample = v.reshape(1, Bs, Tn, H, Dh)
    logf_sample = lf[:, :H].reshape(1, Bs, Tn, H)
    ext_tail = jnp.concatenate([state_pool[0].astype(F32), p3], axis=1)[:, Tn:]
    pool_sample = ext_tail[None]

    return (y_prompt, y_sample, k_prompt, v_prompt, logf_prompt, pool_prompt,
            k_sample, v_sample, logf_sample, pool_sample)
```

```python
import functools

import jax
import jax.numpy as jnp
from jax import lax
from jax.experimental import pallas as pl
from jax.experimental.pallas import tpu as pltpu

F32 = jnp.float32
BF16 = jnp.bfloat16

EPS = 1e-6
HEAD_DIM = 128
LANES = 128
POOL_WINDOWS = (2, 4, 8, 16)
POOL_HIST = max(POOL_WINDOWS) - 1
HIST_ROWS = 16
N_SPLIT = 3
VT_ROWS = HEAD_DIM + 16
LOG2E = 1.4426950408889634
VMEM_LIMIT = 56 * 1024 * 1024
FFN_VMEM_LIMIT = 62 * 1024 * 1024
ATTN_UNROLL = 4
ATTN_FLAT = 6
ATTN_TILE = 512
PRUNE_GAP = 170.0
FFN_ROWS = 1024
FFN_COLS = 256
FFN_SLOTS = 4
FFN_NORM_ROWS = 256


def _cparams(sem, vmem_limit=VMEM_LIMIT):
    return pltpu.CompilerParams(dimension_semantics=sem, vmem_limit_bytes=vmem_limit)


def _pick_tile(length, cap, mult):
    best = None
    for t in range(mult, min(cap, length) + 1, mult):
        if length % t == 0:
            best = t
    assert best is not None, (length, cap, mult)
    return best


def _rms(x, g):
    return x * lax.rsqrt(jnp.mean(x * x, axis=-1, keepdims=True) + EPS) * g


def _split3(x):
    hi = x.astype(BF16).astype(F32)
    r1 = x - hi
    mid = r1.astype(BF16).astype(F32)
    lo = (r1 - mid).astype(BF16).astype(F32)
    return hi, mid, lo


def _ffn_kernel(x_ref, g_ref, gf_ref, wg_hbm, wu_hbm, wd_hbm, o_ref, xn_ref, wg_buf, wu_buf, wd_buf, sem, *,
                final_norm, tf, nf):
    def copies(f):
        slot = f % FFN_SLOTS
        cols = pl.ds(pl.multiple_of(f * tf, tf), tf)
        return (pltpu.make_async_copy(wg_hbm.at[:, cols], wg_buf.at[slot], sem.at[0, slot]),
                pltpu.make_async_copy(wu_hbm.at[:, cols], wu_buf.at[slot], sem.at[1, slot]),
                pltpu.make_async_copy(wd_hbm.at[cols, :], wd_buf.at[slot], sem.at[2, slot]))

    def start(f):
        for c in copies(f):
            c.start()

    def wait(f):
        for c in copies(f):
            c.wait()

    def chunk(f, xn):
        slot = f % FFN_SLOTS
        gate = jnp.dot(xn, wg_buf[slot], preferred_element_type=F32)
        up = jnp.dot(xn, wu_buf[slot], preferred_element_type=F32)
        a = (gate * jax.nn.sigmoid(gate) * up).astype(BF16)
        return jnp.dot(a, wd_buf[slot], preferred_element_type=F32)

    tile = pl.program_id(0)

    @pl.when(tile == 0)
    def _():
        start(0)

    start(1)
    start(2)
    wait(0)
    xn = _rms(x_ref[...], g_ref[...]).astype(BF16)
    xn_ref[...] = xn
    o_ref[...] = chunk(0, xn)

    def pair(i, carry):
        f = 1 + 2 * i
        wait(f)
        wait(f + 1)
        start(f + 2)

        @pl.when(f + 3 < nf)
        def _():
            start(f + 3)

        xn = xn_ref[...]
        o_ref[...] += chunk(f, xn)
        o_ref[...] += chunk(f + 1, xn)
        return carry

    lax.fori_loop(0, (nf - 2) // 2, pair, 0)
    wait(nf - 1)

    @pl.when(tile + 1 < pl.num_programs(0))
    def _():
        start(0)

    o_ref[...] = x_ref[...] + 0.5 * (o_ref[...] + chunk(nf - 1, xn_ref[...]))
    if final_norm:
        rows = min(FFN_NORM_ROWS, o_ref.shape[0])
        for r in range(0, o_ref.shape[0], rows):
            o_ref[r:r + rows, :] = _rms(o_ref[r:r + rows, :], gf_ref[...])


def _ffn(x, g, wg, wu, wd, gf, *, final_norm, name):
    n, d = x.shape
    dff = wg.shape[1]
    tm = min(FFN_ROWS, n)
    tf = FFN_COLS
    nf = dff // tf
    assert n % tm == 0 and dff % tf == 0 and nf >= 4 and nf % 2 == 0
    return pl.pallas_call(
        functools.partial(_ffn_kernel, final_norm=final_norm, tf=tf, nf=nf),
        grid=(n // tm,),
        in_specs=[
            pl.BlockSpec((tm, d), lambda i: (i, 0)),
            pl.BlockSpec((1, d), lambda i: (0, 0)),
            pl.BlockSpec((1, d), lambda i: (0, 0)),
            pl.BlockSpec(memory_space=pl.ANY),
            pl.BlockSpec(memory_space=pl.ANY),
            pl.BlockSpec(memory_space=pl.ANY),
        ],
        out_specs=pl.BlockSpec((tm, d), lambda i: (i, 0)),
        out_shape=jax.ShapeDtypeStruct((n, d), F32),
        scratch_shapes=[pltpu.VMEM((tm, d), BF16),
                        pltpu.VMEM((FFN_SLOTS, d, tf), BF16), pltpu.VMEM((FFN_SLOTS, d, tf), BF16),
                        pltpu.VMEM((FFN_SLOTS, tf, d), BF16), pltpu.SemaphoreType.DMA((3, FFN_SLOTS))],
        compiler_params=_cparams(("arbitrary",), FFN_VMEM_LIMIT),
        name=name,
    )(x, g, gf, wg, wu, wd)


def _inproj_kernel(h_ref, g_ref, w_ref, wf_ref, bf_ref,
                   q_ref, k_ref, v_ref, kb_ref, vb_ref, p_ref, lf_ref, qn_ref, kn_ref, *, scale, width):
    hn = _rms(h_ref[...], g_ref[...]).astype(BF16)
    n_heads = width // HEAD_DIM

    def proj(c):
        return jnp.dot(hn, w_ref[:, c * width:(c + 1) * width], preferred_element_type=F32)

    def store_heads(o_ref, u):
        for hd in range(n_heads):
            o_ref[pl.ds(hd, u.shape[0], stride=n_heads), :] = u[:, hd * HEAD_DIM:(hd + 1) * HEAD_DIM]

    def max_sq_norms(xb):
        sub = lax.broadcasted_iota(jnp.int32, (n_heads, LANES), 0)
        out = jnp.zeros((n_heads, LANES), F32)
        for hd in range(n_heads):
            x = xb[:, hd * HEAD_DIM:(hd + 1) * HEAD_DIM].astype(F32)
            top = jnp.max(jnp.sum(x * x, axis=1, keepdims=True), axis=0, keepdims=True)
            out = jnp.where(sub == hd, top, out)
        return out

    lf_ref[...] = jax.nn.log_sigmoid(jnp.dot(hn, wf_ref[...], preferred_element_type=F32) + bf_ref[...])
    qb = (proj(0) * scale).astype(BF16)
    q_ref[...] = qb
    qn_ref[...] = max_sq_norms(qb)
    u = proj(1)
    store_heads(k_ref, u)
    kb = u.astype(BF16)
    kb_ref[...] = kb
    kn_ref[...] = max_sq_norms(kb)
    u = proj(2)
    store_heads(v_ref, u)
    vb_ref[...] = u.astype(BF16)
    p_ref[...] = proj(3)


def _inproj(h, g, w_qkvp, wf, bf, *, name):
    n, d = h.shape
    w = w_qkvp.shape[1] // 4
    tm = min(ATTN_TILE, n)
    assert n % tm == 0 and w % HEAD_DIM == 0
    n_heads = w // HEAD_DIM
    assert n_heads == 8
    row = lambda i: (i, 0)
    const = lambda i: (0, 0)
    resident = pl.Buffered(1)
    return pl.pallas_call(
        functools.partial(_inproj_kernel, scale=HEAD_DIM ** -0.5 * LOG2E, width=w),
        grid=(n // tm,),
        in_specs=[
            pl.BlockSpec((tm, d), row),
            pl.BlockSpec((1, d), const),
            pl.BlockSpec((d, 4 * w), const, pipeline_mode=resident),
            pl.BlockSpec((d, LANES), const),
            pl.BlockSpec((1, LANES), const),
        ],
        out_specs=[
            pl.BlockSpec((tm, w), row),
            pl.BlockSpec((tm * n_heads, HEAD_DIM), row), pl.BlockSpec((tm * n_heads, HEAD_DIM), row),
            pl.BlockSpec((tm, w), row), pl.BlockSpec((tm, w), row), pl.BlockSpec((tm, w), row),
            pl.BlockSpec((tm, LANES), row),
            pl.BlockSpec((n_heads, LANES), row), pl.BlockSpec((n_heads, LANES), row),
        ],
        out_shape=[
            jax.ShapeDtypeStruct((n, w), BF16),
            jax.ShapeDtypeStruct((n * n_heads, HEAD_DIM), F32),
            jax.ShapeDtypeStruct((n * n_heads, HEAD_DIM), F32),
            jax.ShapeDtypeStruct((n, w), BF16),
            jax.ShapeDtypeStruct((n, w), BF16),
            jax.ShapeDtypeStruct((n, w), F32),
            jax.ShapeDtypeStruct((n, LANES), F32),
            jax.ShapeDtypeStruct((n // tm * n_heads, LANES), F32),
            jax.ShapeDtypeStruct((n // tm * n_heads, LANES), F32),
        ],
        compiler_params=_cparams(("parallel",)),
        name=name,
    )(h, g, w_qkvp, wf, bf)


def _aug_kernel(lf_ref, tri_ref, selq_ref, selk_ref, qx_ref, kx_ref, ends_ref, carry_ref, *, n_heads):
    t = pl.program_id(1)

    @pl.when(t == 0)
    def _():
        carry_ref[...] = jnp.zeros_like(carry_ref)

    lf = lf_ref[0]
    lane = lax.broadcasted_iota(jnp.int32, lf.shape, 1)

    def pick(parts, extra=None):
        out = jnp.zeros_like(lf) if extra is None else extra
        for j in reversed(range(N_SPLIT)):
            out = jnp.where((lane >= j * n_heads) & (lane < (j + 1) * n_heads), parts[j], out)
        return out

    cs = jnp.dot(tri_ref[...], pick(_split3(lf)).astype(BF16), preferred_element_type=F32)
    c = cs
    for j in range(1, N_SPLIT):
        c = c + pltpu.roll(cs, LANES - j * n_heads, axis=1)
    c = c + carry_ref[...]
    carry_ref[...] = c[c.shape[0] - 1:, :]

    crep = c
    for j in range(1, N_SPLIT):
        crep = jnp.where((lane >= j * n_heads) & (lane < (j + 1) * n_heads),
                         pltpu.roll(c, j * n_heads, axis=1), crep)
    ones = jnp.where(lane == N_SPLIT * n_heads, 1.0, 0.0).astype(F32)
    pieces = pick(_split3(crep * LOG2E), extra=ones).astype(BF16)
    qx_ref[0] = jnp.dot(pieces, selq_ref[...], preferred_element_type=F32).astype(BF16)
    kx_ref[0] = jnp.dot(pieces, selk_ref[...], preferred_element_type=F32).astype(BF16)
    sub = lax.broadcasted_iota(jnp.int32, (8, LANES), 0)
    ends_ref[0] = jnp.where(sub == 0, c[0:1, :] * LOG2E, jnp.where(sub == 1, c[c.shape[0] - 1:, :] * LOG2E, 0.0))


def _aug_constants(tm, n_heads):
    r = jnp.arange(tm)
    tri = (r[:, None] >= r[None, :]).astype(BF16)
    rows = jnp.arange(LANES)[:, None]
    cols = jnp.arange(n_heads * HEAD_DIM)[None, :]
    head, lane = cols // HEAD_DIM, cols % HEAD_DIM
    one_row = N_SPLIT * n_heads
    is_piece = lambda off: (lane >= off) & (lane < off + N_SPLIT) & (rows == (lane - off) * n_heads + head)
    is_one = lambda off: (lane >= off) & (lane < off + N_SPLIT) & (rows == one_row)
    selq = (is_piece(0) | is_one(N_SPLIT)).astype(BF16)
    selk = is_one(0).astype(BF16) - is_piece(N_SPLIT).astype(BF16)
    return tri, selq, selk


def _aug(lf, *, n_heads, tile_cap, name):
    nseq, length, _ = lf.shape
    tm = _pick_tile(length, tile_cap, 16)
    assert (N_SPLIT * n_heads) < LANES
    w = n_heads * HEAD_DIM
    tri, selq, selk = _aug_constants(tm, n_heads)
    return pl.pallas_call(
        functools.partial(_aug_kernel, n_heads=n_heads),
        grid=(nseq, length // tm),
        in_specs=[
            pl.BlockSpec((1, tm, LANES), lambda b, t: (b, t, 0)),
            pl.BlockSpec((tm, tm), lambda b, t: (0, 0)),
            pl.BlockSpec((LANES, w), lambda b, t: (0, 0)),
            pl.BlockSpec((LANES, w), lambda b, t: (0, 0)),
        ],
        out_specs=[pl.BlockSpec((1, tm, w), lambda b, t: (b, t, 0)),
                   pl.BlockSpec((1, tm, w), lambda b, t: (b, t, 0)),
                   pl.BlockSpec((1, 8, LANES), lambda b, t: (b, t, 0))],
        out_shape=[jax.ShapeDtypeStruct((nseq, length, w), BF16),
                   jax.ShapeDtypeStruct((nseq, length, w), BF16),
                   jax.ShapeDtypeStruct((nseq, length // tm * 8, LANES), F32)],
        scratch_shapes=[pltpu.VMEM((1, LANES), F32)],
        compiler_params=_cparams(("parallel", "arbitrary")),
        name=name,
    )(lf, tri, selq, selk)


def _attn_kernel(first_ref, q_ref, qx_ref, k_ref, kx_ref, v_ref, o_ref, vt_ref, m_ref, acc_ref, s0_ref, s1_ref, *,
                 t, length):
    b, hd, qi = pl.program_id(0), pl.program_id(1), pl.program_id(2)

    @pl.when(qi == 0)
    def _():
        def transpose_chunk(c, carry):
            st = pl.multiple_of(c * t, t)
            vt_ref[0:HEAD_DIM, pl.ds(st, t)] = v_ref[0, pl.ds(st, t), :].T
            return carry

        lax.fori_loop(0, length // t, transpose_chunk, 0)
        sub = lax.broadcasted_iota(jnp.int32, (VT_ROWS - HEAD_DIM, length), 0)
        vt_ref[HEAD_DIM:, :] = jnp.where(sub == 0, 1.0, 0.0).astype(BF16)

    q = jnp.concatenate([q_ref[0], qx_ref[0]], axis=1)
    n_rest = qi - first_ref[(b * pl.num_programs(1) + hd) * pl.num_programs(2) + qi]
    m_ref[...] = jnp.full_like(m_ref, -jnp.inf)
    acc_ref[...] = jnp.zeros_like(acc_ref)

    def scores(j):
        st = pl.multiple_of(jnp.maximum(j, 0) * t, t)
        k = jnp.concatenate([k_ref[0, pl.ds(st, t), :], kx_ref[0, pl.ds(st, t), :]], axis=1)
        return lax.dot_general(k, q, (((1,), (1,)), ((), ())), preferred_element_type=F32)

    def accumulate(j, s, masked):
        st = pl.multiple_of(j * t, t)
        if masked:
            key_pos = lax.broadcasted_iota(jnp.int32, (t, t), 0)
            qry_pos = lax.broadcasted_iota(jnp.int32, (t, t), 1)
            s = jnp.where(key_pos <= qry_pos, s, -jnp.inf)
        m_prev = m_ref[...]
        m_new = jnp.maximum(m_prev, jnp.max(s, axis=0, keepdims=True))
        alpha = jnp.exp2(m_prev - m_new)
        p = jnp.exp2(s - m_new).astype(BF16)
        acc_ref[...] = alpha * acc_ref[...] + jnp.dot(vt_ref[:, pl.ds(st, t)], p, preferred_element_type=F32)
        m_ref[...] = m_new

    bufs = (s0_ref, s1_ref)

    def visit(k0, n, prefetch_after):
        for u in range(n):
            k = k0[0] + u
            if u + 1 < n or prefetch_after:
                bufs[(k0[1] + u + 1) % 2][...] = scores(qi - k - 1)
            accumulate(qi - k, bufs[(k0[1] + u) % 2][...], False)

    def finish():
        acc = acc_ref[...]
        out_t = acc[0:HEAD_DIM] / acc[HEAD_DIM:HEAD_DIM + 1]
        o_ref[0] = out_t.T.astype(o_ref.dtype)

    for r in range(ATTN_FLAT):
        @pl.when(n_rest == r)
        def _(r=r):
            s_cur = scores(qi)
            for u in range(r + 1):
                s_next = scores(qi - u - 1) if u < r else None
                accumulate(qi - u, s_cur, u == 0)
                s_cur = s_next
            finish()

    @pl.when(n_rest >= ATTN_FLAT)
    def _():
        s0_ref[...] = scores(qi)
        s1_ref[...] = scores(qi - 1)
        accumulate(qi, s0_ref[...], True)

        def group(g, carry):
            visit((1 + ATTN_UNROLL * g, 1), ATTN_UNROLL, True)
            return carry

        lax.fori_loop(0, n_rest // ATTN_UNROLL, group, 0)
        rem = n_rest % ATTN_UNROLL
        for r in range(1, ATTN_UNROLL):
            @pl.when(rem == r)
            def _(r=r):
                visit((n_rest - r + 1, 1), r, False)

        finish()


def _first_needed_chunk(qn2, kn2, ends, n_heads):
    nb, nt = ends.shape[0], ends.shape[1] // 8
    qn = jnp.sqrt(qn2.reshape(nb, nt, n_heads, LANES)[..., 0])
    kn = jnp.sqrt(kn2.reshape(nb, nt, n_heads, LANES)[..., 0])
    ends = ends.reshape(nb, nt, 8, LANES)
    c_first, c_last = ends[:, :, 0, :n_heads], ends[:, :, 1, :n_heads]
    ub = qn[:, :, None, :] * kn[:, None, :, :] + c_first[:, :, None, :] - c_last[:, None, :, :]
    lb = -(qn * kn)[:, :, None, :]
    i = jnp.arange(nt)[None, :, None, None]
    j = jnp.arange(nt)[None, None, :, None]
    needed = (ub - lb >= -PRUNE_GAP) & (j < i)
    first = jnp.min(jnp.where(needed, j, i), axis=2)
    return first.transpose(0, 2, 1).reshape(-1).astype(jnp.int32)


def _attn(first, q, qx, k, kx, v, *, name):
    nb, length, w = q.shape
    n_heads = w // HEAD_DIM
    t = ATTN_TILE
    assert length % t == 0 and k.shape[1] == length
    blk = lambda b, h, i, first: (b, i, h)
    kv_spec = pl.BlockSpec((1, length, HEAD_DIM), lambda b, h, i, first: (b, 0, h))
    return pl.pallas_call(
        functools.partial(_attn_kernel, t=t, length=length),
        grid_spec=pltpu.PrefetchScalarGridSpec(
            num_scalar_prefetch=1,
            grid=(nb, n_heads, length // t),
            in_specs=[pl.BlockSpec((1, t, HEAD_DIM), blk), pl.BlockSpec((1, t, HEAD_DIM), blk),
                      kv_spec, kv_spec, kv_spec],
            out_specs=pl.BlockSpec((1, t, HEAD_DIM), blk),
            scratch_shapes=[pltpu.VMEM((VT_ROWS, length), BF16), pltpu.VMEM((1, t), F32),
                            pltpu.VMEM((VT_ROWS, t), F32), pltpu.VMEM((t, t), F32), pltpu.VMEM((t, t), F32)],
        ),
        out_shape=jax.ShapeDtypeStruct((nb, length, w), BF16),
        compiler_params=_cparams(("parallel", "parallel", "arbitrary")),
        name=name,
    )(first, q, qx, k, kx, v)


def _attn_cached_kernel(q_ref, qx_ref, kx_ref, ck_ref, cv_ref, kn_ref, vn_ref, o_ref, *, n_heads, past, tn):
    pad = LANES - tn
    causal = (lax.broadcasted_iota(jnp.int32, (tn, LANES), 1) <= lax.broadcasted_iota(jnp.int32, (tn, LANES), 0))
    nt = (((1,), (1,)), ((), ()))
    for hd in range(n_heads):
        cols = slice(hd * HEAD_DIM, (hd + 1) * HEAD_DIM)
        q = jnp.concatenate([q_ref[:, cols], qx_ref[0, :, cols]], axis=1)
        kc = jnp.concatenate([ck_ref[pl.ds(hd, past, stride=n_heads), :].astype(BF16),
                              kx_ref[0, 0:past, cols]], axis=1)
        kn = jnp.concatenate([kn_ref[:, cols], kx_ref[0, past:past + tn, cols]], axis=1)
        kn = jnp.concatenate([kn, jnp.zeros((pad, kn.shape[1]), BF16)], axis=0)
        s_c = lax.dot_general(q, kc, nt, preferred_element_type=F32)
        s_n = jnp.where(causal, lax.dot_general(q, kn, nt, preferred_element_type=F32), -jnp.inf)
        m = jnp.maximum(jnp.max(s_c, axis=1, keepdims=True), jnp.max(s_n, axis=1, keepdims=True))
        p_c = jnp.exp2(s_c - m)
        p_n = jnp.exp2(s_n - m)
        denom = jnp.sum(p_c, axis=1, keepdims=True) + jnp.sum(p_n, axis=1, keepdims=True)
        vc = cv_ref[pl.ds(hd, past, stride=n_heads), :].astype(BF16)
        vn = jnp.concatenate([vn_ref[:, cols], jnp.zeros((pad, HEAD_DIM), BF16)], axis=0)
        out = (jnp.dot(p_c.astype(BF16), vc, preferred_element_type=F32)
               + jnp.dot(p_n.astype(BF16), vn, preferred_element_type=F32))
        o_ref[:, cols] = (out / denom).astype(o_ref.dtype)


def _attn_cached(q, qx, kx, cache_k, cache_v, k_new, v_new, *, n_seq, past, name):
    n, w = q.shape
    tn = n // n_seq
    n_heads = w // HEAD_DIM
    assert tn <= LANES and tn % 16 == 0 and past % tn == 0 and kx.shape[1] >= past + tn
    ck = cache_k.reshape(n_seq * past * n_heads, HEAD_DIM)
    cv = cache_v.reshape(n_seq * past * n_heads, HEAD_DIM)
    rows = pl.BlockSpec((tn, w), lambda b: (b, 0))
    cached = pl.BlockSpec((past * n_heads, HEAD_DIM), lambda b: (b, 0))
    return pl.pallas_call(
        functools.partial(_attn_cached_kernel, n_heads=n_heads, past=past, tn=tn),
        grid=(n_seq,),
        in_specs=[
            rows,
            pl.BlockSpec((1, tn, w), lambda b: (b, past // tn, 0)),
            pl.BlockSpec((1, kx.shape[1], w), lambda b: (b, 0, 0)),
            cached, cached, rows, rows,
        ],
        out_specs=rows,
        out_shape=jax.ShapeDtypeStruct((n, w), BF16),
        compiler_params=_cparams(("parallel",)),
        name=name,
    )(q, qx, kx, ck, cv, k_new, v_new)


def _pool_kernel(p_ref, hist_ref, w_ref, sc_ref, o_ref, ext_ref, *, tp, pos0):
    t = pl.program_id(1)

    @pl.when(t == 0)
    def _():
        ext_ref[0:HIST_ROWS, :] = hist_ref[0]

    ext_ref[HIST_ROWS:, :] = p_ref[0]
    pos = pos0 + t * tp + lax.broadcasted_iota(jnp.int32, (tp, 1), 0)
    gd = w_ref.shape[1]
    for g, win in enumerate(POOL_WINDOWS):
        cols = slice(g * gd, (g + 1) * gd)
        cur = ext_ref[HIST_ROWS:, cols]
        tot = cur
        for back in range(1, win):
            tot = tot + ext_ref[HIST_ROWS - back:HIST_ROWS - back + tp, cols]
        cnt = jnp.minimum(pos + 1, win).astype(F32)
        d = (tot / cnt - cur).astype(BF16)
        y = jnp.dot(d, w_ref[g], preferred_element_type=F32) * sc_ref[:, cols]
        o_ref[0, :, cols] = y.astype(o_ref.dtype)
    ext_ref[0:HIST_ROWS, :] = ext_ref[tp:tp + HIST_ROWS, :]


def _pool(p, hist, w_pool, scale, *, pos0, name):
    nseq, length, w = p.shape
    tp = min(512, length)
    assert length % tp == 0 and tp >= HIST_ROWS
    return pl.pallas_call(
        functools.partial(_pool_kernel, tp=tp, pos0=pos0),
        grid=(nseq, length // tp),
        in_specs=[
            pl.BlockSpec((1, tp, w), lambda b, t: (b, t, 0)),
            pl.BlockSpec((1, HIST_ROWS, w), lambda b, t: (b, 0, 0)),
            pl.BlockSpec(w_pool.shape, lambda b, t: (0, 0, 0)),
            pl.BlockSpec((1, w), lambda b, t: (0, 0)),
        ],
        out_specs=pl.BlockSpec((1, tp, w), lambda b, t: (b, t, 0)),
        out_shape=jax.ShapeDtypeStruct((nseq, length, w), BF16),
        scratch_shapes=[pltpu.VMEM((HIST_ROWS + tp, w), F32)],
        compiler_params=_cparams(("parallel", "arbitrary")),
        name=name,
    )(p, hist, w_pool, scale)


def _oproj_kernel(h_ref, fox_ref, pool_ref, wo_ref, o_ref):
    mix = jnp.concatenate([fox_ref[...], pool_ref[...]], axis=1)
    o_ref[...] = h_ref[...] + jnp.dot(mix, wo_ref[...], preferred_element_type=F32)


def _oproj(h, fox, pool, wo, *, name):
    n, d = h.shape
    tm = min(512, n)
    assert n % tm == 0
    return pl.pallas_call(
        _oproj_kernel,
        grid=(n // tm,),
        in_specs=[
            pl.BlockSpec((tm, d), lambda i: (i, 0)),
            pl.BlockSpec((tm, fox.shape[1]), lambda i: (i, 0)),
            pl.BlockSpec((tm, pool.shape[1]), lambda i: (i, 0)),
            pl.BlockSpec(wo.shape, lambda i: (0, 0)),
        ],
        out_specs=pl.BlockSpec((tm, d), lambda i: (i, 0)),
        out_shape=jax.ShapeDtypeStruct((n, d), F32),
        compiler_params=_cparams(("parallel",)),
        name=name,
    )(h, fox, pool, wo)


def _round_up(x, m):
    return (x + m - 1) // m * m


def kernel(x_prompt, x_sample, cache_k, cache_v, cache_logf, state_pool, g_ffn1, w1_gate, w1_up, w1_down, g_mix, w_in, b_f, w_pool, pool_scale, w_o, g_ffn2, w2_gate, w2_up, w2_down, g_final):
    B, S, D = x_prompt.shape
    Bs, Tn, _ = x_sample.shape
    depth, _, P, H, Dh = cache_k.shape
    assert depth == 1 and Dh == HEAD_DIM
    fw = H * Dh
    pw = w_pool.shape[1] * w_pool.shape[2]
    assert w_in.shape[2] == 3 * fw + H + pw and state_pool.shape[2] == POOL_HIST

    row = lambda a: a.reshape(1, -1).astype(F32)
    w1g, w1u, w1d = w1_gate[0].astype(BF16), w1_up[0].astype(BF16), w1_down[0].astype(BF16)
    w2g, w2u, w2d = w2_gate[0].astype(BF16), w2_up[0].astype(BF16), w2_down[0].astype(BF16)
    wi = w_in[0]
    w_qkvp = jnp.concatenate([wi[:, :3 * fw], wi[:, 3 * fw + H:]], axis=1).astype(BF16)
    wf_cols = jnp.tile(wi[:, 3 * fw:3 * fw + H], (1, N_SPLIT))
    wf = jnp.pad(wf_cols, ((0, 0), (0, LANES - N_SPLIT * H))).astype(BF16)
    bf = jnp.pad(jnp.tile(b_f[0].astype(F32), N_SPLIT), (0, LANES - N_SPLIT * H)).reshape(1, LANES)
    wo = w_o[0].astype(BF16)
    wp = w_pool[0].astype(BF16)
    g1, gm, g2, gfin, psc = row(g_ffn1[0]), row(g_mix[0]), row(g_ffn2[0]), row(g_final), row(pool_scale[0])

    def pre(x2d, tag):
        h = _ffn(x2d, g1, w1g, w1u, w1d, gfin, final_norm=False, name=f"ffn1_{tag}")
        return (h,) + tuple(_inproj(h, gm, w_qkvp, wf, bf, name=f"inproj_{tag}"))

    def post(h, fox, pool, tag):
        h2 = _oproj(h, fox, pool, wo, name=f"oproj_{tag}")
        return _ffn(h2, g2, w2g, w2u, w2d, gfin, final_norm=True, name=f"ffn2_{tag}")

    h, q, k, v, kb, vb, p, lf, qn2, kn2 = pre(x_prompt.reshape(B * S, D), "prompt")
    qx, kx, ends = _aug(lf.reshape(B, S, LANES), n_heads=H, tile_cap=ATTN_TILE, name="aug_prompt")
    first = _first_needed_chunk(qn2, kn2, ends, H)
    fox = _attn(first, q.reshape(B, S, fw), qx, kb.reshape(B, S, fw), kx, vb.reshape(B, S, fw), name="attn_prompt")
    p3 = p.reshape(B, S, pw)
    pool = _pool(p3, jnp.zeros((B, HIST_ROWS, pw), F32), wp, psc, pos0=0, name="pool_prompt")
    y_prompt = post(h, fox.reshape(B * S, fw), pool.reshape(B * S, pw), "prompt").reshape(B, S, D)
    k_prompt = k.reshape(1, B, S, H, Dh)
    v_prompt = v.reshape(1, B, S, H, Dh)
    logf_prompt = lf[:, :H].reshape(1, B, S, H)
    pool_prompt = p3[:, S - POOL_HIST:][None]

    h, q, k, v, kb, vb, p, lf, _, _ = pre(x_sample.reshape(Bs * Tn, D), "sample")
    L = _round_up(P + Tn, LANES)
    tail = L - P - Tn
    lf_cache = jnp.pad(jnp.tile(cache_logf[0].astype(F32), (1, 1, N_SPLIT)),
                       ((0, 0), (0, 0), (0, LANES - N_SPLIT * H)))
    lf_all = jnp.concatenate([lf_cache, lf.reshape(Bs, Tn, LANES), jnp.zeros((Bs, tail, LANES), F32)], axis=1)
    qx, kx, _ = _aug(lf_all, n_heads=H, tile_cap=L, name="aug_sample")
    fox = _attn_cached(q, qx, kx, cache_k[0], cache_v[0], kb, vb, n_seq=Bs, past=P, name="attn_sample")
    p3 = p.reshape(Bs, Tn, pw)
    hist = jnp.concatenate([jnp.zeros((Bs, HIST_ROWS - POOL_HIST, pw), F32), state_pool[0].astype(F32)], axis=1)
    pool = _pool(p3, hist, wp, psc, pos0=P, name="pool_sample")
    y_sample = post(h, fox, pool.reshape(Bs * Tn, pw), "sample").reshape(Bs, Tn, D)
    k_sample = k.reshape(1, Bs, Tn, H, Dh)
    v_sample = v.reshape(1, Bs, Tn, H, Dh)
    logf_sample = lf[:, :H].reshape(1, Bs, Tn, H)
    ext_tail = jnp.concatenate([state_pool[0].astype(F32), p3], axis=1)[:, Tn:]
    pool_sample = ext_tail[None]

    return (y_prompt, y_sample, k_prompt, v_prompt, logf_prompt, pool_prompt,
            k_sample, v_sample, logf_sample, pool_sample)
```

```python
import functools

import jax
import jax.numpy as jnp
from jax import lax
from jax.experimental import pallas as pl
from jax.experimental.pallas import tpu as pltpu

F32 = jnp.float32
BF16 = jnp.bfloat16

EPS = 1e-6
HEAD_DIM = 128
LANES = 128
POOL_WINDOWS = (2, 4, 8, 16)
POOL_HIST = max(POOL_WINDOWS) - 1
HIST_ROWS = 16
N_SPLIT = 3
VT_ROWS = HEAD_DIM + 16
LOG2E = 1.4426950408889634
VMEM_LIMIT = 56 * 1024 * 1024
FFN_VMEM_LIMIT = 62 * 1024 * 1024
ATTN_UNROLL = 4
ATTN_FLAT = 6
ATTN_TILE = 512
PRUNE_GAP = 170.0
FFN_ROWS = 1024
FFN_COLS = 256
FFN_SLOTS = 4
FFN_NORM_ROWS = 256


def _cparams(sem, vmem_limit=VMEM_LIMIT):
    return pltpu.CompilerParams(dimension_semantics=sem, vmem_limit_bytes=vmem_limit)


def _pick_tile(length, cap, mult):
    best = None
    for t in range(mult, min(cap, length) + 1, mult):
        if length % t == 0:
            best = t
    assert best is not None, (length, cap, mult)
    return best


def _rms(x, g):
    return x * lax.rsqrt(jnp.mean(x * x, axis=-1, keepdims=True) + EPS) * g


def _split3(x):
    hi = x.astype(BF16).astype(F32)
    r1 = x - hi
    mid = r1.astype(BF16).astype(F32)
    lo = (r1 - mid).astype(BF16).astype(F32)
    return hi, mid, lo


def _ffn_kernel(x_ref, g_ref, gf_ref, wg_hbm, wu_hbm, wd_hbm, o_ref, xn_ref, wg_buf, wu_buf, wd_buf, sem, *,
                final_norm, tf, nf):
    def copies(f):
        slot = f % FFN_SLOTS
        cols = pl.ds(pl.multiple_of(f * tf, tf), tf)
        return (pltpu.make_async_copy(wg_hbm.at[:, cols], wg_buf.at[slot], sem.at[0, slot]),
                pltpu.make_async_copy(wu_hbm.at[:, cols], wu_buf.at[slot], sem.at[1, slot]),
                pltpu.make_async_copy(wd_hbm.at[cols, :], wd_buf.at[slot], sem.at[2, slot]))

    def start(f):
        for c in copies(f):
            c.start()

    def wait(f):
        for c in copies(f):
            c.wait()

    def chunk(f, xn):
        slot = f % FFN_SLOTS
        gate = jnp.dot(xn, wg_buf[slot], preferred_element_type=F32)
        up = jnp.dot(xn, wu_buf[slot], preferred_element_type=F32)
        a = (gate * jax.nn.sigmoid(gate) * up).astype(BF16)
        return jnp.dot(a, wd_buf[slot], preferred_element_type=F32)

    tile = pl.program_id(0)

    @pl.when(tile == 0)
    def _():
        start(0)

    start(1)
    start(2)
    wait(0)
    xn = _rms(x_ref[...], g_ref[...]).astype(BF16)
    xn_ref[...] = xn
    o_ref[...] = chunk(0, xn)

    def pair(i, carry):
        f = 1 + 2 * i
        wait(f)
        wait(f + 1)
        start(f + 2)

        @pl.when(f + 3 < nf)
        def _():
            start(f + 3)

        xn = xn_ref[...]
        o_ref[...] += chunk(f, xn)
        o_ref[...] += chunk(f + 1, xn)
        return carry

    lax.fori_loop(0, (nf - 2) // 2, pair, 0)
    wait(nf - 1)

    @pl.when(tile + 1 < pl.num_programs(0))
    def _():
        start(0)

    o_ref[...] = x_ref[...] + 0.5 * (o_ref[...] + chunk(nf - 1, xn_ref[...]))
    if final_norm:
        rows = min(FFN_NORM_ROWS, o_ref.shape[0])
        for r in range(0, o_ref.shape[0], rows):
            o_ref[r:r + rows, :] = _rms(o_ref[r:r + rows, :], gf_ref[...])


def _ffn(x, g, wg, wu, wd, gf, *, final_norm, name):
    n, d = x.shape
    dff = wg.shape[1]
    tm = min(FFN_ROWS, n)
    tf = FFN_COLS
    nf = dff // tf
    assert n % tm == 0 and dff % tf == 0 and nf >= 4 and nf % 2 == 0
    return pl.pallas_call(
        functools.partial(_ffn_kernel, final_norm=final_norm, tf=tf, nf=nf),
        grid=(n // tm,),
        in_specs=[
            pl.BlockSpec((tm, d), lambda i: (i, 0)),
            pl.BlockSpec((1, d), lambda i: (0, 0)),
            pl.BlockSpec((1, d), lambda i: (0, 0)),
            pl.BlockSpec(memory_space=pl.ANY),
            pl.BlockSpec(memory_space=pl.ANY),
            pl.BlockSpec(memory_space=pl.ANY),
        ],
        out_specs=pl.BlockSpec((tm, d), lambda i: (i, 0)),
        out_shape=jax.ShapeDtypeStruct((n, d), F32),
        scratch_shapes=[pltpu.VMEM((tm, d), BF16),
                        pltpu.VMEM((FFN_SLOTS, d, tf), BF16), pltpu.VMEM((FFN_SLOTS, d, tf), BF16),
                        pltpu.VMEM((FFN_SLOTS, tf, d), BF16), pltpu.SemaphoreType.DMA((3, FFN_SLOTS))],
        compiler_params=_cparams(("arbitrary",), FFN_VMEM_LIMIT),
        name=name,
    )(x, g, gf, wg, wu, wd)


def _inproj_kernel(h_ref, g_ref, w_ref, wf_ref, bf_ref,
                   q_ref, k_ref, v_ref, ka_ref, va_ref, p_ref, lf_ref, qn_ref, kn_ref, *, scale, width, head_major):
    hn = _rms(h_ref[...], g_ref[...]).astype(BF16)
    n_heads = width // HEAD_DIM

    def proj(c):
        return jnp.dot(hn, w_ref[:, c * width:(c + 1) * width], preferred_element_type=F32)

    def store_heads(o_ref, u):
        for hd in range(n_heads):
            o_ref[pl.ds(hd, u.shape[0], stride=n_heads), :] = u[:, hd * HEAD_DIM:(hd + 1) * HEAD_DIM]

    def max_sq_norms(xb):
        sub = lax.broadcasted_iota(jnp.int32, (n_heads, LANES), 0)
        out = jnp.zeros((n_heads, LANES), F32)
        for hd in range(n_heads):
            x = xb[:, hd * HEAD_DIM:(hd + 1) * HEAD_DIM].astype(F32)
            top = jnp.max(jnp.sum(x * x, axis=1, keepdims=True), axis=0, keepdims=True)
            out = jnp.where(sub == hd, top, out)
        return out

    lf_ref[...] = jax.nn.log_sigmoid(jnp.dot(hn, wf_ref[...], preferred_element_type=F32) + bf_ref[...])
    qb = (proj(0) * scale).astype(BF16)
    q_ref[...] = qb
    qn_ref[...] = max_sq_norms(qb)
    u = proj(1)
    store_heads(k_ref, u)
    kb = u.astype(BF16)
    kn_ref[...] = max_sq_norms(kb)
    u = proj(2)
    store_heads(v_ref, u)
    vb = u.astype(BF16)
    if head_major:
        sub = lax.broadcasted_iota(jnp.int32, (VT_ROWS - HEAD_DIM, vb.shape[0]), 0)
        tail = jnp.where(sub == 0, 1.0, 0.0).astype(BF16)
        for hd in range(n_heads):
            cols = slice(hd * HEAD_DIM, (hd + 1) * HEAD_DIM)
            ka_ref[0, hd] = kb[:, cols]
            va_ref[0, hd, 0:HEAD_DIM, :] = vb[:, cols].T
            va_ref[0, hd, HEAD_DIM:, :] = tail
    else:
        ka_ref[...] = kb
        va_ref[...] = vb
    p_ref[...] = proj(3)


def _inproj(h, g, w_qkvp, wf, bf, *, seq_len, name):
    n, d = h.shape
    w = w_qkvp.shape[1] // 4
    tm = min(ATTN_TILE, n)
    assert n % tm == 0 and w % HEAD_DIM == 0
    n_heads = w // HEAD_DIM
    assert n_heads == 8
    row = lambda i: (i, 0)
    const = lambda i: (0, 0)
    resident = pl.Buffered(1)
    if seq_len is None:
        ka_spec = va_spec = pl.BlockSpec((tm, w), row)
        ka_shape = va_shape = jax.ShapeDtypeStruct((n, w), BF16)
    else:
        assert seq_len % tm == 0 and n % seq_len == 0
        tps = seq_len // tm
        ka_spec = pl.BlockSpec((1, n_heads, tm, HEAD_DIM), lambda i: (i // tps, 0, i % tps, 0))
        va_spec = pl.BlockSpec((1, n_heads, VT_ROWS, tm), lambda i: (i // tps, 0, 0, i % tps))
        ka_shape = jax.ShapeDtypeStruct((n // seq_len, n_heads, seq_len, HEAD_DIM), BF16)
        va_shape = jax.ShapeDtypeStruct((n // seq_len, n_heads, VT_ROWS, seq_len), BF16)
    return pl.pallas_call(
        functools.partial(_inproj_kernel, scale=HEAD_DIM ** -0.5 * LOG2E, width=w, head_major=seq_len is not None),
        grid=(n // tm,),
        in_specs=[
            pl.BlockSpec((tm, d), row),
            pl.BlockSpec((1, d), const),
            pl.BlockSpec((d, 4 * w), const, pipeline_mode=resident),
            pl.BlockSpec((d, LANES), const),
            pl.BlockSpec((1, LANES), const),
        ],
        out_specs=[
            pl.BlockSpec((tm, w), row),
            pl.BlockSpec((tm * n_heads, HEAD_DIM), row), pl.BlockSpec((tm * n_heads, HEAD_DIM), row),
            ka_spec, va_spec, pl.BlockSpec((tm, w), row),
            pl.BlockSpec((tm, LANES), row),
            pl.BlockSpec((n_heads, LANES), row), pl.BlockSpec((n_heads, LANES), row),
        ],
        out_shape=[
            jax.ShapeDtypeStruct((n, w), BF16),
            jax.ShapeDtypeStruct((n * n_heads, HEAD_DIM), F32),
            jax.ShapeDtypeStruct((n * n_heads, HEAD_DIM), F32),
            ka_shape,
            va_shape,
            jax.ShapeDtypeStruct((n, w), F32),
            jax.ShapeDtypeStruct((n, LANES), F32),
            jax.ShapeDtypeStruct((n // tm * n_heads, LANES), F32),
            jax.ShapeDtypeStruct((n // tm * n_heads, LANES), F32),
        ],
        compiler_params=_cparams(("parallel",)),
        name=name,
    )(h, g, w_qkvp, wf, bf)


def _aug_kernel(lf_ref, tri_ref, selq_ref, selk_ref, qx_ref, kx_ref, ends_ref, carry_ref, *, n_heads):
    t = pl.program_id(1)

    @pl.when(t == 0)
    def _():
        carry_ref[...] = jnp.zeros_like(carry_ref)

    lf = lf_ref[0]
    lane = lax.broadcasted_iota(jnp.int32, lf.shape, 1)

    def pick(parts, extra=None):
        out = jnp.zeros_like(lf) if extra is None else extra
        for j in reversed(range(N_SPLIT)):
            out = jnp.where((lane >= j * n_heads) & (lane < (j + 1) * n_heads), parts[j], out)
        return out

    cs = jnp.dot(tri_ref[...], pick(_split3(lf)).astype(BF16), preferred_element_type=F32)
    c = cs
    for j in range(1, N_SPLIT):
        c = c + pltpu.roll(cs, LANES - j * n_heads, axis=1)
    c = c + carry_ref[...]
    carry_ref[...] = c[c.shape[0] - 1:, :]

    crep = c
    for j in range(1, N_SPLIT):
        crep = jnp.where((lane >= j * n_heads) & (lane < (j + 1) * n_heads),
                         pltpu.roll(c, j * n_heads, axis=1), crep)
    ones = jnp.where(lane == N_SPLIT * n_heads, 1.0, 0.0).astype(F32)
    pieces = pick(_split3(crep * LOG2E), extra=ones).astype(BF16)
    qx_ref[0] = jnp.dot(pieces, selq_ref[...], preferred_element_type=F32).astype(BF16)
    kx = jnp.dot(pieces, selk_ref[...], preferred_element_type=F32).astype(BF16)
    for hd in range(n_heads):
        kx_ref[0, hd] = kx[:, hd * HEAD_DIM:(hd + 1) * HEAD_DIM]
    sub = lax.broadcasted_iota(jnp.int32, (8, LANES), 0)
    ends_ref[0] = jnp.where(sub == 0, c[0:1, :] * LOG2E, jnp.where(sub == 1, c[c.shape[0] - 1:, :] * LOG2E, 0.0))


def _aug_constants(tm, n_heads):
    r = jnp.arange(tm)
    tri = (r[:, None] >= r[None, :]).astype(BF16)
    rows = jnp.arange(LANES)[:, None]
    cols = jnp.arange(n_heads * HEAD_DIM)[None, :]
    head, lane = cols // HEAD_DIM, cols % HEAD_DIM
    one_row = N_SPLIT * n_heads
    is_piece = lambda off: (lane >= off) & (lane < off + N_SPLIT) & (rows == (lane - off) * n_heads + head)
    is_one = lambda off: (lane >= off) & (lane < off + N_SPLIT) & (rows == one_row)
    selq = (is_piece(0) | is_one(N_SPLIT)).astype(BF16)
    selk = is_one(0).astype(BF16) - is_piece(N_SPLIT).astype(BF16)
    return tri, selq, selk


def _aug(lf, *, n_heads, tile_cap, name):
    nseq, length, _ = lf.shape
    tm = _pick_tile(length, tile_cap, 16)
    assert (N_SPLIT * n_heads) < LANES
    w = n_heads * HEAD_DIM
    tri, selq, selk = _aug_constants(tm, n_heads)
    return pl.pallas_call(
        functools.partial(_aug_kernel, n_heads=n_heads),
        grid=(nseq, length // tm),
        in_specs=[
            pl.BlockSpec((1, tm, LANES), lambda b, t: (b, t, 0)),
            pl.BlockSpec((tm, tm), lambda b, t: (0, 0)),
            pl.BlockSpec((LANES, w), lambda b, t: (0, 0)),
            pl.BlockSpec((LANES, w), lambda b, t: (0, 0)),
        ],
        out_specs=[pl.BlockSpec((1, tm, w), lambda b, t: (b, t, 0)),
                   pl.BlockSpec((1, n_heads, tm, HEAD_DIM), lambda b, t: (b, 0, t, 0)),
                   pl.BlockSpec((1, 8, LANES), lambda b, t: (b, t, 0))],
        out_shape=[jax.ShapeDtypeStruct((nseq, length, w), BF16),
                   jax.ShapeDtypeStruct((nseq, n_heads, length, HEAD_DIM), BF16),
                   jax.ShapeDtypeStruct((nseq, length // tm * 8, LANES), F32)],
        scratch_shapes=[pltpu.VMEM((1, LANES), F32)],
        compiler_params=_cparams(("parallel", "arbitrary")),
        name=name,
    )(lf, tri, selq, selk)


def _attn_kernel(first_ref, q_ref, qx_ref, k_ref, kx_ref, vt_ref, o_ref, m_ref, acc_ref, s0_ref, s1_ref, *, t):
    b, hd, qi = pl.program_id(0), pl.program_id(1), pl.program_id(2)
    q = jnp.concatenate([q_ref[0], qx_ref[0]], axis=1)
    n_rest = qi - first_ref[(b * pl.num_programs(1) + hd) * pl.num_programs(2) + qi]
    m_ref[...] = jnp.full_like(m_ref, -jnp.inf)
    acc_ref[...] = jnp.zeros_like(acc_ref)

    def scores(j):
        st = pl.multiple_of(jnp.maximum(j, 0) * t, t)
        k = jnp.concatenate([k_ref[0, 0, pl.ds(st, t), :], kx_ref[0, 0, pl.ds(st, t), :]], axis=1)
        return lax.dot_general(k, q, (((1,), (1,)), ((), ())), preferred_element_type=F32)

    def accumulate(j, s, masked):
        st = pl.multiple_of(j * t, t)
        if masked:
            key_pos = lax.broadcasted_iota(jnp.int32, (t, t), 0)
            qry_pos = lax.broadcasted_iota(jnp.int32, (t, t), 1)
            s = jnp.where(key_pos <= qry_pos, s, -jnp.inf)
        m_prev = m_ref[...]
        m_new = jnp.maximum(m_prev, jnp.max(s, axis=0, keepdims=True))
        alpha = jnp.exp2(m_prev - m_new)
        p = jnp.exp2(s - m_new).astype(BF16)
        acc_ref[...] = alpha * acc_ref[...] + jnp.dot(vt_ref[0, 0, :, pl.ds(st, t)], p, preferred_element_type=F32)
        m_ref[...] = m_new

    bufs = (s0_ref, s1_ref)

    def visit(k0, n, prefetch_after):
        for u in range(n):
            k = k0[0] + u
            if u + 1 < n or prefetch_after:
                bufs[(k0[1] + u + 1) % 2][...] = scores(qi - k - 1)
            accumulate(qi - k, bufs[(k0[1] + u) % 2][...], False)

    def finish():
        acc = acc_ref[...]
        out_t = acc[0:HEAD_DIM] / acc[HEAD_DIM:HEAD_DIM + 1]
        o_ref[0] = out_t.T.astype(o_ref.dtype)

    for r in range(ATTN_FLAT):
        @pl.when(n_rest == r)
        def _(r=r):
            s_cur = scores(qi)
            for u in range(r + 1):
                s_next = scores(qi - u - 1) if u < r else None
                accumulate(qi - u, s_cur, u == 0)
                s_cur = s_next
            finish()

    @pl.when(n_rest >= ATTN_FLAT)
    def _():
        s0_ref[...] = scores(qi)
        s1_ref[...] = scores(qi - 1)
        accumulate(qi, s0_ref[...], True)

        def group(g, carry):
            visit((1 + ATTN_UNROLL * g, 1), ATTN_UNROLL, True)
            return carry

        lax.fori_loop(0, n_rest // ATTN_UNROLL, group, 0)
        rem = n_rest % ATTN_UNROLL
        for r in range(1, ATTN_UNROLL):
            @pl.when(rem == r)
            def _(r=r):
                visit((n_rest - r + 1, 1), r, False)

        finish()


def _first_needed_chunk(qn2, kn2, ends, n_heads):
    nb, nt = ends.shape[0], ends.shape[1] // 8
    qn = jnp.sqrt(qn2.reshape(nb, nt, n_heads, LANES)[..., 0])
    kn = jnp.sqrt(kn2.reshape(nb, nt, n_heads, LANES)[..., 0])
    ends = ends.reshape(nb, nt, 8, LANES)
    c_first, c_last = ends[:, :, 0, :n_heads], ends[:, :, 1, :n_heads]
    ub = qn[:, :, None, :] * kn[:, None, :, :] + c_first[:, :, None, :] - c_last[:, None, :, :]
    lb = -(qn * kn)[:, :, None, :]
    i = jnp.arange(nt)[None, :, None, None]
    j = jnp.arange(nt)[None, None, :, None]
    needed = (ub - lb >= -PRUNE_GAP) & (j < i)
    first = jnp.min(jnp.where(needed, j, i), axis=2)
    return first.transpose(0, 2, 1).reshape(-1).astype(jnp.int32)


def _attn(first, q, qx, k, kx, vt, *, name):
    nb, length, w = q.shape
    n_heads = w // HEAD_DIM
    t = ATTN_TILE
    assert length % t == 0 and k.shape == (nb, n_heads, length, HEAD_DIM) and vt.shape == (nb, n_heads, VT_ROWS, length)
    blk = lambda b, h, i, first: (b, i, h)
    head = lambda b, h, i, first: (b, h, 0, 0)
    return pl.pallas_call(
        functools.partial(_attn_kernel, t=t),
        grid_spec=pltpu.PrefetchScalarGridSpec(
            num_scalar_prefetch=1,
            grid=(nb, n_heads, length // t),
            in_specs=[pl.BlockSpec((1, t, HEAD_DIM), blk), pl.BlockSpec((1, t, HEAD_DIM), blk),
                      pl.BlockSpec((1, 1, length, HEAD_DIM), head), pl.BlockSpec((1, 1, length, HEAD_DIM), head),
                      pl.BlockSpec((1, 1, VT_ROWS, length), head)],
            out_specs=pl.BlockSpec((1, t, HEAD_DIM), blk),
            scratch_shapes=[pltpu.VMEM((1, t), F32), pltpu.VMEM((VT_ROWS, t), F32),
                            pltpu.VMEM((t, t), F32), pltpu.VMEM((t, t), F32)],
        ),
        out_shape=jax.ShapeDtypeStruct((nb, length, w), BF16),
        compiler_params=_cparams(("parallel", "parallel", "arbitrary")),
        name=name,
    )(first, q, qx, k, kx, vt)


def _attn_cached_kernel(q_ref, qx_ref, kx_ref, ck_ref, cv_ref, kn_ref, vn_ref, o_ref, *, n_heads, past, tn):
    pad = LANES - tn
    causal = (lax.broadcasted_iota(jnp.int32, (tn, LANES), 1) <= lax.broadcasted_iota(jnp.int32, (tn, LANES), 0))
    nt = (((1,), (1,)), ((), ()))
    for hd in range(n_heads):
        cols = slice(hd * HEAD_DIM, (hd + 1) * HEAD_DIM)
        q = jnp.concatenate([q_ref[:, cols], qx_ref[0, :, cols]], axis=1)
        kc = jnp.concatenate([ck_ref[pl.ds(hd, past, stride=n_heads), :].astype(BF16),
                              kx_ref[0, hd, 0:past, :]], axis=1)
        kn = jnp.concatenate([kn_ref[:, cols], kx_ref[0, hd, past:past + tn, :]], axis=1)
        kn = jnp.concatenate([kn, jnp.zeros((pad, kn.shape[1]), BF16)], axis=0)
        s_c = lax.dot_general(q, kc, nt, preferred_element_type=F32)
        s_n = jnp.where(causal, lax.dot_general(q, kn, nt, preferred_element_type=F32), -jnp.inf)
        m = jnp.maximum(jnp.max(s_c, axis=1, keepdims=True), jnp.max(s_n, axis=1, keepdims=True))
        p_c = jnp.exp2(s_c - m)
        p_n = jnp.exp2(s_n - m)
        denom = jnp.sum(p_c, axis=1, keepdims=True) + jnp.sum(p_n, axis=1, keepdims=True)
        vc = cv_ref[pl.ds(hd, past, stride=n_heads), :].astype(BF16)
        vn = jnp.concatenate([vn_ref[:, cols], jnp.zeros((pad, HEAD_DIM), BF16)], axis=0)
        out = (jnp.dot(p_c.astype(BF16), vc, preferred_element_type=F32)
               + jnp.dot(p_n.astype(BF16), vn, preferred_element_type=F32))
        o_ref[:, cols] = (out / denom).astype(o_ref.dtype)


def _attn_cached(q, qx, kx, cache_k, cache_v, k_new, v_new, *, n_seq, past, name):
    n, w = q.shape
    tn = n // n_seq
    n_heads = w // HEAD_DIM
    assert tn <= LANES and tn % 16 == 0 and past % tn == 0 and kx.shape[2] >= past + tn
    ck = cache_k.reshape(n_seq * past * n_heads, HEAD_DIM)
    cv = cache_v.reshape(n_seq * past * n_heads, HEAD_DIM)
    rows = pl.BlockSpec((tn, w), lambda b: (b, 0))
    cached = pl.BlockSpec((past * n_heads, HEAD_DIM), lambda b: (b, 0))
    return pl.pallas_call(
        functools.partial(_attn_cached_kernel, n_heads=n_heads, past=past, tn=tn),
        grid=(n_seq,),
        in_specs=[
            rows,
            pl.BlockSpec((1, tn, w), lambda b: (b, past // tn, 0)),
            pl.BlockSpec((1,) + kx.shape[1:], lambda b: (b, 0, 0, 0)),
            cached, cached, rows, rows,
        ],
        out_specs=rows,
        out_shape=jax.ShapeDtypeStruct((n, w), BF16),
        compiler_params=_cparams(("parallel",)),
        name=name,
    )(q, qx, kx, ck, cv, k_new, v_new)


def _pool_kernel(p_ref, hist_ref, w_ref, sc_ref, o_ref, ext_ref, *, tp, pos0):
    t = pl.program_id(1)

    @pl.when(t == 0)
    def _():
        ext_ref[0:HIST_ROWS, :] = hist_ref[0]

    ext_ref[HIST_ROWS:, :] = p_ref[0]
    pos = pos0 + t * tp + lax.broadcasted_iota(jnp.int32, (tp, 1), 0)
    gd = w_ref.shape[1]
    for g, win in enumerate(POOL_WINDOWS):
        cols = slice(g * gd, (g + 1) * gd)
        cur = ext_ref[HIST_ROWS:, cols]
        tot = cur
        for back in range(1, win):
            tot = tot + ext_ref[HIST_ROWS - back:HIST_ROWS - back + tp, cols]
        cnt = jnp.minimum(pos + 1, win).astype(F32)
        d = (tot / cnt - cur).astype(BF16)
        y = jnp.dot(d, w_ref[g], preferred_element_type=F32) * sc_ref[:, cols]
        o_ref[0, :, cols] = y.astype(o_ref.dtype)
    ext_ref[0:HIST_ROWS, :] = ext_ref[tp:tp + HIST_ROWS, :]


def _pool(p, hist, w_pool, scale, *, pos0, name):
    nseq, length, w = p.shape
    tp = min(512, length)
    assert length % tp == 0 and tp >= HIST_ROWS
    return pl.pallas_call(
        functools.partial(_pool_kernel, tp=tp, pos0=pos0),
        grid=(nseq, length // tp),
        in_specs=[
            pl.BlockSpec((1, tp, w), lambda b, t: (b, t, 0)),
            pl.BlockSpec((1, HIST_ROWS, w), lambda b, t: (b, 0, 0)),
            pl.BlockSpec(w_pool.shape, lambda b, t: (0, 0, 0)),
            pl.BlockSpec((1, w), lambda b, t: (0, 0)),
        ],
        out_specs=pl.BlockSpec((1, tp, w), lambda b, t: (b, t, 0)),
        out_shape=jax.ShapeDtypeStruct((nseq, length, w), BF16),
        scratch_shapes=[pltpu.VMEM((HIST_ROWS + tp, w), F32)],
        compiler_params=_cparams(("parallel", "arbitrary")),
        name=name,
    )(p, hist, w_pool, scale)


def _oproj_kernel(h_ref, fox_ref, pool_ref, wo_ref, o_ref):
    mix = jnp.concatenate([fox_ref[...], pool_ref[...]], axis=1)
    o_ref[...] = h_ref[...] + jnp.dot(mix, wo_ref[...], preferred_element_type=F32)


def _oproj(h, fox, pool, wo, *, name):
    n, d = h.shape
    tm = min(512, n)
    assert n % tm == 0
    return pl.pallas_call(
        _oproj_kernel,
        grid=(n // tm,),
        in_specs=[
            pl.BlockSpec((tm, d), lambda i: (i, 0)),
            pl.BlockSpec((tm, fox.shape[1]), lambda i: (i, 0)),
            pl.BlockSpec((tm, pool.shape[1]), lambda i: (i, 0)),
            pl.BlockSpec(wo.shape, lambda i: (0, 0)),
        ],
        out_specs=pl.BlockSpec((tm, d), lambda i: (i, 0)),
        out_shape=jax.ShapeDtypeStruct((n, d), F32),
        compiler_params=_cparams(("parallel",)),
        name=name,
    )(h, fox, pool, wo)


def _round_up(x, m):
    return (x + m - 1) // m * m


def kernel(x_prompt, x_sample, cache_k, cache_v, cache_logf, state_pool, g_ffn1, w1_gate, w1_up, w1_down, g_mix, w_in, b_f, w_pool, pool_scale, w_o, g_ffn2, w2_gate, w2_up, w2_down, g_final):
    B, S, D = x_prompt.shape
    Bs, Tn, _ = x_sample.shape
    depth, _, P, H, Dh = cache_k.shape
    assert depth == 1 and Dh == HEAD_DIM
    fw = H * Dh
    pw = w_pool.shape[1] * w_pool.shape[2]
    assert w_in.shape[2] == 3 * fw + H + pw and state_pool.shape[2] == POOL_HIST

    row = lambda a: a.reshape(1, -1).astype(F32)
    w1g, w1u, w1d = w1_gate[0].astype(BF16), w1_up[0].astype(BF16), w1_down[0].astype(BF16)
    w2g, w2u, w2d = w2_gate[0].astype(BF16), w2_up[0].astype(BF16), w2_down[0].astype(BF16)
    wi = w_in[0]
    w_qkvp = jnp.concatenate([wi[:, :3 * fw], wi[:, 3 * fw + H:]], axis=1).astype(BF16)
    wf_cols = jnp.tile(wi[:, 3 * fw:3 * fw + H], (1, N_SPLIT))
    wf = jnp.pad(wf_cols, ((0, 0), (0, LANES - N_SPLIT * H))).astype(BF16)
    bf = jnp.pad(jnp.tile(b_f[0].astype(F32), N_SPLIT), (0, LANES - N_SPLIT * H)).reshape(1, LANES)
    wo = w_o[0].astype(BF16)
    wp = w_pool[0].astype(BF16)
    g1, gm, g2, gfin, psc = row(g_ffn1[0]), row(g_mix[0]), row(g_ffn2[0]), row(g_final), row(pool_scale[0])

    def pre(x2d, seq_len, tag):
        h = _ffn(x2d, g1, w1g, w1u, w1d, gfin, final_norm=False, name=f"ffn1_{tag}")
        return (h,) + tuple(_inproj(h, gm, w_qkvp, wf, bf, seq_len=seq_len, name=f"inproj_{tag}"))

    def post(h, fox, pool, tag):
        h2 = _oproj(h, fox, pool, wo, name=f"oproj_{tag}")
        return _ffn(h2, g2, w2g, w2u, w2d, gfin, final_norm=True, name=f"ffn2_{tag}")

    h, q, k, v, kh, vt, p, lf, qn2, kn2 = pre(x_prompt.reshape(B * S, D), S, "prompt")
    qx, kx, ends = _aug(lf.reshape(B, S, LANES), n_heads=H, tile_cap=ATTN_TILE, name="aug_prompt")
    first = _first_needed_chunk(qn2, kn2, ends, H)
    fox = _attn(first, q.reshape(B, S, fw), qx, kh, kx, vt, name="attn_prompt")
    p3 = p.reshape(B, S, pw)
    pool = _pool(p3, jnp.zeros((B, HIST_ROWS, pw), F32), wp, psc, pos0=0, name="pool_prompt")
    y_prompt = post(h, fox.reshape(B * S, fw), pool.reshape(B * S, pw), "prompt").reshape(B, S, D)
    k_prompt = k.reshape(1, B, S, H, Dh)
    v_prompt = v.reshape(1, B, S, H, Dh)
    logf_prompt = lf[:, :H].reshape(1, B, S, H)
    pool_prompt = p3[:, S - POOL_HIST:][None]

    h, q, k, v, kb, vb, p, lf, _, _ = pre(x_sample.reshape(Bs * Tn, D), None, "sample")
    L = _round_up(P + Tn, LANES)
    tail = L - P - Tn
    lf_cache = jnp.pad(jnp.tile(cache_logf[0].astype(F32), (1, 1, N_SPLIT)),
                       ((0, 0), (0, 0), (0, LANES - N_SPLIT * H)))
    lf_all = jnp.concatenate([lf_cache, lf.reshape(Bs, Tn, LANES), jnp.zeros((Bs, tail, LANES), F32)], axis=1)
    qx, kx, _ = _aug(lf_all, n_heads=H, tile_cap=L, name="aug_sample")
    fox = _attn_cached(q, qx, kx, cache_k[0], cache_v[0], kb, vb, n_seq=Bs, past=P, name="attn_sample")
    p3 = p.reshape(Bs, Tn, pw)
    hist = jnp.concatenate([jnp.zeros((Bs, HIST_ROWS - POOL_HIST, pw), F32), state_pool[0].astype(F32)], axis=1)
    pool = _pool(p3, hist, wp, psc, pos0=P, name="pool_sample")
    y_sample = post(h, fox, pool.reshape(Bs * Tn, pw), "sample").reshape(Bs, Tn, D)
    k_sample = k.reshape(1, Bs, Tn, H, Dh)
    v_sample = v.reshape(1, Bs, Tn, H, Dh)
    logf_sample = lf[:, :H].reshape(1, Bs, Tn, H)
    ext_tail = jnp.concatenate([state_pool[0].astype(F32), p3], axis=1)[:, Tn:]
    pool_sample = ext_tail[None]

    return (y_prompt, y_sample, k_prompt, v_prompt, logf_prompt, pool_prompt,
            k_sample, v_sample, logf_sample, pool_sample)
```

```python
import functools

import jax
import jax.numpy as jnp
from jax import lax
from jax.experimental import pallas as pl
from jax.experimental.pallas import tpu as pltpu

F32 = jnp.float32
BF16 = jnp.bfloat16

EPS = 1e-6
HEAD_DIM = 128
LANES = 128
POOL_WINDOWS = (2, 4, 8, 16)
POOL_HIST = max(POOL_WINDOWS) - 1
HIST_ROWS = 16
N_SPLIT = 3
VT_ROWS = HEAD_DIM + 16
LOG2E = 1.4426950408889634
VMEM_LIMIT = 56 * 1024 * 1024
FFN_VMEM_LIMIT = 62 * 1024 * 1024
ATTN_UNROLL = 4
ATTN_FLAT = 6
ATTN_TILE = 512
PRUNE_GAP = 170.0
FFN_ROWS = 1024
FFN_COLS = 256
FFN_SLOTS = 4
FFN_NORM_ROWS = 256


def _cparams(sem, vmem_limit=VMEM_LIMIT):
    return pltpu.CompilerParams(dimension_semantics=sem, vmem_limit_bytes=vmem_limit)


def _pick_tile(length, cap, mult):
    best = None
    for t in range(mult, min(cap, length) + 1, mult):
        if length % t == 0:
            best = t
    assert best is not None, (length, cap, mult)
    return best


def _rms(x, g):
    return x * lax.rsqrt(jnp.mean(x * x, axis=-1, keepdims=True) + EPS) * g


def _split3(x):
    hi = x.astype(BF16).astype(F32)
    r1 = x - hi
    mid = r1.astype(BF16).astype(F32)
    lo = (r1 - mid).astype(BF16).astype(F32)
    return hi, mid, lo


def _ffn_kernel(x_ref, g_ref, gf_ref, wg_hbm, wu_hbm, wd_hbm, o_ref, xn_ref, wg_buf, wu_buf, wd_buf, sem, *,
                final_norm, tf, nf):
    def copies(f):
        slot = f % FFN_SLOTS
        cols = pl.ds(pl.multiple_of(f * tf, tf), tf)
        return (pltpu.make_async_copy(wg_hbm.at[:, cols], wg_buf.at[slot], sem.at[0, slot]),
                pltpu.make_async_copy(wu_hbm.at[:, cols], wu_buf.at[slot], sem.at[1, slot]),
                pltpu.make_async_copy(wd_hbm.at[cols, :], wd_buf.at[slot], sem.at[2, slot]))

    def start(f):
        for c in copies(f):
            c.start(priority=1)

    def wait(f):
        for c in copies(f):
            c.wait()

    def chunk(f, xn):
        slot = f % FFN_SLOTS
        gate = jnp.dot(xn, wg_buf[slot], preferred_element_type=F32)
        up = jnp.dot(xn, wu_buf[slot], preferred_element_type=F32)
        a = (gate * jax.nn.sigmoid(gate) * up).astype(BF16)
        return jnp.dot(a, wd_buf[slot], preferred_element_type=F32)

    tile = pl.program_id(0)

    @pl.when(tile == 0)
    def _():
        start(0)

    start(1)
    start(2)
    wait(0)
    xn = _rms(x_ref[...], g_ref[...]).astype(BF16)
    xn_ref[...] = xn
    o_ref[...] = chunk(0, xn)

    def pair(i, carry):
        f = 1 + 2 * i
        wait(f)
        wait(f + 1)
        start(f + 2)

        @pl.when(f + 3 < nf)
        def _():
            start(f + 3)

        xn = xn_ref[...]
        o_ref[...] += chunk(f, xn)
        o_ref[...] += chunk(f + 1, xn)
        return carry

    lax.fori_loop(0, (nf - 2) // 2, pair, 0)
    wait(nf - 1)

    @pl.when(tile + 1 < pl.num_programs(0))
    def _():
        start(0)

    o_ref[...] = x_ref[...] + 0.5 * (o_ref[...] + chunk(nf - 1, xn_ref[...]))
    if final_norm:
        rows = min(FFN_NORM_ROWS, o_ref.shape[0])
        for r in range(0, o_ref.shape[0], rows):
            o_ref[r:r + rows, :] = _rms(o_ref[r:r + rows, :], gf_ref[...])


def _ffn(x, g, wg, wu, wd, gf, *, final_norm, name):
    n, d = x.shape
    dff = wg.shape[1]
    tm = min(FFN_ROWS, n)
    tf = FFN_COLS
    nf = dff // tf
    assert n % tm == 0 and dff % tf == 0 and nf >= 4 and nf % 2 == 0
    return pl.pallas_call(
        functools.partial(_ffn_kernel, final_norm=final_norm, tf=tf, nf=nf),
        grid=(n // tm,),
        in_specs=[
            pl.BlockSpec((tm, d), lambda i: (i, 0)),
            pl.BlockSpec((1, d), lambda i: (0, 0)),
            pl.BlockSpec((1, d), lambda i: (0, 0)),
            pl.BlockSpec(memory_space=pl.ANY),
            pl.BlockSpec(memory_space=pl.ANY),
            pl.BlockSpec(memory_space=pl.ANY),
        ],
        out_specs=pl.BlockSpec((tm, d), lambda i: (i, 0)),
        out_shape=jax.ShapeDtypeStruct((n, d), F32),
        scratch_shapes=[pltpu.VMEM((tm, d), BF16),
                        pltpu.VMEM((FFN_SLOTS, d, tf), BF16), pltpu.VMEM((FFN_SLOTS, d, tf), BF16),
                        pltpu.VMEM((FFN_SLOTS, tf, d), BF16), pltpu.SemaphoreType.DMA((3, FFN_SLOTS))],
        compiler_params=_cparams(("arbitrary",), FFN_VMEM_LIMIT),
        name=name,
    )(x, g, gf, wg, wu, wd)


def _inproj_kernel(h_ref, g_ref, w_ref, wf_ref, bf_ref,
                   q_ref, k_ref, v_ref, ka_ref, va_ref, p_ref, lf_ref, qn_ref, kn_ref, *, scale, width, head_major):
    hn = _rms(h_ref[...], g_ref[...]).astype(BF16)
    n_heads = width // HEAD_DIM

    def proj(c):
        return jnp.dot(hn, w_ref[:, c * width:(c + 1) * width], preferred_element_type=F32)

    def store_heads(o_ref, u):
        for hd in range(n_heads):
            o_ref[pl.ds(hd, u.shape[0], stride=n_heads), :] = u[:, hd * HEAD_DIM:(hd + 1) * HEAD_DIM]

    def max_sq_norms(xb):
        sub = lax.broadcasted_iota(jnp.int32, (n_heads, LANES), 0)
        out = jnp.zeros((n_heads, LANES), F32)
        for hd in range(n_heads):
            x = xb[:, hd * HEAD_DIM:(hd + 1) * HEAD_DIM].astype(F32)
            top = jnp.max(jnp.sum(x * x, axis=1, keepdims=True), axis=0, keepdims=True)
            out = jnp.where(sub == hd, top, out)
        return out

    lf_ref[...] = jax.nn.log_sigmoid(jnp.dot(hn, wf_ref[...], preferred_element_type=F32) + bf_ref[...])
    qb = (proj(0) * scale).astype(BF16)
    q_ref[...] = qb
    qn_ref[...] = max_sq_norms(qb)
    u = proj(1)
    store_heads(k_ref, u)
    kb = u.astype(BF16)
    kn_ref[...] = max_sq_norms(kb)
    u = proj(2)
    store_heads(v_ref, u)
    vb = u.astype(BF16)
    if head_major:
        sub = lax.broadcasted_iota(jnp.int32, (VT_ROWS - HEAD_DIM, vb.shape[0]), 0)
        tail = jnp.where(sub == 0, 1.0, 0.0).astype(BF16)
        for hd in range(n_heads):
            cols = slice(hd * HEAD_DIM, (hd + 1) * HEAD_DIM)
            ka_ref[0, hd] = kb[:, cols]
            va_ref[0, hd, 0:HEAD_DIM, :] = vb[:, cols].T
            va_ref[0, hd, HEAD_DIM:, :] = tail
    else:
        ka_ref[...] = kb
        va_ref[...] = vb
    p_ref[...] = proj(3)


def _inproj(h, g, w_qkvp, wf, bf, *, seq_len, name):
    n, d = h.shape
    w = w_qkvp.shape[1] // 4
    tm = min(ATTN_TILE, n)
    assert n % tm == 0 and w % HEAD_DIM == 0
    n_heads = w // HEAD_DIM
    assert n_heads == 8
    row = lambda i: (i, 0)
    const = lambda i: (0, 0)
    resident = pl.Buffered(1)
    if seq_len is None:
        ka_spec = va_spec = pl.BlockSpec((tm, w), row)
        ka_shape = va_shape = jax.ShapeDtypeStruct((n, w), BF16)
    else:
        assert seq_len % tm == 0 and n % seq_len == 0
        tps = seq_len // tm
        ka_spec = pl.BlockSpec((1, n_heads, tm, HEAD_DIM), lambda i: (i // tps, 0, i % tps, 0))
        va_spec = pl.BlockSpec((1, n_heads, VT_ROWS, tm), lambda i: (i // tps, 0, 0, i % tps))
        ka_shape = jax.ShapeDtypeStruct((n // seq_len, n_heads, seq_len, HEAD_DIM), BF16)
        va_shape = jax.ShapeDtypeStruct((n // seq_len, n_heads, VT_ROWS, seq_len), BF16)
    return pl.pallas_call(
        functools.partial(_inproj_kernel, scale=HEAD_DIM ** -0.5 * LOG2E, width=w, head_major=seq_len is not None),
        grid=(n // tm,),
        in_specs=[
            pl.BlockSpec((tm, d), row),
            pl.BlockSpec((1, d), const),
            pl.BlockSpec((d, 4 * w), const, pipeline_mode=resident),
            pl.BlockSpec((d, LANES), const),
            pl.BlockSpec((1, LANES), const),
        ],
        out_specs=[
            pl.BlockSpec((tm, w), row),
            pl.BlockSpec((tm * n_heads, HEAD_DIM), row), pl.BlockSpec((tm * n_heads, HEAD_DIM), row),
            ka_spec, va_spec, pl.BlockSpec((tm, w), row),
            pl.BlockSpec((tm, LANES), row),
            pl.BlockSpec((n_heads, LANES), row), pl.BlockSpec((n_heads, LANES), row),
        ],
        out_shape=[
            jax.ShapeDtypeStruct((n, w), BF16),
            jax.ShapeDtypeStruct((n * n_heads, HEAD_DIM), F32),
            jax.ShapeDtypeStruct((n * n_heads, HEAD_DIM), F32),
            ka_shape,
            va_shape,
            jax.ShapeDtypeStruct((n, w), F32),
            jax.ShapeDtypeStruct((n, LANES), F32),
            jax.ShapeDtypeStruct((n // tm * n_heads, LANES), F32),
            jax.ShapeDtypeStruct((n // tm * n_heads, LANES), F32),
        ],
        compiler_params=_cparams(("parallel",)),
        name=name,
    )(h, g, w_qkvp, wf, bf)


def _aug_kernel(lf_ref, tri_ref, selq_ref, selk_ref, qx_ref, kx_ref, ends_ref, carry_ref, *, n_heads):
    t = pl.program_id(1)

    @pl.when(t == 0)
    def _():
        carry_ref[...] = jnp.zeros_like(carry_ref)

    lf = lf_ref[0]
    lane = lax.broadcasted_iota(jnp.int32, lf.shape, 1)

    def pick(parts, extra=None):
        out = jnp.zeros_like(lf) if extra is None else extra
        for j in reversed(range(N_SPLIT)):
            out = jnp.where((lane >= j * n_heads) & (lane < (j + 1) * n_heads), parts[j], out)
        return out

    cs = jnp.dot(tri_ref[...], pick(_split3(lf)).astype(BF16), preferred_element_type=F32)
    c = cs
    for j in range(1, N_SPLIT):
        c = c + pltpu.roll(cs, LANES - j * n_heads, axis=1)
    c = c + carry_ref[...]
    carry_ref[...] = c[c.shape[0] - 1:, :]

    crep = c
    for j in range(1, N_SPLIT):
        crep = jnp.where((lane >= j * n_heads) & (lane < (j + 1) * n_heads),
                         pltpu.roll(c, j * n_heads, axis=1), crep)
    ones = jnp.where(lane == N_SPLIT * n_heads, 1.0, 0.0).astype(F32)
    pieces = pick(_split3(crep * LOG2E), extra=ones).astype(BF16)
    qx_ref[0] = jnp.dot(pieces, selq_ref[...], preferred_element_type=F32).astype(BF16)
    kx = jnp.dot(pieces, selk_ref[...], preferred_element_type=F32).astype(BF16)
    for hd in range(n_heads):
        kx_ref[0, hd] = kx[:, hd * HEAD_DIM:(hd + 1) * HEAD_DIM]
    sub = lax.broadcasted_iota(jnp.int32, (8, LANES), 0)
    ends_ref[0] = jnp.where(sub == 0, c[0:1, :] * LOG2E, jnp.where(sub == 1, c[c.shape[0] - 1:, :] * LOG2E, 0.0))


def _aug_constants(tm, n_heads):
    r = jnp.arange(tm)
    tri = (r[:, None] >= r[None, :]).astype(BF16)
    rows = jnp.arange(LANES)[:, None]
    cols = jnp.arange(n_heads * HEAD_DIM)[None, :]
    head, lane = cols // HEAD_DIM, cols % HEAD_DIM
    one_row = N_SPLIT * n_heads
    is_piece = lambda off: (lane >= off) & (lane < off + N_SPLIT) & (rows == (lane - off) * n_heads + head)
    is_one = lambda off: (lane >= off) & (lane < off + N_SPLIT) & (rows == one_row)
    selq = (is_piece(0) | is_one(N_SPLIT)).astype(BF16)
    selk = is_one(0).astype(BF16) - is_piece(N_SPLIT).astype(BF16)
    return tri, selq, selk


def _aug(lf, *, n_heads, tile_cap, name):
    nseq, length, _ = lf.shape
    tm = _pick_tile(length, tile_cap, 16)
    assert (N_SPLIT * n_heads) < LANES
    w = n_heads * HEAD_DIM
    tri, selq, selk = _aug_constants(tm, n_heads)
    return pl.pallas_call(
        functools.partial(_aug_kernel, n_heads=n_heads),
        grid=(nseq, length // tm),
        in_specs=[
            pl.BlockSpec((1, tm, LANES), lambda b, t: (b, t, 0)),
            pl.BlockSpec((tm, tm), lambda b, t: (0, 0)),
            pl.BlockSpec((LANES, w), lambda b, t: (0, 0)),
            pl.BlockSpec((LANES, w), lambda b, t: (0, 0)),
        ],
        out_specs=[pl.BlockSpec((1, tm, w), lambda b, t: (b, t, 0)),
                   pl.BlockSpec((1, n_heads, tm, HEAD_DIM), lambda b, t: (b, 0, t, 0)),
                   pl.BlockSpec((1, 8, LANES), lambda b, t: (b, t, 0))],
        out_shape=[jax.ShapeDtypeStruct((nseq, length, w), BF16),
                   jax.ShapeDtypeStruct((nseq, n_heads, length, HEAD_DIM), BF16),
                   jax.ShapeDtypeStruct((nseq, length // tm * 8, LANES), F32)],
        scratch_shapes=[pltpu.VMEM((1, LANES), F32)],
        compiler_params=_cparams(("parallel", "arbitrary")),
        name=name,
    )(lf, tri, selq, selk)


def _attn_kernel(first_ref, q_ref, qx_ref, k_ref, kx_ref, vt_ref, o_ref, m_ref, acc_ref, s0_ref, s1_ref, *, t):
    b, hd, qi = pl.program_id(0), pl.program_id(1), pl.program_id(2)
    q = jnp.concatenate([q_ref[0], qx_ref[0]], axis=1)
    n_rest = qi - first_ref[(b * pl.num_programs(1) + hd) * pl.num_programs(2) + qi]
    m_ref[...] = jnp.full_like(m_ref, -jnp.inf)
    acc_ref[...] = jnp.zeros_like(acc_ref)

    def scores(j):
        st = pl.multiple_of(jnp.maximum(j, 0) * t, t)
        k = jnp.concatenate([k_ref[0, 0, pl.ds(st, t), :], kx_ref[0, 0, pl.ds(st, t), :]], axis=1)
        return lax.dot_general(k, q, (((1,), (1,)), ((), ())), preferred_element_type=F32)

    def accumulate(j, s, masked):
        st = pl.multiple_of(j * t, t)
        if masked:
            key_pos = lax.broadcasted_iota(jnp.int32, (t, t), 0)
            qry_pos = lax.broadcasted_iota(jnp.int32, (t, t), 1)
            s = jnp.where(key_pos <= qry_pos, s, -jnp.inf)
        m_prev = m_ref[...]
        m_new = jnp.maximum(m_prev, jnp.max(s, axis=0, keepdims=True))
        alpha = jnp.exp2(m_prev - m_new)
        p = jnp.exp2(s - m_new).astype(BF16)
        acc_ref[...] = alpha * acc_ref[...] + jnp.dot(vt_ref[0, 0, :, pl.ds(st, t)], p, preferred_element_type=F32)
        m_ref[...] = m_new

    bufs = (s0_ref, s1_ref)

    def visit(k0, n, prefetch_after):
        for u in range(n):
            k = k0[0] + u
            if u + 1 < n or prefetch_after:
                bufs[(k0[1] + u + 1) % 2][...] = scores(qi - k - 1)
            accumulate(qi - k, bufs[(k0[1] + u) % 2][...], False)

    def finish():
        acc = acc_ref[...]
        out_t = acc[0:HEAD_DIM] / acc[HEAD_DIM:HEAD_DIM + 1]
        o_ref[0] = out_t.T.astype(o_ref.dtype)

    for r in range(ATTN_FLAT):
        @pl.when(n_rest == r)
        def _(r=r):
            s_cur = scores(qi)
            for u in range(r + 1):
                s_next = scores(qi - u - 1) if u < r else None
                accumulate(qi - u, s_cur, u == 0)
                s_cur = s_next
            finish()

    @pl.when(n_rest >= ATTN_FLAT)
    def _():
        s0_ref[...] = scores(qi)
        s1_ref[...] = scores(qi - 1)
        accumulate(qi, s0_ref[...], True)

        def group(g, carry):
            visit((1 + ATTN_UNROLL * g, 1), ATTN_UNROLL, True)
            return carry

        lax.fori_loop(0, n_rest // ATTN_UNROLL, group, 0)
        rem = n_rest % ATTN_UNROLL
        for r in range(1, ATTN_UNROLL):
            @pl.when(rem == r)
            def _(r=r):
                visit((n_rest - r + 1, 1), r, False)

        finish()


def _first_needed_chunk(qn2, kn2, ends, n_heads):
    nb, nt = ends.shape[0], ends.shape[1] // 8
    qn = jnp.sqrt(qn2.reshape(nb, nt, n_heads, LANES)[..., 0])
    kn = jnp.sqrt(kn2.reshape(nb, nt, n_heads, LANES)[..., 0])
    ends = ends.reshape(nb, nt, 8, LANES)
    c_first, c_last = ends[:, :, 0, :n_heads], ends[:, :, 1, :n_heads]
    ub = qn[:, :, None, :] * kn[:, None, :, :] + c_first[:, :, None, :] - c_last[:, None, :, :]
    lb = -(qn * kn)[:, :, None, :]
    i = jnp.arange(nt)[None, :, None, None]
    j = jnp.arange(nt)[None, None, :, None]
    needed = (ub - lb >= -PRUNE_GAP) & (j < i)
    first = jnp.min(jnp.where(needed, j, i), axis=2)
    return first.transpose(0, 2, 1).reshape(-1).astype(jnp.int32)


def _attn(first, q, qx, k, kx, vt, *, name):
    nb, length, w = q.shape
    n_heads = w // HEAD_DIM
    t = ATTN_TILE
    assert length % t == 0 and k.shape == (nb, n_heads, length, HEAD_DIM) and vt.shape == (nb, n_heads, VT_ROWS, length)
    blk = lambda b, h, i, first: (b, i, h)
    head = lambda b, h, i, first: (b, h, 0, 0)
    return pl.pallas_call(
        functools.partial(_attn_kernel, t=t),
        grid_spec=pltpu.PrefetchScalarGridSpec(
            num_scalar_prefetch=1,
            grid=(nb, n_heads, length // t),
            in_specs=[pl.BlockSpec((1, t, HEAD_DIM), blk), pl.BlockSpec((1, t, HEAD_DIM), blk),
                      pl.BlockSpec((1, 1, length, HEAD_DIM), head), pl.BlockSpec((1, 1, length, HEAD_DIM), head),
                      pl.BlockSpec((1, 1, VT_ROWS, length), head)],
            out_specs=pl.BlockSpec((1, t, HEAD_DIM), blk),
            scratch_shapes=[pltpu.VMEM((1, t), F32), pltpu.VMEM((VT_ROWS, t), F32),
                            pltpu.VMEM((t, t), F32), pltpu.VMEM((t, t), F32)],
        ),
        out_shape=jax.ShapeDtypeStruct((nb, length, w), BF16),
        compiler_params=_cparams(("parallel", "parallel", "arbitrary")),
        name=name,
    )(first, q, qx, k, kx, vt)


def _attn_cached_kernel(q_ref, qx_ref, kx_ref, ck_ref, cv_ref, kn_ref, vn_ref, o_ref, *, n_heads, past, tn):
    pad = LANES - tn
    causal = (lax.broadcasted_iota(jnp.int32, (tn, LANES), 1) <= lax.broadcasted_iota(jnp.int32, (tn, LANES), 0))
    nt = (((1,), (1,)), ((), ()))
    for hd in range(n_heads):
        cols = slice(hd * HEAD_DIM, (hd + 1) * HEAD_DIM)
        q = jnp.concatenate([q_ref[:, cols], qx_ref[0, :, cols]], axis=1)
        kc = jnp.concatenate([ck_ref[pl.ds(hd, past, stride=n_heads), :].astype(BF16),
                              kx_ref[0, hd, 0:past, :]], axis=1)
        kn = jnp.concatenate([kn_ref[:, cols], kx_ref[0, hd, past:past + tn, :]], axis=1)
        kn = jnp.concatenate([kn, jnp.zeros((pad, kn.shape[1]), BF16)], axis=0)
        s_c = lax.dot_general(q, kc, nt, preferred_element_type=F32)
        s_n = jnp.where(causal, lax.dot_general(q, kn, nt, preferred_element_type=F32), -jnp.inf)
        m = jnp.maximum(jnp.max(s_c, axis=1, keepdims=True), jnp.max(s_n, axis=1, keepdims=True))
        p_c = jnp.exp2(s_c - m)
        p_n = jnp.exp2(s_n - m)
        denom = jnp.sum(p_c, axis=1, keepdims=True) + jnp.sum(p_n, axis=1, keepdims=True)
        vc = cv_ref[pl.ds(hd, past, stride=n_heads), :].astype(BF16)
        vn = jnp.concatenate([vn_ref[:, cols], jnp.zeros((pad, HEAD_DIM), BF16)], axis=0)
        out = (jnp.dot(p_c.astype(BF16), vc, preferred_element_type=F32)
               + jnp.dot(p_n.astype(BF16), vn, preferred_element_type=F32))
        o_ref[:, cols] = (out / denom).astype(o_ref.dtype)


def _attn_cached(q, qx, kx, cache_k, cache_v, k_new, v_new, *, n_seq, past, name):
    n, w = q.shape
    tn = n // n_seq
    n_heads = w // HEAD_DIM
    assert tn <= LANES and tn % 16 == 0 and past % tn == 0 and kx.shape[2] >= past + tn
    ck = cache_k.reshape(n_seq * past * n_heads, HEAD_DIM)
    cv = cache_v.reshape(n_seq * past * n_heads, HEAD_DIM)
    rows = pl.BlockSpec((tn, w), lambda b: (b, 0))
    cached = pl.BlockSpec((past * n_heads, HEAD_DIM), lambda b: (b, 0))
    return pl.pallas_call(
        functools.partial(_attn_cached_kernel, n_heads=n_heads, past=past, tn=tn),
        grid=(n_seq,),
        in_specs=[
            rows,
            pl.BlockSpec((1, tn, w), lambda b: (b, past // tn, 0)),
            pl.BlockSpec((1,) + kx.shape[1:], lambda b: (b, 0, 0, 0)),
            cached, cached, rows, rows,
        ],
        out_specs=rows,
        out_shape=jax.ShapeDtypeStruct((n, w), BF16),
        compiler_params=_cparams(("parallel",)),
        name=name,
    )(q, qx, kx, ck, cv, k_new, v_new)


def _pool_kernel(p_ref, hist_ref, w_ref, sc_ref, o_ref, ext_ref, *, tp, pos0):
    t = pl.program_id(1)

    @pl.when(t == 0)
    def _():
        ext_ref[0:HIST_ROWS, :] = hist_ref[0]

    ext_ref[HIST_ROWS:, :] = p_ref[0]
    pos = pos0 + t * tp + lax.broadcasted_iota(jnp.int32, (tp, 1), 0)
    gd = w_ref.shape[1]
    for g, win in enumerate(POOL_WINDOWS):
        cols = slice(g * gd, (g + 1) * gd)
        cur = ext_ref[HIST_ROWS:, cols]
        tot = cur
        for back in range(1, win):
            tot = tot + ext_ref[HIST_ROWS - back:HIST_ROWS - back + tp, cols]
        cnt = jnp.minimum(pos + 1, win).astype(F32)
        d = (tot / cnt - cur).astype(BF16)
        y = jnp.dot(d, w_ref[g], preferred_element_type=F32) * sc_ref[:, cols]
        o_ref[0, :, cols] = y.astype(o_ref.dtype)
    ext_ref[0:HIST_ROWS, :] = ext_ref[tp:tp + HIST_ROWS, :]


def _pool(p, hist, w_pool, scale, *, pos0, name):
    nseq, length, w = p.shape
    tp = min(512, length)
    assert length % tp == 0 and tp >= HIST_ROWS
    return pl.pallas_call(
        functools.partial(_pool_kernel, tp=tp, pos0=pos0),
        grid=(nseq, length // tp),
        in_specs=[
            pl.BlockSpec((1, tp, w), lambda b, t: (b, t, 0)),
            pl.BlockSpec((1, HIST_ROWS, w), lambda b, t: (b, 0, 0)),
            pl.BlockSpec(w_pool.shape, lambda b, t: (0, 0, 0)),
            pl.BlockSpec((1, w), lambda b, t: (0, 0)),
        ],
        out_specs=pl.BlockSpec((1, tp, w), lambda b, t: (b, t, 0)),
        out_shape=jax.ShapeDtypeStruct((nseq, length, w), BF16),
        scratch_shapes=[pltpu.VMEM((HIST_ROWS + tp, w), F32)],
        compiler_params=_cparams(("parallel", "arbitrary")),
        name=name,
    )(p, hist, w_pool, scale)


def _oproj_kernel(h_ref, fox_ref, pool_ref, wo_ref, o_ref):
    mix = jnp.concatenate([fox_ref[...], pool_ref[...]], axis=1)
    o_ref[...] = h_ref[...] + jnp.dot(mix, wo_ref[...], preferred_element_type=F32)


def _oproj(h, fox, pool, wo, *, name):
    n, d = h.shape
    tm = min(512, n)
    assert n % tm == 0
    return pl.pallas_call(
        _oproj_kernel,
        grid=(n // tm,),
        in_specs=[
            pl.BlockSpec((tm, d), lambda i: (i, 0)),
            pl.BlockSpec((tm, fox.shape[1]), lambda i: (i, 0)),
            pl.BlockSpec((tm, pool.shape[1]), lambda i: (i, 0)),
            pl.BlockSpec(wo.shape, lambda i: (0, 0)),
        ],
        out_specs=pl.BlockSpec((tm, d), lambda i: (i, 0)),
        out_shape=jax.ShapeDtypeStruct((n, d), F32),
        compiler_params=_cparams(("parallel",)),
        name=name,
    )(h, fox, pool, wo)


def _round_up(x, m):
    return (x + m - 1) // m * m


def kernel(x_prompt, x_sample, cache_k, cache_v, cache_logf, state_pool, g_ffn1, w1_gate, w1_up, w1_down, g_mix, w_in, b_f, w_pool, pool_scale, w_o, g_ffn2, w2_gate, w2_up, w2_down, g_final):
    B, S, D = x_prompt.shape
    Bs, Tn, _ = x_sample.shape
    depth, _, P, H, Dh = cache_k.shape
    assert depth == 1 and Dh == HEAD_DIM
    fw = H * Dh
    pw = w_pool.shape[1] * w_pool.shape[2]
    assert w_in.shape[2] == 3 * fw + H + pw and state_pool.shape[2] == POOL_HIST

    row = lambda a: a.reshape(1, -1).astype(F32)
    w1g, w1u, w1d = w1_gate[0].astype(BF16), w1_up[0].astype(BF16), w1_down[0].astype(BF16)
    w2g, w2u, w2d = w2_gate[0].astype(BF16), w2_up[0].astype(BF16), w2_down[0].astype(BF16)
    wi = w_in[0]
    w_qkvp = jnp.concatenate([wi[:, :3 * fw], wi[:, 3 * fw + H:]], axis=1).astype(BF16)
    wf_cols = jnp.tile(wi[:, 3 * fw:3 * fw + H], (1, N_SPLIT))
    wf = jnp.pad(wf_cols, ((0, 0), (0, LANES - N_SPLIT * H))).astype(BF16)
    bf = jnp.pad(jnp.tile(b_f[0].astype(F32), N_SPLIT), (0, LANES - N_SPLIT * H)).reshape(1, LANES)
    wo = w_o[0].astype(BF16)
    wp = w_pool[0].astype(BF16)
    g1, gm, g2, gfin, psc = row(g_ffn1[0]), row(g_mix[0]), row(g_ffn2[0]), row(g_final), row(pool_scale[0])

    def pre(x2d, seq_len, tag):
        h = _ffn(x2d, g1, w1g, w1u, w1d, gfin, final_norm=False, name=f"ffn1_{tag}")
        return (h,) + tuple(_inproj(h, gm, w_qkvp, wf, bf, seq_len=seq_len, name=f"inproj_{tag}"))

    def post(h, fox, pool, tag):
        h2 = _oproj(h, fox, pool, wo, name=f"oproj_{tag}")
        return _ffn(h2, g2, w2g, w2u, w2d, gfin, final_norm=True, name=f"ffn2_{tag}")

    h, q, k, v, kh, vt, p, lf, qn2, kn2 = pre(x_prompt.reshape(B * S, D), S, "prompt")
    qx, kx, ends = _aug(lf.reshape(B, S, LANES), n_heads=H, tile_cap=ATTN_TILE, name="aug_prompt")
    first = _first_needed_chunk(qn2, kn2, ends, H)
    fox = _attn(first, q.reshape(B, S, fw), qx, kh, kx, vt, name="attn_prompt")
    p3 = p.reshape(B, S, pw)
    pool = _pool(p3, jnp.zeros((B, HIST_ROWS, pw), F32), wp, psc, pos0=0, name="pool_prompt")
    y_prompt = post(h, fox.reshape(B * S, fw), pool.reshape(B * S, pw), "prompt").reshape(B, S, D)
    k_prompt = k.reshape(1, B, S, H, Dh)
    v_prompt = v.reshape(1, B, S, H, Dh)
    logf_prompt = lf[:, :H].reshape(1, B, S, H)
    pool_prompt = p3[:, S - POOL_HIST:][None]

    h, q, k, v, kb, vb, p, lf, _, _ = pre(x_sample.reshape(Bs * Tn, D), None, "sample")
    L = _round_up(P + Tn, LANES)
    tail = L - P - Tn
    lf_cache = jnp.pad(jnp.tile(cache_logf[0].astype(F32), (1, 1, N_SPLIT)),
                       ((0, 0), (0, 0), (0, LANES - N_SPLIT * H)))
    lf_all = jnp.concatenate([lf_cache, lf.reshape(Bs, Tn, LANES), jnp.zeros((Bs, tail, LANES), F32)], axis=1)
    qx, kx, _ = _aug(lf_all, n_heads=H, tile_cap=L, name="aug_sample")
    fox = _attn_cached(q, qx, kx, cache_k[0], cache_v[0], kb, vb, n_seq=Bs, past=P, name="attn_sample")
    p3 = p.reshape(Bs, Tn, pw)
    hist = jnp.concatenate([jnp.zeros((Bs, HIST_ROWS - POOL_HIST, pw), F32), state_pool[0].astype(F32)], axis=1)
    pool = _pool(p3, hist, wp, psc, pos0=P, name="pool_sample")
    y_sample = post(h, fox, pool.reshape(Bs * Tn, pw), "sample").reshape(Bs, Tn, D)
    k_sample = k.reshape(1, Bs, Tn, H, Dh)
    v_sample = v.reshape(1, Bs, Tn, H, Dh)
    logf_sample = lf[:, :H].reshape(1, Bs, Tn, H)
    ext_tail = jnp.concatenate([state_pool[0].astype(F32), p3], axis=1)[:, Tn:]
    pool_sample = ext_tail[None]

    return (y_prompt, y_sample, k_prompt, v_prompt, logf_prompt, pool_prompt,
            k_sample, v_sample, logf_sample, pool_sample)
```

```python
import functools

import jax
import jax.numpy as jnp
from jax import lax
from jax.experimental import pallas as pl
from jax.experimental.pallas import tpu as pltpu

F32 = jnp.float32
BF16 = jnp.bfloat16

EPS = 1e-6
HEAD_DIM = 128
LANES = 128
POOL_WINDOWS = (2, 4, 8, 16)
POOL_HIST = max(POOL_WINDOWS) - 1
HIST_ROWS = 16
N_SPLIT = 3
VT_ROWS = HEAD_DIM + 16
LOG2E = 1.4426950408889634
VMEM_LIMIT = 56 * 1024 * 1024
FFN_VMEM_LIMIT = 62 * 1024 * 1024
ATTN_UNROLL = 4
ATTN_FLAT = 6
ATTN_TILE = 512
PRUNE_GAP = 170.0
FFN_ROWS = 1024
FFN_COLS = 512
FFN_SLOTS = 2
FFN_NORM_ROWS = 256


def _cparams(sem, vmem_limit=VMEM_LIMIT):
    return pltpu.CompilerParams(dimension_semantics=sem, vmem_limit_bytes=vmem_limit)


def _pick_tile(length, cap, mult):
    best = None
    for t in range(mult, min(cap, length) + 1, mult):
        if length % t == 0:
            best = t
    assert best is not None, (length, cap, mult)
    return best


def _rms(x, g):
    return x * lax.rsqrt(jnp.mean(x * x, axis=-1, keepdims=True) + EPS) * g


def _split3(x):
    hi = x.astype(BF16).astype(F32)
    r1 = x - hi
    mid = r1.astype(BF16).astype(F32)
    lo = (r1 - mid).astype(BF16).astype(F32)
    return hi, mid, lo


def _ffn_kernel(x_ref, g_ref, gf_ref, wg_hbm, wu_hbm, wd_hbm, o_ref, xn_ref, wg_buf, wu_buf, wd_buf, sem, *,
                final_norm, tf, nf):
    tile = pl.program_id(0)
    flip = (tile * (nf % 2)) % 2

    def copies(f, flip):
        slot = (f + flip) % 2
        cols = pl.ds(pl.multiple_of(f * tf, tf), tf)
        return (pltpu.make_async_copy(wg_hbm.at[:, cols], wg_buf.at[slot], sem.at[0, slot]),
                pltpu.make_async_copy(wu_hbm.at[:, cols], wu_buf.at[slot], sem.at[1, slot]),
                pltpu.make_async_copy(wd_hbm.at[cols, :], wd_buf.at[slot], sem.at[2, slot]))

    def start(f, flip=flip):
        for c in copies(f, flip):
            c.start()

    def wait(f):
        for c in copies(f, flip):
            c.wait()

    def chunk(f, xn):
        slot = (f + flip) % 2
        gate = jnp.dot(xn, wg_buf[slot], preferred_element_type=F32)
        up = jnp.dot(xn, wu_buf[slot], preferred_element_type=F32)
        a = (gate * jax.nn.sigmoid(gate) * up).astype(BF16)
        return jnp.dot(a, wd_buf[slot], preferred_element_type=F32)

    @pl.when(tile == 0)
    def _():
        start(0)

    wait(0)
    start(1)
    xn = _rms(x_ref[...], g_ref[...]).astype(BF16)
    xn_ref[...] = xn
    o_ref[...] = chunk(0, xn)

    def middle(f, carry):
        wait(f)
        start(f + 1)
        o_ref[...] += chunk(f, xn_ref[...])
        return carry

    lax.fori_loop(1, nf - 1, middle, 0)
    wait(nf - 1)

    @pl.when(tile + 1 < pl.num_programs(0))
    def _():
        start(0, ((tile + 1) * (nf % 2)) % 2)

    o_ref[...] = x_ref[...] + 0.5 * (o_ref[...] + chunk(nf - 1, xn_ref[...]))
    if final_norm:
        rows = min(FFN_NORM_ROWS, o_ref.shape[0])
        for r in range(0, o_ref.shape[0], rows):
            o_ref[r:r + rows, :] = _rms(o_ref[r:r + rows, :], gf_ref[...])


def _ffn(x, g, wg, wu, wd, gf, *, final_norm, name):
    n, d = x.shape
    dff = wg.shape[1]
    tm = min(FFN_ROWS, n)
    tf = FFN_COLS
    nf = dff // tf
    assert n % tm == 0 and dff % tf == 0 and nf >= 3
    return pl.pallas_call(
        functools.partial(_ffn_kernel, final_norm=final_norm, tf=tf, nf=nf),
        grid=(n // tm,),
        in_specs=[
            pl.BlockSpec((tm, d), lambda i: (i, 0)),
            pl.BlockSpec((1, d), lambda i: (0, 0)),
            pl.BlockSpec((1, d), lambda i: (0, 0)),
            pl.BlockSpec(memory_space=pl.ANY),
            pl.BlockSpec(memory_space=pl.ANY),
            pl.BlockSpec(memory_space=pl.ANY),
        ],
        out_specs=pl.BlockSpec((tm, d), lambda i: (i, 0)),
        out_shape=jax.ShapeDtypeStruct((n, d), F32),
        scratch_shapes=[pltpu.VMEM((tm, d), BF16),
                        pltpu.VMEM((FFN_SLOTS, d, tf), BF16), pltpu.VMEM((FFN_SLOTS, d, tf), BF16),
                        pltpu.VMEM((FFN_SLOTS, tf, d), BF16), pltpu.SemaphoreType.DMA((3, FFN_SLOTS))],
        compiler_params=_cparams(("arbitrary",), FFN_VMEM_LIMIT),
        name=name,
    )(x, g, gf, wg, wu, wd)


def _inproj_kernel(h_ref, g_ref, w_ref, wf_ref, bf_ref,
                   q_ref, k_ref, v_ref, ka_ref, va_ref, p_ref, lf_ref, qn_ref, kn_ref, *, scale, width, head_major):
    hn = _rms(h_ref[...], g_ref[...]).astype(BF16)
    n_heads = width // HEAD_DIM

    def proj(c):
        return jnp.dot(hn, w_ref[:, c * width:(c + 1) * width], preferred_element_type=F32)

    def store_heads(o_ref, u):
        for hd in range(n_heads):
            o_ref[pl.ds(hd, u.shape[0], stride=n_heads), :] = u[:, hd * HEAD_DIM:(hd + 1) * HEAD_DIM]

    def max_sq_norms(xb):
        sub = lax.broadcasted_iota(jnp.int32, (n_heads, LANES), 0)
        out = jnp.zeros((n_heads, LANES), F32)
        for hd in range(n_heads):
            x = xb[:, hd * HEAD_DIM:(hd + 1) * HEAD_DIM].astype(F32)
            top = jnp.max(jnp.sum(x * x, axis=1, keepdims=True), axis=0, keepdims=True)
            out = jnp.where(sub == hd, top, out)
        return out

    lf_ref[...] = jax.nn.log_sigmoid(jnp.dot(hn, wf_ref[...], preferred_element_type=F32) + bf_ref[...])
    qb = (proj(0) * scale).astype(BF16)
    q_ref[...] = qb
    qn_ref[...] = max_sq_norms(qb)
    u = proj(1)
    store_heads(k_ref, u)
    kb = u.astype(BF16)
    kn_ref[...] = max_sq_norms(kb)
    u = proj(2)
    store_heads(v_ref, u)
    vb = u.astype(BF16)
    if head_major:
        sub = lax.broadcasted_iota(jnp.int32, (VT_ROWS - HEAD_DIM, vb.shape[0]), 0)
        tail = jnp.where(sub == 0, 1.0, 0.0).astype(BF16)
        for hd in range(n_heads):
            cols = slice(hd * HEAD_DIM, (hd + 1) * HEAD_DIM)
            ka_ref[0, hd] = kb[:, cols]
            va_ref[0, hd, 0:HEAD_DIM, :] = vb[:, cols].T
            va_ref[0, hd, HEAD_DIM:, :] = tail
    else:
        ka_ref[...] = kb
        va_ref[...] = vb
    p_ref[...] = proj(3)


def _inproj(h, g, w_qkvp, wf, bf, *, seq_len, name):
    n, d = h.shape
    w = w_qkvp.shape[1] // 4
    tm = min(ATTN_TILE, n)
    assert n % tm == 0 and w % HEAD_DIM == 0
    n_heads = w // HEAD_DIM
    assert n_heads == 8
    row = lambda i: (i, 0)
    const = lambda i: (0, 0)
    resident = pl.Buffered(1)
    if seq_len is None:
        ka_spec = va_spec = pl.BlockSpec((tm, w), row)
        ka_shape = va_shape = jax.ShapeDtypeStruct((n, w), BF16)
    else:
        assert seq_len % tm == 0 and n % seq_len == 0
        tps = seq_len // tm
        ka_spec = pl.BlockSpec((1, n_heads, tm, HEAD_DIM), lambda i: (i // tps, 0, i % tps, 0))
        va_spec = pl.BlockSpec((1, n_heads, VT_ROWS, tm), lambda i: (i // tps, 0, 0, i % tps))
        ka_shape = jax.ShapeDtypeStruct((n // seq_len, n_heads, seq_len, HEAD_DIM), BF16)
        va_shape = jax.ShapeDtypeStruct((n // seq_len, n_heads, VT_ROWS, seq_len), BF16)
    return pl.pallas_call(
        functools.partial(_inproj_kernel, scale=HEAD_DIM ** -0.5 * LOG2E, width=w, head_major=seq_len is not None),
        grid=(n // tm,),
        in_specs=[
            pl.BlockSpec((tm, d), row),
            pl.BlockSpec((1, d), const),
            pl.BlockSpec((d, 4 * w), const, pipeline_mode=resident),
            pl.BlockSpec((d, LANES), const),
            pl.BlockSpec((1, LANES), const),
        ],
        out_specs=[
            pl.BlockSpec((tm, w), row),
            pl.BlockSpec((tm * n_heads, HEAD_DIM), row), pl.BlockSpec((tm * n_heads, HEAD_DIM), row),
            ka_spec, va_spec, pl.BlockSpec((tm, w), row),
            pl.BlockSpec((tm, LANES), row),
            pl.BlockSpec((n_heads, LANES), row), pl.BlockSpec((n_heads, LANES), row),
        ],
        out_shape=[
            jax.ShapeDtypeStruct((n, w), BF16),
            jax.ShapeDtypeStruct((n * n_heads, HEAD_DIM), F32),
            jax.ShapeDtypeStruct((n * n_heads, HEAD_DIM), F32),
            ka_shape,
            va_shape,
            jax.ShapeDtypeStruct((n, w), F32),
            jax.ShapeDtypeStruct((n, LANES), F32),
            jax.ShapeDtypeStruct((n // tm * n_heads, LANES), F32),
            jax.ShapeDtypeStruct((n // tm * n_heads, LANES), F32),
        ],
        compiler_params=_cparams(("parallel",)),
        name=name,
    )(h, g, w_qkvp, wf, bf)


def _aug_kernel(lf_ref, tri_ref, selq_ref, selk_ref, qx_ref, kx_ref, ends_ref, carry_ref, *, n_heads):
    t = pl.program_id(1)

    @pl.when(t == 0)
    def _():
        carry_ref[...] = jnp.zeros_like(carry_ref)

    lf = lf_ref[0]
    lane = lax.broadcasted_iota(jnp.int32, lf.shape, 1)

    def pick(parts, extra=None):
        out = jnp.zeros_like(lf) if extra is None else extra
        for j in reversed(range(N_SPLIT)):
            out = jnp.where((lane >= j * n_heads) & (lane < (j + 1) * n_heads), parts[j], out)
        return out

    cs = jnp.dot(tri_ref[...], pick(_split3(lf)).astype(BF16), preferred_element_type=F32)
    c = cs
    for j in range(1, N_SPLIT):
        c = c + pltpu.roll(cs, LANES - j * n_heads, axis=1)
    c = c + carry_ref[...]
    carry_ref[...] = c[c.shape[0] - 1:, :]

    crep = c
    for j in range(1, N_SPLIT):
        crep = jnp.where((lane >= j * n_heads) & (lane < (j + 1) * n_heads),
                         pltpu.roll(c, j * n_heads, axis=1), crep)
    ones = jnp.where(lane == N_SPLIT * n_heads, 1.0, 0.0).astype(F32)
    pieces = pick(_split3(crep * LOG2E), extra=ones).astype(BF16)
    qx_ref[0] = jnp.dot(pieces, selq_ref[...], preferred_element_type=F32).astype(BF16)
    kx = jnp.dot(pieces, selk_ref[...], preferred_element_type=F32).astype(BF16)
    for hd in range(n_heads):
        kx_ref[0, hd] = kx[:, hd * HEAD_DIM:(hd + 1) * HEAD_DIM]
    sub = lax.broadcasted_iota(jnp.int32, (8, LANES), 0)
    ends_ref[0] = jnp.where(sub == 0, c[0:1, :] * LOG2E, jnp.where(sub == 1, c[c.shape[0] - 1:, :] * LOG2E, 0.0))


def _aug_constants(tm, n_heads):
    r = jnp.arange(tm)
    tri = (r[:, None] >= r[None, :]).astype(BF16)
    rows = jnp.arange(LANES)[:, None]
    cols = jnp.arange(n_heads * HEAD_DIM)[None, :]
    head, lane = cols // HEAD_DIM, cols % HEAD_DIM
    one_row = N_SPLIT * n_heads
    is_piece = lambda off: (lane >= off) & (lane < off + N_SPLIT) & (rows == (lane - off) * n_heads + head)
    is_one = lambda off: (lane >= off) & (lane < off + N_SPLIT) & (rows == one_row)
    selq = (is_piece(0) | is_one(N_SPLIT)).astype(BF16)
    selk = is_one(0).astype(BF16) - is_piece(N_SPLIT).astype(BF16)
    return tri, selq, selk


def _aug(lf, *, n_heads, tile_cap, name):
    nseq, length, _ = lf.shape
    tm = _pick_tile(length, tile_cap, 16)
    assert (N_SPLIT * n_heads) < LANES
    w = n_heads * HEAD_DIM
    tri, selq, selk = _aug_constants(tm, n_heads)
    return pl.pallas_call(
        functools.partial(_aug_kernel, n_heads=n_heads),
        grid=(nseq, length // tm),
        in_specs=[
            pl.BlockSpec((1, tm, LANES), lambda b, t: (b, t, 0)),
            pl.BlockSpec((tm, tm), lambda b, t: (0, 0)),
            pl.BlockSpec((LANES, w), lambda b, t: (0, 0)),
            pl.BlockSpec((LANES, w), lambda b, t: (0, 0)),
        ],
        out_specs=[pl.BlockSpec((1, tm, w), lambda b, t: (b, t, 0)),
                   pl.BlockSpec((1, n_heads, tm, HEAD_DIM), lambda b, t: (b, 0, t, 0)),
                   pl.BlockSpec((1, 8, LANES), lambda b, t: (b, t, 0))],
        out_shape=[jax.ShapeDtypeStruct((nseq, length, w), BF16),
                   jax.ShapeDtypeStruct((nseq, n_heads, length, HEAD_DIM), BF16),
                   jax.ShapeDtypeStruct((nseq, length // tm * 8, LANES), F32)],
        scratch_shapes=[pltpu.VMEM((1, LANES), F32)],
        compiler_params=_cparams(("parallel", "arbitrary")),
        name=name,
    )(lf, tri, selq, selk)


def _attn_kernel(first_ref, q_ref, qx_ref, k_ref, kx_ref, vt_ref, o_ref, m_ref, acc_ref, s0_ref, s1_ref, *, t):
    b, hd, qi = pl.program_id(0), pl.program_id(1), pl.program_id(2)
    q = jnp.concatenate([q_ref[0], qx_ref[0]], axis=1)
    n_rest = qi - first_ref[(b * pl.num_programs(1) + hd) * pl.num_programs(2) + qi]
    m_ref[...] = jnp.full_like(m_ref, -jnp.inf)
    acc_ref[...] = jnp.zeros_like(acc_ref)

    def scores(j):
        st = pl.multiple_of(jnp.maximum(j, 0) * t, t)
        k = jnp.concatenate([k_ref[0, 0, pl.ds(st, t), :], kx_ref[0, 0, pl.ds(st, t), :]], axis=1)
        return lax.dot_general(k, q, (((1,), (1,)), ((), ())), preferred_element_type=F32)

    def accumulate(j, s, masked):
        st = pl.multiple_of(j * t, t)
        if masked:
            key_pos = lax.broadcasted_iota(jnp.int32, (t, t), 0)
            qry_pos = lax.broadcasted_iota(jnp.int32, (t, t), 1)
            s = jnp.where(key_pos <= qry_pos, s, -jnp.inf)
        m_prev = m_ref[...]
        m_new = jnp.maximum(m_prev, jnp.max(s, axis=0, keepdims=True))
        alpha = jnp.exp2(m_prev - m_new)
        p = jnp.exp2(s - m_new).astype(BF16)
        acc_ref[...] = alpha * acc_ref[...] + jnp.dot(vt_ref[0, 0, :, pl.ds(st, t)], p, preferred_element_type=F32)
        m_ref[...] = m_new

    bufs = (s0_ref, s1_ref)

    def visit(k0, n, prefetch_after):
        for u in range(n):
            k = k0[0] + u
            if u + 1 < n or prefetch_after:
                bufs[(k0[1] + u + 1) % 2][...] = scores(qi - k - 1)
            accumulate(qi - k, bufs[(k0[1] + u) % 2][...], False)

    def finish():
        acc = acc_ref[...]
        out_t = acc[0:HEAD_DIM] / acc[HEAD_DIM:HEAD_DIM + 1]
        o_ref[0] = out_t.T.astype(o_ref.dtype)

    for r in range(ATTN_FLAT):
        @pl.when(n_rest == r)
        def _(r=r):
            s_cur = scores(qi)
            for u in range(r + 1):
                s_next = scores(qi - u - 1) if u < r else None
                accumulate(qi - u, s_cur, u == 0)
                s_cur = s_next
            finish()

    @pl.when(n_rest >= ATTN_FLAT)
    def _():
        s0_ref[...] = scores(qi)
        s1_ref[...] = scores(qi - 1)
        accumulate(qi, s0_ref[...], True)

        def group(g, carry):
            visit((1 + ATTN_UNROLL * g, 1), ATTN_UNROLL, True)
            return carry

        lax.fori_loop(0, n_rest // ATTN_UNROLL, group, 0)
        rem = n_rest % ATTN_UNROLL
        for r in range(1, ATTN_UNROLL):
            @pl.when(rem == r)
            def _(r=r):
                visit((n_rest - r + 1, 1), r, False)

        finish()


def _first_needed_chunk(qn2, kn2, ends, n_heads):
    nb, nt = ends.shape[0], ends.shape[1] // 8
    qn = jnp.sqrt(qn2.reshape(nb, nt, n_heads, LANES)[..., 0])
    kn = jnp.sqrt(kn2.reshape(nb, nt, n_heads, LANES)[..., 0])
    ends = ends.reshape(nb, nt, 8, LANES)
    c_first, c_last = ends[:, :, 0, :n_heads], ends[:, :, 1, :n_heads]
    ub = qn[:, :, None, :] * kn[:, None, :, :] + c_first[:, :, None, :] - c_last[:, None, :, :]
    lb = -(qn * kn)[:, :, None, :]
    i = jnp.arange(nt)[None, :, None, None]
    j = jnp.arange(nt)[None, None, :, None]
    needed = (ub - lb >= -PRUNE_GAP) & (j < i)
    first = jnp.min(jnp.where(needed, j, i), axis=2)
    return first.transpose(0, 2, 1).reshape(-1).astype(jnp.int32)


def _attn(first, q, qx, k, kx, vt, *, name):
    nb, length, w = q.shape
    n_heads = w // HEAD_DIM
    t = ATTN_TILE
    assert length % t == 0 and k.shape == (nb, n_heads, length, HEAD_DIM) and vt.shape == (nb, n_heads, VT_ROWS, length)
    blk = lambda b, h, i, first: (b, i, h)
    head = lambda b, h, i, first: (b, h, 0, 0)
    return pl.pallas_call(
        functools.partial(_attn_kernel, t=t),
        grid_spec=pltpu.PrefetchScalarGridSpec(
            num_scalar_prefetch=1,
            grid=(nb, n_heads, length // t),
            in_specs=[pl.BlockSpec((1, t, HEAD_DIM), blk), pl.BlockSpec((1, t, HEAD_DIM), blk),
                      pl.BlockSpec((1, 1, length, HEAD_DIM), head), pl.BlockSpec((1, 1, length, HEAD_DIM), head),
                      pl.BlockSpec((1, 1, VT_ROWS, length), head)],
            out_specs=pl.BlockSpec((1, t, HEAD_DIM), blk),
            scratch_shapes=[pltpu.VMEM((1, t), F32), pltpu.VMEM((VT_ROWS, t), F32),
                            pltpu.VMEM((t, t), F32), pltpu.VMEM((t, t), F32)],
        ),
        out_shape=jax.ShapeDtypeStruct((nb, length, w), BF16),
        compiler_params=_cparams(("parallel", "parallel", "arbitrary")),
        name=name,
    )(first, q, qx, k, kx, vt)


def _attn_cached_kernel(q_ref, qx_ref, kx_ref, ck_ref, cv_ref, kn_ref, vn_ref, o_ref, *, n_heads, past, tn):
    pad = LANES - tn
    causal = (lax.broadcasted_iota(jnp.int32, (tn, LANES), 1) <= lax.broadcasted_iota(jnp.int32, (tn, LANES), 0))
    nt = (((1,), (1,)), ((), ()))
    for hd in range(n_heads):
        cols = slice(hd * HEAD_DIM, (hd + 1) * HEAD_DIM)
        q = jnp.concatenate([q_ref[:, cols], qx_ref[0, :, cols]], axis=1)
        kc = jnp.concatenate([ck_ref[pl.ds(hd, past, stride=n_heads), :].astype(BF16),
                              kx_ref[0, hd, 0:past, :]], axis=1)
        kn = jnp.concatenate([kn_ref[:, cols], kx_ref[0, hd, past:past + tn, :]], axis=1)
        kn = jnp.concatenate([kn, jnp.zeros((pad, kn.shape[1]), BF16)], axis=0)
        s_c = lax.dot_general(q, kc, nt, preferred_element_type=F32)
        s_n = jnp.where(causal, lax.dot_general(q, kn, nt, preferred_element_type=F32), -jnp.inf)
        m = jnp.maximum(jnp.max(s_c, axis=1, keepdims=True), jnp.max(s_n, axis=1, keepdims=True))
        p_c = jnp.exp2(s_c - m)
        p_n = jnp.exp2(s_n - m)
        denom = jnp.sum(p_c, axis=1, keepdims=True) + jnp.sum(p_n, axis=1, keepdims=True)
        vc = cv_ref[pl.ds(hd, past, stride=n_heads), :].astype(BF16)
        vn = jnp.concatenate([vn_ref[:, cols], jnp.zeros((pad, HEAD_DIM), BF16)], axis=0)
        out = (jnp.dot(p_c.astype(BF16), vc, preferred_element_type=F32)
               + jnp.dot(p_n.astype(BF16), vn, preferred_element_type=F32))
        o_ref[:, cols] = (out / denom).astype(o_ref.dtype)


def _attn_cached(q, qx, kx, cache_k, cache_v, k_new, v_new, *, n_seq, past, name):
    n, w = q.shape
    tn = n // n_seq
    n_heads = w // HEAD_DIM
    assert tn <= LANES and tn % 16 == 0 and past % tn == 0 and kx.shape[2] >= past + tn
    ck = cache_k.reshape(n_seq * past * n_heads, HEAD_DIM)
    cv = cache_v.reshape(n_seq * past * n_heads, HEAD_DIM)
    rows = pl.BlockSpec((tn, w), lambda b: (b, 0))
    cached = pl.BlockSpec((past * n_heads, HEAD_DIM), lambda b: (b, 0))
    return pl.pallas_call(
        functools.partial(_attn_cached_kernel, n_heads=n_heads, past=past, tn=tn),
        grid=(n_seq,),
        in_specs=[
            rows,
            pl.BlockSpec((1, tn, w), lambda b: (b, past // tn, 0)),
            pl.BlockSpec((1,) + kx.shape[1:], lambda b: (b, 0, 0, 0)),
            cached, cached, rows, rows,
        ],
        out_specs=rows,
        out_shape=jax.ShapeDtypeStruct((n, w), BF16),
        compiler_params=_cparams(("parallel",)),
        name=name,
    )(q, qx, kx, ck, cv, k_new, v_new)


def _pool_kernel(p_ref, hist_ref, w_ref, sc_ref, o_ref, ext_ref, *, tp, pos0):
    t = pl.program_id(1)

    @pl.when(t == 0)
    def _():
        ext_ref[0:HIST_ROWS, :] = hist_ref[0]

    ext_ref[HIST_ROWS:, :] = p_ref[0]
    pos = pos0 + t * tp + lax.broadcasted_iota(jnp.int32, (tp, 1), 0)
    gd = w_ref.shape[1]
    for g, win in enumerate(POOL_WINDOWS):
        cols = slice(g * gd, (g + 1) * gd)
        cur = ext_ref[HIST_ROWS:, cols]
        tot = cur
        for back in range(1, win):
            tot = tot + ext_ref[HIST_ROWS - back:HIST_ROWS - back + tp, cols]
        cnt = jnp.minimum(pos + 1, win).astype(F32)
        d = (tot / cnt - cur).astype(BF16)
        y = jnp.dot(d, w_ref[g], preferred_element_type=F32) * sc_ref[:, cols]
        o_ref[0, :, cols] = y.astype(o_ref.dtype)
    ext_ref[0:HIST_ROWS, :] = ext_ref[tp:tp + HIST_ROWS, :]


def _pool(p, hist, w_pool, scale, *, pos0, name):
    nseq, length, w = p.shape
    tp = min(512, length)
    assert length % tp == 0 and tp >= HIST_ROWS
    return pl.pallas_call(
        functools.partial(_pool_kernel, tp=tp, pos0=pos0),
        grid=(nseq, length // tp),
        in_specs=[
            pl.BlockSpec((1, tp, w), lambda b, t: (b, t, 0)),
            pl.BlockSpec((1, HIST_ROWS, w), lambda b, t: (b, 0, 0)),
            pl.BlockSpec(w_pool.shape, lambda b, t: (0, 0, 0)),
            pl.BlockSpec((1, w), lambda b, t: (0, 0)),
        ],
        out_specs=pl.BlockSpec((1, tp, w), lambda b, t: (b, t, 0)),
        out_shape=jax.ShapeDtypeStruct((nseq, length, w), BF16),
        scratch_shapes=[pltpu.VMEM((HIST_ROWS + tp, w), F32)],
        compiler_params=_cparams(("parallel", "arbitrary")),
        name=name,
    )(p, hist, w_pool, scale)


def _oproj_kernel(h_ref, fox_ref, pool_ref, wo_ref, o_ref):
    mix = jnp.concatenate([fox_ref[...], pool_ref[...]], axis=1)
    o_ref[...] = h_ref[...] + jnp.dot(mix, wo_ref[...], preferred_element_type=F32)


def _oproj(h, fox, pool, wo, *, name):
    n, d = h.shape
    tm = min(512, n)
    assert n % tm == 0
    return pl.pallas_call(
        _oproj_kernel,
        grid=(n // tm,),
        in_specs=[
            pl.BlockSpec((tm, d), lambda i: (i, 0)),
            pl.BlockSpec((tm, fox.shape[1]), lambda i: (i, 0)),
            pl.BlockSpec((tm, pool.shape[1]), lambda i: (i, 0)),
            pl.BlockSpec(wo.shape, lambda i: (0, 0)),
        ],
        out_specs=pl.BlockSpec((tm, d), lambda i: (i, 0)),
        out_shape=jax.ShapeDtypeStruct((n, d), F32),
        compiler_params=_cparams(("parallel",)),
        name=name,
    )(h, fox, pool, wo)


def _round_up(x, m):
    return (x + m - 1) // m * m


def kernel(x_prompt, x_sample, cache_k, cache_v, cache_logf, state_pool, g_ffn1, w1_gate, w1_up, w1_down, g_mix, w_in, b_f, w_pool, pool_scale, w_o, g_ffn2, w2_gate, w2_up, w2_down, g_final):
    B, S, D = x_prompt.shape
    Bs, Tn, _ = x_sample.shape
    depth, _, P, H, Dh = cache_k.shape
    assert depth == 1 and Dh == HEAD_DIM
    fw = H * Dh
    pw = w_pool.shape[1] * w_pool.shape[2]
    assert w_in.shape[2] == 3 * fw + H + pw and state_pool.shape[2] == POOL_HIST

    row = lambda a: a.reshape(1, -1).astype(F32)
    w1g, w1u, w1d = w1_gate[0].astype(BF16), w1_up[0].astype(BF16), w1_down[0].astype(BF16)
    w2g, w2u, w2d = w2_gate[0].astype(BF16), w2_up[0].astype(BF16), w2_down[0].astype(BF16)
    wi = w_in[0]
    w_qkvp = jnp.concatenate([wi[:, :3 * fw], wi[:, 3 * fw + H:]], axis=1).astype(BF16)
    wf_cols = jnp.tile(wi[:, 3 * fw:3 * fw + H], (1, N_SPLIT))
    wf = jnp.pad(wf_cols, ((0, 0), (0, LANES - N_SPLIT * H))).astype(BF16)
    bf = jnp.pad(jnp.tile(b_f[0].astype(F32), N_SPLIT), (0, LANES - N_SPLIT * H)).reshape(1, LANES)
    wo = w_o[0].astype(BF16)
    wp = w_pool[0].astype(BF16)
    g1, gm, g2, gfin, psc = row(g_ffn1[0]), row(g_mix[0]), row(g_ffn2[0]), row(g_final), row(pool_scale[0])

    def pre(x2d, seq_len, tag):
        h = _ffn(x2d, g1, w1g, w1u, w1d, gfin, final_norm=False, name=f"ffn1_{tag}")
        return (h,) + tuple(_inproj(h, gm, w_qkvp, wf, bf, seq_len=seq_len, name=f"inproj_{tag}"))

    def post(h, fox, pool, tag):
        h2 = _oproj(h, fox, pool, wo, name=f"oproj_{tag}")
        return _ffn(h2, g2, w2g, w2u, w2d, gfin, final_norm=True, name=f"ffn2_{tag}")

    h, q, k, v, kh, vt, p, lf, qn2, kn2 = pre(x_prompt.reshape(B * S, D), S, "prompt")
    qx, kx, ends = _aug(lf.reshape(B, S, LANES), n_heads=H, tile_cap=ATTN_TILE, name="aug_prompt")
    first = _first_needed_chunk(qn2, kn2, ends, H)
    fox = _attn(first, q.reshape(B, S, fw), qx, kh, kx, vt, name="attn_prompt")
    p3 = p.reshape(B, S, pw)
    pool = _pool(p3, jnp.zeros((B, HIST_ROWS, pw), F32), wp, psc, pos0=0, name="pool_prompt")
    y_prompt = post(h, fox.reshape(B * S, fw), pool.reshape(B * S, pw), "prompt").reshape(B, S, D)
    k_prompt = k.reshape(1, B, S, H, Dh)
    v_prompt = v.reshape(1, B, S, H, Dh)
    logf_prompt = lf[:, :H].reshape(1, B, S, H)
    pool_prompt = p3[:, S - POOL_HIST:][None]

    h, q, k, v, kb, vb, p, lf, _, _ = pre(x_sample.reshape(Bs * Tn, D), None, "sample")
    L = _round_up(P + Tn, LANES)
    tail = L - P - Tn
    lf_cache = jnp.pad(jnp.tile(cache_logf[0].astype(F32), (1, 1, N_SPLIT)),
                       ((0, 0), (0, 0), (0, LANES - N_SPLIT * H)))
    lf_all = jnp.concatenate([lf_cache, lf.reshape(Bs, Tn, LANES), jnp.zeros((Bs, tail, LANES), F32)], axis=1)
    qx, kx, _ = _aug(lf_all, n_heads=H, tile_cap=L, name="aug_sample")
    fox = _attn_cached(q, qx, kx, cache_k[0], cache_v[0], kb, vb, n_seq=Bs, past=P, name="attn_sample")
    p3 = p.reshape(Bs, Tn, pw)
    hist = jnp.concatenate([jnp.zeros((Bs, HIST_ROWS - POOL_HIST, pw), F32), state_pool[0].astype(F32)], axis=1)
    pool = _pool(p3, hist, wp, psc, pos0=P, name="pool_sample")
    y_sample = post(h, fox, pool.reshape(Bs * Tn, pw), "sample").reshape(Bs, Tn, D)
    k_sample = k.reshape(1, Bs, Tn, H, Dh)
    v_sample = v.reshape(1, Bs, Tn, H, Dh)
    logf_sample = lf[:, :H].reshape(1, Bs, Tn, H)
    ext_tail = jnp.concatenate([state_pool[0].astype(F32), p3], axis=1)[:, Tn:]
    pool_sample = ext_tail[None]

    return (y_prompt, y_sample, k_prompt, v_prompt, logf_prompt, pool_prompt,
            k_sample, v_sample, logf_sample, pool_sample)
```

```python
import functools

import jax
import jax.numpy as jnp
from jax import lax
from jax.experimental import pallas as pl
from jax.experimental.pallas import tpu as pltpu

F32 = jnp.float32
BF16 = jnp.bfloat16

EPS = 1e-6
HEAD_DIM = 128
LANES = 128
POOL_WINDOWS = (2, 4, 8, 16)
POOL_HIST = max(POOL_WINDOWS) - 1
HIST_ROWS = 16
N_SPLIT = 3
VT_ROWS = HEAD_DIM + 16
LOG2E = 1.4426950408889634
VMEM_LIMIT = 56 * 1024 * 1024
FFN_VMEM_LIMIT = 62 * 1024 * 1024
ATTN_UNROLL = 4
ATTN_FLAT = 6
ATTN_TILE = 512
PRUNE_GAP = 170.0
FFN_ROWS = 512
FFN_COLS = 1024
FFN_SLOTS = 2
FFN_NORM_ROWS = 256


def _cparams(sem, vmem_limit=VMEM_LIMIT):
    return pltpu.CompilerParams(dimension_semantics=sem, vmem_limit_bytes=vmem_limit)


def _pick_tile(length, cap, mult):
    best = None
    for t in range(mult, min(cap, length) + 1, mult):
        if length % t == 0:
            best = t
    assert best is not None, (length, cap, mult)
    return best


def _rms(x, g):
    return x * lax.rsqrt(jnp.mean(x * x, axis=-1, keepdims=True) + EPS) * g


def _split3(x):
    hi = x.astype(BF16).astype(F32)
    r1 = x - hi
    mid = r1.astype(BF16).astype(F32)
    lo = (r1 - mid).astype(BF16).astype(F32)
    return hi, mid, lo


def _ffn_kernel(x_ref, g_ref, gf_ref, wg_hbm, wu_hbm, wd_hbm, o_ref, xn_ref, wg_buf, wu_buf, wd_buf, sem, *,
                final_norm, tf, dff):
    nf = pl.cdiv(dff, tf)
    last_width = dff - (nf - 1) * tf
    tile = pl.program_id(0)
    flip = (tile * (nf % 2)) % 2

    def copies(f, width, flip):
        slot = (f + flip) % 2
        cols = pl.ds(pl.multiple_of(f * tf, tf), width)
        return (pltpu.make_async_copy(wg_hbm.at[:, cols], wg_buf.at[slot, :, 0:width], sem.at[0, slot]),
                pltpu.make_async_copy(wu_hbm.at[:, cols], wu_buf.at[slot, :, 0:width], sem.at[1, slot]),
                pltpu.make_async_copy(wd_hbm.at[cols, :], wd_buf.at[slot, 0:width, :], sem.at[2, slot]))

    def start(f, width=tf, flip=flip):
        for c in copies(f, width, flip):
            c.start()

    def wait(f, width=tf):
        for c in copies(f, width, flip):
            c.wait()

    def chunk(f, xn, width=tf):
        slot = (f + flip) % 2
        gate = jnp.dot(xn, wg_buf[slot, :, 0:width], preferred_element_type=F32)
        up = jnp.dot(xn, wu_buf[slot, :, 0:width], preferred_element_type=F32)
        a = (gate * jax.nn.sigmoid(gate) * up).astype(BF16)
        return jnp.dot(a, wd_buf[slot, 0:width, :], preferred_element_type=F32)

    @pl.when(tile == 0)
    def _():
        start(0)

    wait(0)
    start(1)
    xn = _rms(x_ref[...], g_ref[...]).astype(BF16)
    xn_ref[...] = xn
    o_ref[...] = chunk(0, xn)

    def middle(f, carry):
        wait(f)
        start(f + 1)
        o_ref[...] += chunk(f, xn_ref[...])
        return carry

    lax.fori_loop(1, nf - 2, middle, 0)
    wait(nf - 2)
    start(nf - 1, last_width)
    o_ref[...] += chunk(nf - 2, xn_ref[...])
    wait(nf - 1, last_width)

    @pl.when(tile + 1 < pl.num_programs(0))
    def _():
        start(0, tf, ((tile + 1) * (nf % 2)) % 2)

    o_ref[...] = x_ref[...] + 0.5 * (o_ref[...] + chunk(nf - 1, xn_ref[...], last_width))
    if final_norm:
        rows = min(FFN_NORM_ROWS, o_ref.shape[0])
        for r in range(0, o_ref.shape[0], rows):
            o_ref[r:r + rows, :] = _rms(o_ref[r:r + rows, :], gf_ref[...])


def _ffn(x, g, wg, wu, wd, gf, *, final_norm, name):
    n, d = x.shape
    dff = wg.shape[1]
    tm = min(FFN_ROWS, n)
    tf = FFN_COLS
    assert n % tm == 0 and dff % LANES == 0 and pl.cdiv(dff, tf) >= 4
    return pl.pallas_call(
        functools.partial(_ffn_kernel, final_norm=final_norm, tf=tf, dff=dff),
        grid=(n // tm,),
        in_specs=[
            pl.BlockSpec((tm, d), lambda i: (i, 0)),
            pl.BlockSpec((1, d), lambda i: (0, 0)),
            pl.BlockSpec((1, d), lambda i: (0, 0)),
            pl.BlockSpec(memory_space=pl.ANY),
            pl.BlockSpec(memory_space=pl.ANY),
            pl.BlockSpec(memory_space=pl.ANY),
        ],
        out_specs=pl.BlockSpec((tm, d), lambda i: (i, 0)),
        out_shape=jax.ShapeDtypeStruct((n, d), F32),
        scratch_shapes=[pltpu.VMEM((tm, d), BF16),
                        pltpu.VMEM((FFN_SLOTS, d, tf), BF16), pltpu.VMEM((FFN_SLOTS, d, tf), BF16),
                        pltpu.VMEM((FFN_SLOTS, tf, d), BF16), pltpu.SemaphoreType.DMA((3, FFN_SLOTS))],
        compiler_params=_cparams(("arbitrary",), FFN_VMEM_LIMIT),
        name=name,
    )(x, g, gf, wg, wu, wd)


def _inproj_kernel(h_ref, g_ref, w_ref, wf_ref, bf_ref,
                   q_ref, k_ref, v_ref, ka_ref, va_ref, p_ref, lf_ref, qn_ref, kn_ref, *, scale, width, head_major):
    hn = _rms(h_ref[...], g_ref[...]).astype(BF16)
    n_heads = width // HEAD_DIM

    def proj(c):
        return jnp.dot(hn, w_ref[:, c * width:(c + 1) * width], preferred_element_type=F32)

    def store_heads(o_ref, u):
        for hd in range(n_heads):
            o_ref[pl.ds(hd, u.shape[0], stride=n_heads), :] = u[:, hd * HEAD_DIM:(hd + 1) * HEAD_DIM]

    def max_sq_norms(xb):
        sub = lax.broadcasted_iota(jnp.int32, (n_heads, LANES), 0)
        out = jnp.zeros((n_heads, LANES), F32)
        for hd in range(n_heads):
            x = xb[:, hd * HEAD_DIM:(hd + 1) * HEAD_DIM].astype(F32)
            top = jnp.max(jnp.sum(x * x, axis=1, keepdims=True), axis=0, keepdims=True)
            out = jnp.where(sub == hd, top, out)
        return out

    lf_ref[...] = jax.nn.log_sigmoid(jnp.dot(hn, wf_ref[...], preferred_element_type=F32) + bf_ref[...])
    qb = (proj(0) * scale).astype(BF16)
    q_ref[...] = qb
    qn_ref[...] = max_sq_norms(qb)
    u = proj(1)
    store_heads(k_ref, u)
    kb = u.astype(BF16)
    kn_ref[...] = max_sq_norms(kb)
    u = proj(2)
    store_heads(v_ref, u)
    vb = u.astype(BF16)
    if head_major:
        sub = lax.broadcasted_iota(jnp.int32, (VT_ROWS - HEAD_DIM, vb.shape[0]), 0)
        tail = jnp.where(sub == 0, 1.0, 0.0).astype(BF16)
        for hd in range(n_heads):
            cols = slice(hd * HEAD_DIM, (hd + 1) * HEAD_DIM)
            ka_ref[0, hd] = kb[:, cols]
            va_ref[0, hd, 0:HEAD_DIM, :] = vb[:, cols].T
            va_ref[0, hd, HEAD_DIM:, :] = tail
    else:
        ka_ref[...] = kb
        va_ref[...] = vb
    p_ref[...] = proj(3)


def _inproj(h, g, w_qkvp, wf, bf, *, seq_len, name):
    n, d = h.shape
    w = w_qkvp.shape[1] // 4
    tm = min(ATTN_TILE, n)
    assert n % tm == 0 and w % HEAD_DIM == 0
    n_heads = w // HEAD_DIM
    assert n_heads == 8
    row = lambda i: (i, 0)
    const = lambda i: (0, 0)
    resident = pl.Buffered(1)
    if seq_len is None:
        ka_spec = va_spec = pl.BlockSpec((tm, w), row)
        ka_shape = va_shape = jax.ShapeDtypeStruct((n, w), BF16)
    else:
        assert seq_len % tm == 0 and n % seq_len == 0
        tps = seq_len // tm
        ka_spec = pl.BlockSpec((1, n_heads, tm, HEAD_DIM), lambda i: (i // tps, 0, i % tps, 0))
        va_spec = pl.BlockSpec((1, n_heads, VT_ROWS, tm), lambda i: (i // tps, 0, 0, i % tps))
        ka_shape = jax.ShapeDtypeStruct((n // seq_len, n_heads, seq_len, HEAD_DIM), BF16)
        va_shape = jax.ShapeDtypeStruct((n // seq_len, n_heads, VT_ROWS, seq_len), BF16)
    return pl.pallas_call(
        functools.partial(_inproj_kernel, scale=HEAD_DIM ** -0.5 * LOG2E, width=w, head_major=seq_len is not None),
        grid=(n // tm,),
        in_specs=[
            pl.BlockSpec((tm, d), row),
            pl.BlockSpec((1, d), const),
            pl.BlockSpec((d, 4 * w), const, pipeline_mode=resident),
            pl.BlockSpec((d, LANES), const),
            pl.BlockSpec((1, LANES), const),
        ],
        out_specs=[
            pl.BlockSpec((tm, w), row),
            pl.BlockSpec((tm * n_heads, HEAD_DIM), row), pl.BlockSpec((tm * n_heads, HEAD_DIM), row),
            ka_spec, va_spec, pl.BlockSpec((tm, w), row),
            pl.BlockSpec((tm, LANES), row),
            pl.BlockSpec((n_heads, LANES), row), pl.BlockSpec((n_heads, LANES), row),
        ],
        out_shape=[
            jax.ShapeDtypeStruct((n, w), BF16),
            jax.ShapeDtypeStruct((n * n_heads, HEAD_DIM), F32),
            jax.ShapeDtypeStruct((n * n_heads, HEAD_DIM), F32),
            ka_shape,
            va_shape,
            jax.ShapeDtypeStruct((n, w), F32),
            jax.ShapeDtypeStruct((n, LANES), F32),
            jax.ShapeDtypeStruct((n // tm * n_heads, LANES), F32),
            jax.ShapeDtypeStruct((n // tm * n_heads, LANES), F32),
        ],
        compiler_params=_cparams(("parallel",)),
        name=name,
    )(h, g, w_qkvp, wf, bf)


def _aug_kernel(lf_ref, tri_ref, selq_ref, selk_ref, qx_ref, kx_ref, ends_ref, carry_ref, *, n_heads):
    t = pl.program_id(1)

    @pl.when(t == 0)
    def _():
        carry_ref[...] = jnp.zeros_like(carry_ref)

    lf = lf_ref[0]
    lane = lax.broadcasted_iota(jnp.int32, lf.shape, 1)

    def pick(parts, extra=None):
        out = jnp.zeros_like(lf) if extra is None else extra
        for j in reversed(range(N_SPLIT)):
            out = jnp.where((lane >= j * n_heads) & (lane < (j + 1) * n_heads), parts[j], out)
        return out

    cs = jnp.dot(tri_ref[...], pick(_split3(lf)).astype(BF16), preferred_element_type=F32)
    c = cs
    for j in range(1, N_SPLIT):
        c = c + pltpu.roll(cs, LANES - j * n_heads, axis=1)
    c = c + carry_ref[...]
    carry_ref[...] = c[c.shape[0] - 1:, :]

    crep = c
    for j in range(1, N_SPLIT):
        crep = jnp.where((lane >= j * n_heads) & (lane < (j + 1) * n_heads),
                         pltpu.roll(c, j * n_heads, axis=1), crep)
    ones = jnp.where(lane == N_SPLIT * n_heads, 1.0, 0.0).astype(F32)
    pieces = pick(_split3(crep * LOG2E), extra=ones).astype(BF16)
    qx_ref[0] = jnp.dot(pieces, selq_ref[...], preferred_element_type=F32).astype(BF16)
    kx = jnp.dot(pieces, selk_ref[...], preferred_element_type=F32).astype(BF16)
    for hd in range(n_heads):
        kx_ref[0, hd] = kx[:, hd * HEAD_DIM:(hd + 1) * HEAD_DIM]
    sub = lax.broadcasted_iota(jnp.int32, (8, LANES), 0)
    ends_ref[0] = jnp.where(sub == 0, c[0:1, :] * LOG2E, jnp.where(sub == 1, c[c.shape[0] - 1:, :] * LOG2E, 0.0))


def _aug_constants(tm, n_heads):
    r = jnp.arange(tm)
    tri = (r[:, None] >= r[None, :]).astype(BF16)
    rows = jnp.arange(LANES)[:, None]
    cols = jnp.arange(n_heads * HEAD_DIM)[None, :]
    head, lane = cols // HEAD_DIM, cols % HEAD_DIM
    one_row = N_SPLIT * n_heads
    is_piece = lambda off: (lane >= off) & (lane < off + N_SPLIT) & (rows == (lane - off) * n_heads + head)
    is_one = lambda off: (lane >= off) & (lane < off + N_SPLIT) & (rows == one_row)
    selq = (is_piece(0) | is_one(N_SPLIT)).astype(BF16)
    selk = is_one(0).astype(BF16) - is_piece(N_SPLIT).astype(BF16)
    return tri, selq, selk


def _aug(lf, *, n_heads, tile_cap, name):
    nseq, length, _ = lf.shape
    tm = _pick_tile(length, tile_cap, 16)
    assert (N_SPLIT * n_heads) < LANES
    w = n_heads * HEAD_DIM
    tri, selq, selk = _aug_constants(tm, n_heads)
    return pl.pallas_call(
        functools.partial(_aug_kernel, n_heads=n_heads),
        grid=(nseq, length // tm),
        in_specs=[
            pl.BlockSpec((1, tm, LANES), lambda b, t: (b, t, 0)),
            pl.BlockSpec((tm, tm), lambda b, t: (0, 0)),
            pl.BlockSpec((LANES, w), lambda b, t: (0, 0)),
            pl.BlockSpec((LANES, w), lambda b, t: (0, 0)),
        ],
        out_specs=[pl.BlockSpec((1, tm, w), lambda b, t: (b, t, 0)),
                   pl.BlockSpec((1, n_heads, tm, HEAD_DIM), lambda b, t: (b, 0, t, 0)),
                   pl.BlockSpec((1, 8, LANES), lambda b, t: (b, t, 0))],
        out_shape=[jax.ShapeDtypeStruct((nseq, length, w), BF16),
                   jax.ShapeDtypeStruct((nseq, n_heads, length, HEAD_DIM), BF16),
                   jax.ShapeDtypeStruct((nseq, length // tm * 8, LANES), F32)],
        scratch_shapes=[pltpu.VMEM((1, LANES), F32)],
        compiler_params=_cparams(("parallel", "arbitrary")),
        name=name,
    )(lf, tri, selq, selk)


def _attn_kernel(first_ref, q_ref, qx_ref, k_ref, kx_ref, vt_ref, o_ref, m_ref, acc_ref, s0_ref, s1_ref, *, t):
    b, hd, qi = pl.program_id(0), pl.program_id(1), pl.program_id(2)
    q = jnp.concatenate([q_ref[0], qx_ref[0]], axis=1)
    n_rest = qi - first_ref[(b * pl.num_programs(1) + hd) * pl.num_programs(2) + qi]
    m_ref[...] = jnp.full_like(m_ref, -jnp.inf)
    acc_ref[...] = jnp.zeros_like(acc_ref)

    def scores(j):
        st = pl.multiple_of(jnp.maximum(j, 0) * t, t)
        k = jnp.concatenate([k_ref[0, 0, pl.ds(st, t), :], kx_ref[0, 0, pl.ds(st, t), :]], axis=1)
        return lax.dot_general(k, q, (((1,), (1,)), ((), ())), preferred_element_type=F32)

    def accumulate(j, s, masked):
        st = pl.multiple_of(j * t, t)
        if masked:
            key_pos = lax.broadcasted_iota(jnp.int32, (t, t), 0)
            qry_pos = lax.broadcasted_iota(jnp.int32, (t, t), 1)
            s = jnp.where(key_pos <= qry_pos, s, -jnp.inf)
        m_prev = m_ref[...]
        m_new = jnp.maximum(m_prev, jnp.max(s, axis=0, keepdims=True))
        alpha = jnp.exp2(m_prev - m_new)
        p = jnp.exp2(s - m_new).astype(BF16)
        acc_ref[...] = alpha * acc_ref[...] + jnp.dot(vt_ref[0, 0, :, pl.ds(st, t)], p, preferred_element_type=F32)
        m_ref[...] = m_new

    bufs = (s0_ref, s1_ref)

    def visit(k0, n, prefetch_after):
        for u in range(n):
            k = k0[0] + u
            if u + 1 < n or prefetch_after:
                bufs[(k0[1] + u + 1) % 2][...] = scores(qi - k - 1)
            accumulate(qi - k, bufs[(k0[1] + u) % 2][...], False)

    def finish():
        acc = acc_ref[...]
        out_t = acc[0:HEAD_DIM] / acc[HEAD_DIM:HEAD_DIM + 1]
        o_ref[0] = out_t.T.astype(o_ref.dtype)

    for r in range(ATTN_FLAT):
        @pl.when(n_rest == r)
        def _(r=r):
            s_cur = scores(qi)
            for u in range(r + 1):
                s_next = scores(qi - u - 1) if u < r else None
                accumulate(qi - u, s_cur, u == 0)
                s_cur = s_next
            finish()

    @pl.when(n_rest >= ATTN_FLAT)
    def _():
        s0_ref[...] = scores(qi)
        s1_ref[...] = scores(qi - 1)
        accumulate(qi, s0_ref[...], True)

        def group(g, carry):
            visit((1 + ATTN_UNROLL * g, 1), ATTN_UNROLL, True)
            return carry

        lax.fori_loop(0, n_rest // ATTN_UNROLL, group, 0)
        rem = n_rest % ATTN_UNROLL
        for r in range(1, ATTN_UNROLL):
            @pl.when(rem == r)
            def _(r=r):
                visit((n_rest - r + 1, 1), r, False)

        finish()


def _first_needed_chunk(qn2, kn2, ends, n_heads):
    nb, nt = ends.shape[0], ends.shape[1] // 8
    qn = jnp.sqrt(qn2.reshape(nb, nt, n_heads, LANES)[..., 0])
    kn = jnp.sqrt(kn2.reshape(nb, nt, n_heads, LANES)[..., 0])
    ends = ends.reshape(nb, nt, 8, LANES)
    c_first, c_last = ends[:, :, 0, :n_heads], ends[:, :, 1, :n_heads]
    ub = qn[:, :, None, :] * kn[:, None, :, :] + c_first[:, :, None, :] - c_last[:, None, :, :]
    lb = -(qn * kn)[:, :, None, :]
    i = jnp.arange(nt)[None, :, None, None]
    j = jnp.arange(nt)[None, None, :, None]
    needed = (ub - lb >= -PRUNE_GAP) & (j < i)
    first = jnp.min(jnp.where(needed, j, i), axis=2)
    return first.transpose(0, 2, 1).reshape(-1).astype(jnp.int32)


def _attn(first, q, qx, k, kx, vt, *, name):
    nb, length, w = q.shape
    n_heads = w // HEAD_DIM
    t = ATTN_TILE
    assert length % t == 0 and k.shape == (nb, n_heads, length, HEAD_DIM) and vt.shape == (nb, n_heads, VT_ROWS, length)
    blk = lambda b, h, i, first: (b, i, h)
    head = lambda b, h, i, first: (b, h, 0, 0)
    return pl.pallas_call(
        functools.partial(_attn_kernel, t=t),
        grid_spec=pltpu.PrefetchScalarGridSpec(
            num_scalar_prefetch=1,
            grid=(nb, n_heads, length // t),
            in_specs=[pl.BlockSpec((1, t, HEAD_DIM), blk), pl.BlockSpec((1, t, HEAD_DIM), blk),
                      pl.BlockSpec((1, 1, length, HEAD_DIM), head), pl.BlockSpec((1, 1, length, HEAD_DIM), head),
                      pl.BlockSpec((1, 1, VT_ROWS, length), head)],
            out_specs=pl.BlockSpec((1, t, HEAD_DIM), blk),
            scratch_shapes=[pltpu.VMEM((1, t), F32), pltpu.VMEM((VT_ROWS, t), F32),
                            pltpu.VMEM((t, t), F32), pltpu.VMEM((t, t), F32)],
        ),
        out_shape=jax.ShapeDtypeStruct((nb, length, w), BF16),
        compiler_params=_cparams(("parallel", "parallel", "arbitrary")),
        name=name,
    )(first, q, qx, k, kx, vt)


def _attn_cached_kernel(q_ref, qx_ref, kx_ref, ck_ref, cv_ref, kn_ref, vn_ref, o_ref, *, n_heads, past, tn):
    pad = LANES - tn
    causal = (lax.broadcasted_iota(jnp.int32, (tn, LANES), 1) <= lax.broadcasted_iota(jnp.int32, (tn, LANES), 0))
    nt = (((1,), (1,)), ((), ()))
    for hd in range(n_heads):
        cols = slice(hd * HEAD_DIM, (hd + 1) * HEAD_DIM)
        q = jnp.concatenate([q_ref[:, cols], qx_ref[0, :, cols]], axis=1)
        kc = jnp.concatenate([ck_ref[pl.ds(hd, past, stride=n_heads), :].astype(BF16),
                              kx_ref[0, hd, 0:past, :]], axis=1)
        kn = jnp.concatenate([kn_ref[:, cols], kx_ref[0, hd, past:past + tn, :]], axis=1)
        kn = jnp.concatenate([kn, jnp.zeros((pad, kn.shape[1]), BF16)], axis=0)
        s_c = lax.dot_general(q, kc, nt, preferred_element_type=F32)
        s_n = jnp.where(causal, lax.dot_general(q, kn, nt, preferred_element_type=F32), -jnp.inf)
        m = jnp.maximum(jnp.max(s_c, axis=1, keepdims=True), jnp.max(s_n, axis=1, keepdims=True))
        p_c = jnp.exp2(s_c - m)
        p_n = jnp.exp2(s_n - m)
        denom = jnp.sum(p_c, axis=1, keepdims=True) + jnp.sum(p_n, axis=1, keepdims=True)
        vc = cv_ref[pl.ds(hd, past, stride=n_heads), :].astype(BF16)
        vn = jnp.concatenate([vn_ref[:, cols], jnp.zeros((pad, HEAD_DIM), BF16)], axis=0)
        out = (jnp.dot(p_c.astype(BF16), vc, preferred_element_type=F32)
               + jnp.dot(p_n.astype(BF16), vn, preferred_element_type=F32))
        o_ref[:, cols] = (out / denom).astype(o_ref.dtype)


def _attn_cached(q, qx, kx, cache_k, cache_v, k_new, v_new, *, n_seq, past, name):
    n, w = q.shape
    tn = n // n_seq
    n_heads = w // HEAD_DIM
    assert tn <= LANES and tn % 16 == 0 and past % tn == 0 and kx.shape[2] >= past + tn
    ck = cache_k.reshape(n_seq * past * n_heads, HEAD_DIM)
    cv = cache_v.reshape(n_seq * past * n_heads, HEAD_DIM)
    rows = pl.BlockSpec((tn, w), lambda b: (b, 0))
    cached = pl.BlockSpec((past * n_heads, HEAD_DIM), lambda b: (b, 0))
    return pl.pallas_call(
        functools.partial(_attn_cached_kernel, n_heads=n_heads, past=past, tn=tn),
        grid=(n_seq,),
        in_specs=[
            rows,
            pl.BlockSpec((1, tn, w), lambda b: (b, past // tn, 0)),
            pl.BlockSpec((1,) + kx.shape[1:], lambda b: (b, 0, 0, 0)),
            cached, cached, rows, rows,
        ],
        out_specs=rows,
        out_shape=jax.ShapeDtypeStruct((n, w), BF16),
        compiler_params=_cparams(("parallel",)),
        name=name,
    )(q, qx, kx, ck, cv, k_new, v_new)


def _pool_kernel(p_ref, hist_ref, w_ref, sc_ref, o_ref, ext_ref, *, tp, pos0):
    t = pl.program_id(1)

    @pl.when(t == 0)
    def _():
        ext_ref[0:HIST_ROWS, :] = hist_ref[0]

    ext_ref[HIST_ROWS:, :] = p_ref[0]
    pos = pos0 + t * tp + lax.broadcasted_iota(jnp.int32, (tp, 1), 0)
    gd = w_ref.shape[1]
    for g, win in enumerate(POOL_WINDOWS):
        cols = slice(g * gd, (g + 1) * gd)
        cur = ext_ref[HIST_ROWS:, cols]
        tot = cur
        for back in range(1, win):
            tot = tot + ext_ref[HIST_ROWS - back:HIST_ROWS - back + tp, cols]
        cnt = jnp.minimum(pos + 1, win).astype(F32)
        d = (tot / cnt - cur).astype(BF16)
        y = jnp.dot(d, w_ref[g], preferred_element_type=F32) * sc_ref[:, cols]
        o_ref[0, :, cols] = y.astype(o_ref.dtype)
    ext_ref[0:HIST_ROWS, :] = ext_ref[tp:tp + HIST_ROWS, :]


def _pool(p, hist, w_pool, scale, *, pos0, name):
    nseq, length, w = p.shape
    tp = min(512, length)
    assert length % tp == 0 and tp >= HIST_ROWS
    return pl.pallas_call(
        functools.partial(_pool_kernel, tp=tp, pos0=pos0),
        grid=(nseq, length // tp),
        in_specs=[
            pl.BlockSpec((1, tp, w), lambda b, t: (b, t, 0)),
            pl.BlockSpec((1, HIST_ROWS, w), lambda b, t: (b, 0, 0)),
            pl.BlockSpec(w_pool.shape, lambda b, t: (0, 0, 0)),
            pl.BlockSpec((1, w), lambda b, t: (0, 0)),
        ],
        out_specs=pl.BlockSpec((1, tp, w), lambda b, t: (b, t, 0)),
        out_shape=jax.ShapeDtypeStruct((nseq, length, w), BF16),
        scratch_shapes=[pltpu.VMEM((HIST_ROWS + tp, w), F32)],
        compiler_params=_cparams(("parallel", "arbitrary")),
        name=name,
    )(p, hist, w_pool, scale)


def _oproj_kernel(h_ref, fox_ref, pool_ref, wo_ref, o_ref):
    mix = jnp.concatenate([fox_ref[...], pool_ref[...]], axis=1)
    o_ref[...] = h_ref[...] + jnp.dot(mix, wo_ref[...], preferred_element_type=F32)


def _oproj(h, fox, pool, wo, *, name):
    n, d = h.shape
    tm = min(512, n)
    assert n % tm == 0
    return pl.pallas_call(
        _oproj_kernel,
        grid=(n // tm,),
        in_specs=[
            pl.BlockSpec((tm, d), lambda i: (i, 0)),
            pl.BlockSpec((tm, fox.shape[1]), lambda i: (i, 0)),
            pl.BlockSpec((tm, pool.shape[1]), lambda i: (i, 0)),
            pl.BlockSpec(wo.shape, lambda i: (0, 0)),
        ],
        out_specs=pl.BlockSpec((tm, d), lambda i: (i, 0)),
        out_shape=jax.ShapeDtypeStruct((n, d), F32),
        compiler_params=_cparams(("parallel",)),
        name=name,
    )(h, fox, pool, wo)


def _round_up(x, m):
    return (x + m - 1) // m * m


def kernel(x_prompt, x_sample, cache_k, cache_v, cache_logf, state_pool, g_ffn1, w1_gate, w1_up, w1_down, g_mix, w_in, b_f, w_pool, pool_scale, w_o, g_ffn2, w2_gate, w2_up, w2_down, g_final):
    B, S, D = x_prompt.shape
    Bs, Tn, _ = x_sample.shape
    depth, _, P, H, Dh = cache_k.shape
    assert depth == 1 and Dh == HEAD_DIM
    fw = H * Dh
    pw = w_pool.shape[1] * w_pool.shape[2]
    assert w_in.shape[2] == 3 * fw + H + pw and state_pool.shape[2] == POOL_HIST

    row = lambda a: a.reshape(1, -1).astype(F32)
    w1g, w1u, w1d = w1_gate[0].astype(BF16), w1_up[0].astype(BF16), w1_down[0].astype(BF16)
    w2g, w2u, w2d = w2_gate[0].astype(BF16), w2_up[0].astype(BF16), w2_down[0].astype(BF16)
    wi = w_in[0]
    w_qkvp = jnp.concatenate([wi[:, :3 * fw], wi[:, 3 * fw + H:]], axis=1).astype(BF16)
    wf_cols = jnp.tile(wi[:, 3 * fw:3 * fw + H], (1, N_SPLIT))
    wf = jnp.pad(wf_cols, ((0, 0), (0, LANES - N_SPLIT * H))).astype(BF16)
    bf = jnp.pad(jnp.tile(b_f[0].astype(F32), N_SPLIT), (0, LANES - N_SPLIT * H)).reshape(1, LANES)
    wo = w_o[0].astype(BF16)
    wp = w_pool[0].astype(BF16)
    g1, gm, g2, gfin, psc = row(g_ffn1[0]), row(g_mix[0]), row(g_ffn2[0]), row(g_final), row(pool_scale[0])

    def pre(x2d, seq_len, tag):
        h = _ffn(x2d, g1, w1g, w1u, w1d, gfin, final_norm=False, name=f"ffn1_{tag}")
        return (h,) + tuple(_inproj(h, gm, w_qkvp, wf, bf, seq_len=seq_len, name=f"inproj_{tag}"))

    def post(h, fox, pool, tag):
        h2 = _oproj(h, fox, pool, wo, name=f"oproj_{tag}")
        return _ffn(h2, g2, w2g, w2u, w2d, gfin, final_norm=True, name=f"ffn2_{tag}")

    h, q, k, v, kh, vt, p, lf, qn2, kn2 = pre(x_prompt.reshape(B * S, D), S, "prompt")
    qx, kx, ends = _aug(lf.reshape(B, S, LANES), n_heads=H, tile_cap=ATTN_TILE, name="aug_prompt")
    first = _first_needed_chunk(qn2, kn2, ends, H)
    fox = _attn(first, q.reshape(B, S, fw), qx, kh, kx, vt, name="attn_prompt")
    p3 = p.reshape(B, S, pw)
    pool = _pool(p3, jnp.zeros((B, HIST_ROWS, pw), F32), wp, psc, pos0=0, name="pool_prompt")
    y_prompt = post(h, fox.reshape(B * S, fw), pool.reshape(B * S, pw), "prompt").reshape(B, S, D)
    k_prompt = k.reshape(1, B, S, H, Dh)
    v_prompt = v.reshape(1, B, S, H, Dh)
    logf_prompt = lf[:, :H].reshape(1, B, S, H)
    pool_prompt = p3[:, S - POOL_HIST:][None]

    h, q, k, v, kb, vb, p, lf, _, _ = pre(x_sample.reshape(Bs * Tn, D), None, "sample")
    L = _round_up(P + Tn, LANES)
    tail = L - P - Tn
    lf_cache = jnp.pad(jnp.tile(cache_logf[0].astype(F32), (1, 1, N_SPLIT)),
                       ((0, 0), (0, 0), (0, LANES - N_SPLIT * H)))
    lf_all = jnp.concatenate([lf_cache, lf.reshape(Bs, Tn, LANES), jnp.zeros((Bs, tail, LANES), F32)], axis=1)
    qx, kx, _ = _aug(lf_all, n_heads=H, tile_cap=L, name="aug_sample")
    fox = _attn_cached(q, qx, kx, cache_k[0], cache_v[0], kb, vb, n_seq=Bs, past=P, name="attn_sample")
    p3 = p.reshape(Bs, Tn, pw)
    hist = jnp.concatenate([jnp.zeros((Bs, HIST_ROWS - POOL_HIST, pw), F32), state_pool[0].astype(F32)], axis=1)
    pool = _pool(p3, hist, wp, psc, pos0=P, name="pool_sample")
    y_sample = post(h, fox, pool.reshape(Bs * Tn, pw), "sample").reshape(Bs, Tn, D)
    k_sample = k.reshape(1, Bs, Tn, H, Dh)
    v_sample = v.reshape(1, Bs, Tn, H, Dh)
    logf_sample = lf[:, :H].reshape(1, Bs, Tn, H)
    ext_tail = jnp.concatenate([state_pool[0].astype(F32), p3], axis=1)[:, Tn:]
    pool_sample = ext_tail[None]

    return (y_prompt, y_sample, k_prompt, v_prompt, logf_prompt, pool_prompt,
            k_sample, v_sample, logf_sample, pool_sample)
```

```python
import functools

import jax
import jax.numpy as jnp
from jax import lax
from jax.experimental import pallas as pl
from jax.experimental.pallas import tpu as pltpu

F32 = jnp.float32
BF16 = jnp.bfloat16

EPS = 1e-6
HEAD_DIM = 128
LANES = 128
MXU_COLS = 256
POOL_WINDOWS = (2, 4, 8, 16)
POOL_HIST = max(POOL_WINDOWS) - 1
HIST_ROWS = 16
N_SPLIT = 3
VT_ROWS = HEAD_DIM + 16
LOG2E = 1.4426950408889634
VMEM_LIMIT = 56 * 1024 * 1024
FFN_VMEM_LIMIT = 62 * 1024 * 1024
ATTN_UNROLL = 4
ATTN_FLAT = 6
ATTN_TILE = 512
PRUNE_GAP = 170.0
FFN_ROWS = 1024
FFN_COLS = 512
FFN_SLOTS = 2
FFN_NORM_ROWS = 256


def _cparams(sem, vmem_limit=VMEM_LIMIT):
    return pltpu.CompilerParams(dimension_semantics=sem, vmem_limit_bytes=vmem_limit)


def _pick_tile(length, cap, mult):
    best = None
    for t in range(mult, min(cap, length) + 1, mult):
        if length % t == 0:
            best = t
    assert best is not None, (length, cap, mult)
    return best


def _rms(x, g):
    return x * lax.rsqrt(jnp.mean(x * x, axis=-1, keepdims=True) + EPS) * g


def _split3(x):
    hi = x.astype(BF16).astype(F32)
    r1 = x - hi
    mid = r1.astype(BF16).astype(F32)
    lo = (r1 - mid).astype(BF16).astype(F32)
    return hi, mid, lo


def _ffn_kernel(x_ref, g_ref, gf_ref, wg_hbm, wu_hbm, wd_hbm, o_ref, xn_ref, wgu_buf, wd_buf, sem, *,
                final_norm, tf, nf):
    tile = pl.program_id(0)
    flip = (tile * (nf % 2)) % 2

    nblk = tf // MXU_COLS

    def copies(f, flip):
        slot = (f + flip) % 2
        out = []
        for j in range(nblk):
            src = pl.ds(pl.multiple_of(f * tf + j * MXU_COLS, MXU_COLS), MXU_COLS)
            for which, w_hbm in enumerate((wg_hbm, wu_hbm)):
                dst = wgu_buf.at[slot, :, pl.ds((2 * j + which) * MXU_COLS, MXU_COLS)]
                out.append(pltpu.make_async_copy(w_hbm.at[:, src], dst, sem.at[2 * j + which, slot]))
        rows = pl.ds(pl.multiple_of(f * tf, tf), tf)
        out.append(pltpu.make_async_copy(wd_hbm.at[rows, :], wd_buf.at[slot], sem.at[2 * nblk, slot]))
        return out

    def start(f, flip=flip):
        for c in copies(f, flip):
            c.start()

    def wait(f):
        for c in copies(f, flip):
            c.wait()

    def chunk(f, xn):
        slot = (f + flip) % 2
        gu = jnp.dot(xn, wgu_buf[slot], preferred_element_type=F32)
        blocks = [gu[:, j * MXU_COLS:(j + 1) * MXU_COLS] for j in range(2 * nblk)]
        gate = jnp.concatenate(blocks[0::2], axis=1)
        up = jnp.concatenate(blocks[1::2], axis=1)
        a = (gate * jax.nn.sigmoid(gate) * up).astype(BF16)
        return jnp.dot(a, wd_buf[slot], preferred_element_type=F32)

    @pl.when(tile == 0)
    def _():
        start(0)

    wait(0)
    start(1)
    xn = _rms(x_ref[...], g_ref[...]).astype(BF16)
    xn_ref[...] = xn
    o_ref[...] = chunk(0, xn)

    def middle(f, carry):
        wait(f)
        start(f + 1)
        o_ref[...] += chunk(f, xn_ref[...])
        return carry

    lax.fori_loop(1, nf - 1, middle, 0)
    wait(nf - 1)

    @pl.when(tile + 1 < pl.num_programs(0))
    def _():
        start(0, ((tile + 1) * (nf % 2)) % 2)

    o_ref[...] = x_ref[...] + 0.5 * (o_ref[...] + chunk(nf - 1, xn_ref[...]))
    if final_norm:
        rows = min(FFN_NORM_ROWS, o_ref.shape[0])
        for r in range(0, o_ref.shape[0], rows):
            o_ref[r:r + rows, :] = _rms(o_ref[r:r + rows, :], gf_ref[...])


def _ffn(x, g, wg, wu, wd, gf, *, final_norm, name):
    n, d = x.shape
    dff = wg.shape[1]
    tm = min(FFN_ROWS, n)
    tf = FFN_COLS
    nf = dff // tf
    assert n % tm == 0 and dff % tf == 0 and nf >= 3 and tf % MXU_COLS == 0
    return pl.pallas_call(
        functools.partial(_ffn_kernel, final_norm=final_norm, tf=tf, nf=nf),
        grid=(n // tm,),
        in_specs=[
            pl.BlockSpec((tm, d), lambda i: (i, 0)),
            pl.BlockSpec((1, d), lambda i: (0, 0)),
            pl.BlockSpec((1, d), lambda i: (0, 0)),
            pl.BlockSpec(memory_space=pl.ANY),
            pl.BlockSpec(memory_space=pl.ANY),
            pl.BlockSpec(memory_space=pl.ANY),
        ],
        out_specs=pl.BlockSpec((tm, d), lambda i: (i, 0)),
        out_shape=jax.ShapeDtypeStruct((n, d), F32),
        scratch_shapes=[pltpu.VMEM((tm, d), BF16),
                        pltpu.VMEM((FFN_SLOTS, d, 2 * tf), BF16), pltpu.VMEM((FFN_SLOTS, tf, d), BF16),
                        pltpu.SemaphoreType.DMA((2 * (tf // MXU_COLS) + 1, FFN_SLOTS))],
        compiler_params=_cparams(("arbitrary",), FFN_VMEM_LIMIT),
        name=name,
    )(x, g, gf, wg, wu, wd)


def _inproj_kernel(h_ref, g_ref, w_ref, wf_ref, bf_ref,
                   q_ref, k_ref, v_ref, ka_ref, va_ref, p_ref, lf_ref, qn_ref, kn_ref, *, scale, width, head_major):
    hn = _rms(h_ref[...], g_ref[...]).astype(BF16)
    n_heads = width // HEAD_DIM

    def proj(c):
        return jnp.dot(hn, w_ref[:, c * width:(c + 1) * width], preferred_element_type=F32)

    def store_heads(o_ref, u):
        for hd in range(n_heads):
            o_ref[pl.ds(hd, u.shape[0], stride=n_heads), :] = u[:, hd * HEAD_DIM:(hd + 1) * HEAD_DIM]

    def max_sq_norms(xb):
        sub = lax.broadcasted_iota(jnp.int32, (n_heads, LANES), 0)
        out = jnp.zeros((n_heads, LANES), F32)
        for hd in range(n_heads):
            x = xb[:, hd * HEAD_DIM:(hd + 1) * HEAD_DIM].astype(F32)
            top = jnp.max(jnp.sum(x * x, axis=1, keepdims=True), axis=0, keepdims=True)
            out = jnp.where(sub == hd, top, out)
        return out

    lf_ref[...] = jax.nn.log_sigmoid(jnp.dot(hn, wf_ref[...], preferred_element_type=F32) + bf_ref[...])
    qb = (proj(0) * scale).astype(BF16)
    q_ref[...] = qb
    qn_ref[...] = max_sq_norms(qb)
    u = proj(1)
    store_heads(k_ref, u)
    kb = u.astype(BF16)
    kn_ref[...] = max_sq_norms(kb)
    u = proj(2)
    store_heads(v_ref, u)
    vb = u.astype(BF16)
    if head_major:
        sub = lax.broadcasted_iota(jnp.int32, (VT_ROWS - HEAD_DIM, vb.shape[0]), 0)
        tail = jnp.where(sub == 0, 1.0, 0.0).astype(BF16)
        for hd in range(n_heads):
            cols = slice(hd * HEAD_DIM, (hd + 1) * HEAD_DIM)
            ka_ref[0, hd] = kb[:, cols]
            va_ref[0, hd, 0:HEAD_DIM, :] = vb[:, cols].T
            va_ref[0, hd, HEAD_DIM:, :] = tail
    else:
        ka_ref[...] = kb
        va_ref[...] = vb
    p_ref[...] = proj(3)


def _inproj(h, g, w_qkvp, wf, bf, *, seq_len, name):
    n, d = h.shape
    w = w_qkvp.shape[1] // 4
    tm = min(ATTN_TILE, n)
    assert n % tm == 0 and w % HEAD_DIM == 0
    n_heads = w // HEAD_DIM
    assert n_heads == 8
    row = lambda i: (i, 0)
    const = lambda i: (0, 0)
    resident = pl.Buffered(1)
    if seq_len is None:
        ka_spec = va_spec = pl.BlockSpec((tm, w), row)
        ka_shape = va_shape = jax.ShapeDtypeStruct((n, w), BF16)
    else:
        assert seq_len % tm == 0 and n % seq_len == 0
        tps = seq_len // tm
        ka_spec = pl.BlockSpec((1, n_heads, tm, HEAD_DIM), lambda i: (i // tps, 0, i % tps, 0))
        va_spec = pl.BlockSpec((1, n_heads, VT_ROWS, tm), lambda i: (i // tps, 0, 0, i % tps))
        ka_shape = jax.ShapeDtypeStruct((n // seq_len, n_heads, seq_len, HEAD_DIM), BF16)
        va_shape = jax.ShapeDtypeStruct((n // seq_len, n_heads, VT_ROWS, seq_len), BF16)
    return pl.pallas_call(
        functools.partial(_inproj_kernel, scale=HEAD_DIM ** -0.5 * LOG2E, width=w, head_major=seq_len is not None),
        grid=(n // tm,),
        in_specs=[
            pl.BlockSpec((tm, d), row),
            pl.BlockSpec((1, d), const),
            pl.BlockSpec((d, 4 * w), const, pipeline_mode=resident),
            pl.BlockSpec((d, LANES), const),
            pl.BlockSpec((1, LANES), const),
        ],
        out_specs=[
            pl.BlockSpec((tm, w), row),
            pl.BlockSpec((tm * n_heads, HEAD_DIM), row), pl.BlockSpec((tm * n_heads, HEAD_DIM), row),
            ka_spec, va_spec, pl.BlockSpec((tm, w), row),
            pl.BlockSpec((tm, LANES), row),
            pl.BlockSpec((n_heads, LANES), row), pl.BlockSpec((n_heads, LANES), row),
        ],
        out_shape=[
            jax.ShapeDtypeStruct((n, w), BF16),
            jax.ShapeDtypeStruct((n * n_heads, HEAD_DIM), F32),
            jax.ShapeDtypeStruct((n * n_heads, HEAD_DIM), F32),
            ka_shape,
            va_shape,
            jax.ShapeDtypeStruct((n, w), F32),
            jax.ShapeDtypeStruct((n, LANES), F32),
            jax.ShapeDtypeStruct((n // tm * n_heads, LANES), F32),
            jax.ShapeDtypeStruct((n // tm * n_heads, LANES), F32),
        ],
        compiler_params=_cparams(("parallel",)),
        name=name,
    )(h, g, w_qkvp, wf, bf)


def _aug_kernel(lf_ref, tri_ref, selq_ref, selk_ref, qx_ref, kx_ref, ends_ref, carry_ref, *, n_heads):
    t = pl.program_id(1)

    @pl.when(t == 0)
    def _():
        carry_ref[...] = jnp.zeros_like(carry_ref)

    lf = lf_ref[0]
    lane = lax.broadcasted_iota(jnp.int32, lf.shape, 1)

    def pick(parts, extra=None):
        out = jnp.zeros_like(lf) if extra is None else extra
        for j in reversed(range(N_SPLIT)):
            out = jnp.where((lane >= j * n_heads) & (lane < (j + 1) * n_heads), parts[j], out)
        return out

    cs = jnp.dot(tri_ref[...], pick(_split3(lf)).astype(BF16), preferred_element_type=F32)
    c = cs
    for j in range(1, N_SPLIT):
        c = c + pltpu.roll(cs, LANES - j * n_heads, axis=1)
    c = c + carry_ref[...]
    carry_ref[...] = c[c.shape[0] - 1:, :]

    crep = c
    for j in range(1, N_SPLIT):
        crep = jnp.where((lane >= j * n_heads) & (lane < (j + 1) * n_heads),
                         pltpu.roll(c, j * n_heads, axis=1), crep)
    ones = jnp.where(lane == N_SPLIT * n_heads, 1.0, 0.0).astype(F32)
    pieces = pick(_split3(crep * LOG2E), extra=ones).astype(BF16)
    qx_ref[0] = jnp.dot(pieces, selq_ref[...], preferred_element_type=F32).astype(BF16)
    kx = jnp.dot(pieces, selk_ref[...], preferred_element_type=F32).astype(BF16)
    for hd in range(n_heads):
        kx_ref[0, hd] = kx[:, hd * HEAD_DIM:(hd + 1) * HEAD_DIM]
    sub = lax.broadcasted_iota(jnp.int32, (8, LANES), 0)
    ends_ref[0] = jnp.where(sub == 0, c[0:1, :] * LOG2E, jnp.where(sub == 1, c[c.shape[0] - 1:, :] * LOG2E, 0.0))


def _aug_constants(tm, n_heads):
    r = jnp.arange(tm)
    tri = (r[:, None] >= r[None, :]).astype(BF16)
    rows = jnp.arange(LANES)[:, None]
    cols = jnp.arange(n_heads * HEAD_DIM)[None, :]
    head, lane = cols // HEAD_DIM, cols % HEAD_DIM
    one_row = N_SPLIT * n_heads
    is_piece = lambda off: (lane >= off) & (lane < off + N_SPLIT) & (rows == (lane - off) * n_heads + head)
    is_one = lambda off: (lane >= off) & (lane < off + N_SPLIT) & (rows == one_row)
    selq = (is_piece(0) | is_one(N_SPLIT)).astype(BF16)
    selk = is_one(0).astype(BF16) - is_piece(N_SPLIT).astype(BF16)
    return tri, selq, selk


def _aug(lf, *, n_heads, tile_cap, name):
    nseq, length, _ = lf.shape
    tm = _pick_tile(length, tile_cap, 16)
    assert (N_SPLIT * n_heads) < LANES
    w = n_heads * HEAD_DIM
    tri, selq, selk = _aug_constants(tm, n_heads)
    return pl.pallas_call(
        functools.partial(_aug_kernel, n_heads=n_heads),
        grid=(nseq, length // tm),
        in_specs=[
            pl.BlockSpec((1, tm, LANES), lambda b, t: (b, t, 0)),
            pl.BlockSpec((tm, tm), lambda b, t: (0, 0)),
            pl.BlockSpec((LANES, w), lambda b, t: (0, 0)),
            pl.BlockSpec((LANES, w), lambda b, t: (0, 0)),
        ],
        out_specs=[pl.BlockSpec((1, tm, w), lambda b, t: (b, t, 0)),
                   pl.BlockSpec((1, n_heads, tm, HEAD_DIM), lambda b, t: (b, 0, t, 0)),
                   pl.BlockSpec((1, 8, LANES), lambda b, t: (b, t, 0))],
        out_shape=[jax.ShapeDtypeStruct((nseq, length, w), BF16),
                   jax.ShapeDtypeStruct((nseq, n_heads, length, HEAD_DIM), BF16),
                   jax.ShapeDtypeStruct((nseq, length // tm * 8, LANES), F32)],
        scratch_shapes=[pltpu.VMEM((1, LANES), F32)],
        compiler_params=_cparams(("parallel", "arbitrary")),
        name=name,
    )(lf, tri, selq, selk)


def _attn_kernel(first_ref, q_ref, qx_ref, k_ref, kx_ref, vt_ref, o_ref, m_ref, acc_ref, s0_ref, s1_ref, *, t):
    b, hd, qi = pl.program_id(0), pl.program_id(1), pl.program_id(2)
    q = jnp.concatenate([q_ref[0], qx_ref[0]], axis=1)
    n_rest = qi - first_ref[(b * pl.num_programs(1) + hd) * pl.num_programs(2) + qi]
    m_ref[...] = jnp.full_like(m_ref, -jnp.inf)
    acc_ref[...] = jnp.zeros_like(acc_ref)

    def scores(j):
        st = pl.multiple_of(jnp.maximum(j, 0) * t, t)
        k = jnp.concatenate([k_ref[0, 0, pl.ds(st, t), :], kx_ref[0, 0, pl.ds(st, t), :]], axis=1)
        return lax.dot_general(k, q, (((1,), (1,)), ((), ())), preferred_element_type=F32)

    def accumulate(j, s, masked):
        st = pl.multiple_of(j * t, t)
        if masked:
            key_pos = lax.broadcasted_iota(jnp.int32, (t, t), 0)
            qry_pos = lax.broadcasted_iota(jnp.int32, (t, t), 1)
            s = jnp.where(key_pos <= qry_pos, s, -jnp.inf)
        m_prev = m_ref[...]
        m_new = jnp.maximum(m_prev, jnp.max(s, axis=0, keepdims=True))
        alpha = jnp.exp2(m_prev - m_new)
        p = jnp.exp2(s - m_new).astype(BF16)
        acc_ref[...] = alpha * acc_ref[...] + jnp.dot(vt_ref[0, 0, :, pl.ds(st, t)], p, preferred_element_type=F32)
        m_ref[...] = m_new

    bufs = (s0_ref, s1_ref)

    def visit(k0, n, prefetch_after):
        for u in range(n):
            k = k0[0] + u
            if u + 1 < n or prefetch_after:
                bufs[(k0[1] + u + 1) % 2][...] = scores(qi - k - 1)
            accumulate(qi - k, bufs[(k0[1] + u) % 2][...], False)

    def finish():
        acc = acc_ref[...]
        out_t = acc[0:HEAD_DIM] / acc[HEAD_DIM:HEAD_DIM + 1]
        o_ref[0] = out_t.T.astype(o_ref.dtype)

    for r in range(ATTN_FLAT):
        @pl.when(n_rest == r)
        def _(r=r):
            s_cur = scores(qi)
            for u in range(r + 1):
                s_next = scores(qi - u - 1) if u < r else None
                accumulate(qi - u, s_cur, u == 0)
                s_cur = s_next
            finish()

    @pl.when(n_rest >= ATTN_FLAT)
    def _():
        s0_ref[...] = scores(qi)
        s1_ref[...] = scores(qi - 1)
        accumulate(qi, s0_ref[...], True)

        def group(g, carry):
            visit((1 + ATTN_UNROLL * g, 1), ATTN_UNROLL, True)
            return carry

        lax.fori_loop(0, n_rest // ATTN_UNROLL, group, 0)
        rem = n_rest % ATTN_UNROLL
        for r in range(1, ATTN_UNROLL):
            @pl.when(rem == r)
            def _(r=r):
                visit((n_rest - r + 1, 1), r, False)

        finish()


def _first_needed_chunk(qn2, kn2, ends, n_heads):
    nb, nt = ends.shape[0], ends.shape[1] // 8
    qn = jnp.sqrt(qn2.reshape(nb, nt, n_heads, LANES)[..., 0])
    kn = jnp.sqrt(kn2.reshape(nb, nt, n_heads, LANES)[..., 0])
    ends = ends.reshape(nb, nt, 8, LANES)
    c_first, c_last = ends[:, :, 0, :n_heads], ends[:, :, 1, :n_heads]
    ub = qn[:, :, None, :] * kn[:, None, :, :] + c_first[:, :, None, :] - c_last[:, None, :, :]
    lb = -(qn * kn)[:, :, None, :]
    i = jnp.arange(nt)[None, :, None, None]
    j = jnp.arange(nt)[None, None, :, None]
    needed = (ub - lb >= -PRUNE_GAP) & (j < i)
    first = jnp.min(jnp.where(needed, j, i), axis=2)
    return first.transpose(0, 2, 1).reshape(-1).astype(jnp.int32)


def _attn(first, q, qx, k, kx, vt, *, name):
    nb, length, w = q.shape
    n_heads = w // HEAD_DIM
    t = ATTN_TILE
    assert length % t == 0 and k.shape == (nb, n_heads, length, HEAD_DIM) and vt.shape == (nb, n_heads, VT_ROWS, length)
    blk = lambda b, h, i, first: (b, i, h)
    head = lambda b, h, i, first: (b, h, 0, 0)
    return pl.pallas_call(
        functools.partial(_attn_kernel, t=t),
        grid_spec=pltpu.PrefetchScalarGridSpec(
            num_scalar_prefetch=1,
            grid=(nb, n_heads, length // t),
            in_specs=[pl.BlockSpec((1, t, HEAD_DIM), blk), pl.BlockSpec((1, t, HEAD_DIM), blk),
                      pl.BlockSpec((1, 1, length, HEAD_DIM), head), pl.BlockSpec((1, 1, length, HEAD_DIM), head),
                      pl.BlockSpec((1, 1, VT_ROWS, length), head)],
            out_specs=pl.BlockSpec((1, t, HEAD_DIM), blk),
            scratch_shapes=[pltpu.VMEM((1, t), F32), pltpu.VMEM((VT_ROWS, t), F32),
                            pltpu.VMEM((t, t), F32), pltpu.VMEM((t, t), F32)],
        ),
        out_shape=jax.ShapeDtypeStruct((nb, length, w), BF16),
        compiler_params=_cparams(("parallel", "parallel", "arbitrary")),
        name=name,
    )(first, q, qx, k, kx, vt)


def _attn_cached_kernel(q_ref, qx_ref, kx_ref, ck_ref, cv_ref, kn_ref, vn_ref, o_ref, *, n_heads, past, tn):
    pad = LANES - tn
    causal = (lax.broadcasted_iota(jnp.int32, (tn, LANES), 1) <= lax.broadcasted_iota(jnp.int32, (tn, LANES), 0))
    nt = (((1,), (1,)), ((), ()))
    for hd in range(n_heads):
        cols = slice(hd * HEAD_DIM, (hd + 1) * HEAD_DIM)
        q = jnp.concatenate([q_ref[:, cols], qx_ref[0, :, cols]], axis=1)
        kc = jnp.concatenate([ck_ref[pl.ds(hd, past, stride=n_heads), :].astype(BF16),
                              kx_ref[0, hd, 0:past, :]], axis=1)
        kn = jnp.concatenate([kn_ref[:, cols], kx_ref[0, hd, past:past + tn, :]], axis=1)
        kn = jnp.concatenate([kn, jnp.zeros((pad, kn.shape[1]), BF16)], axis=0)
        s_c = lax.dot_general(q, kc, nt, preferred_element_type=F32)
        s_n = jnp.where(causal, lax.dot_general(q, kn, nt, preferred_element_type=F32), -jnp.inf)
        m = jnp.maximum(jnp.max(s_c, axis=1, keepdims=True), jnp.max(s_n, axis=1, keepdims=True))
        p_c = jnp.exp2(s_c - m)
        p_n = jnp.exp2(s_n - m)
        denom = jnp.sum(p_c, axis=1, keepdims=True) + jnp.sum(p_n, axis=1, keepdims=True)
        vc = cv_ref[pl.ds(hd, past, stride=n_heads), :].astype(BF16)
        vn = jnp.concatenate([vn_ref[:, cols], jnp.zeros((pad, HEAD_DIM), BF16)], axis=0)
        out = (jnp.dot(p_c.astype(BF16), vc, preferred_element_type=F32)
               + jnp.dot(p_n.astype(BF16), vn, preferred_element_type=F32))
        o_ref[:, cols] = (out / denom).astype(o_ref.dtype)


def _attn_cached(q, qx, kx, cache_k, cache_v, k_new, v_new, *, n_seq, past, name):
    n, w = q.shape
    tn = n // n_seq
    n_heads = w // HEAD_DIM
    assert tn <= LANES and tn % 16 == 0 and past % tn == 0 and kx.shape[2] >= past + tn
    ck = cache_k.reshape(n_seq * past * n_heads, HEAD_DIM)
    cv = cache_v.reshape(n_seq * past * n_heads, HEAD_DIM)
    rows = pl.BlockSpec((tn, w), lambda b: (b, 0))
    cached = pl.BlockSpec((past * n_heads, HEAD_DIM), lambda b: (b, 0))
    return pl.pallas_call(
        functools.partial(_attn_cached_kernel, n_heads=n_heads, past=past, tn=tn),
        grid=(n_seq,),
        in_specs=[
            rows,
            pl.BlockSpec((1, tn, w), lambda b: (b, past // tn, 0)),
            pl.BlockSpec((1,) + kx.shape[1:], lambda b: (b, 0, 0, 0)),
            cached, cached, rows, rows,
        ],
        out_specs=rows,
        out_shape=jax.ShapeDtypeStruct((n, w), BF16),
        compiler_params=_cparams(("parallel",)),
        name=name,
    )(q, qx, kx, ck, cv, k_new, v_new)


def _pool_kernel(p_ref, hist_ref, w_ref, sc_ref, o_ref, ext_ref, *, tp, pos0):
    t = pl.program_id(1)

    @pl.when(t == 0)
    def _():
        ext_ref[0:HIST_ROWS, :] = hist_ref[0]

    ext_ref[HIST_ROWS:, :] = p_ref[0]
    pos = pos0 + t * tp + lax.broadcasted_iota(jnp.int32, (tp, 1), 0)
    gd = w_ref.shape[1]
    for g, win in enumerate(POOL_WINDOWS):
        cols = slice(g * gd, (g + 1) * gd)
        cur = ext_ref[HIST_ROWS:, cols]
        tot = cur
        for back in range(1, win):
            tot = tot + ext_ref[HIST_ROWS - back:HIST_ROWS - back + tp, cols]
        cnt = jnp.minimum(pos + 1, win).astype(F32)
        d = (tot / cnt - cur).astype(BF16)
        y = jnp.dot(d, w_ref[g], preferred_element_type=F32) * sc_ref[:, cols]
        o_ref[0, :, cols] = y.astype(o_ref.dtype)
    ext_ref[0:HIST_ROWS, :] = ext_ref[tp:tp + HIST_ROWS, :]


def _pool(p, hist, w_pool, scale, *, pos0, name):
    nseq, length, w = p.shape
    tp = min(512, length)
    assert length % tp == 0 and tp >= HIST_ROWS
    return pl.pallas_call(
        functools.partial(_pool_kernel, tp=tp, pos0=pos0),
        grid=(nseq, length // tp),
        in_specs=[
            pl.BlockSpec((1, tp, w), lambda b, t: (b, t, 0)),
            pl.BlockSpec((1, HIST_ROWS, w), lambda b, t: (b, 0, 0)),
            pl.BlockSpec(w_pool.shape, lambda b, t: (0, 0, 0)),
            pl.BlockSpec((1, w), lambda b, t: (0, 0)),
        ],
        out_specs=pl.BlockSpec((1, tp, w), lambda b, t: (b, t, 0)),
        out_shape=jax.ShapeDtypeStruct((nseq, length, w), BF16),
        scratch_shapes=[pltpu.VMEM((HIST_ROWS + tp, w), F32)],
        compiler_params=_cparams(("parallel", "arbitrary")),
        name=name,
    )(p, hist, w_pool, scale)


def _oproj_kernel(h_ref, fox_ref, pool_ref, wo_ref, o_ref):
    mix = jnp.concatenate([fox_ref[...], pool_ref[...]], axis=1)
    o_ref[...] = h_ref[...] + jnp.dot(mix, wo_ref[...], preferred_element_type=F32)


def _oproj(h, fox, pool, wo, *, name):
    n, d = h.shape
    tm = min(512, n)
    assert n % tm == 0
    return pl.pallas_call(
        _oproj_kernel,
        grid=(n // tm,),
        in_specs=[
            pl.BlockSpec((tm, d), lambda i: (i, 0)),
            pl.BlockSpec((tm, fox.shape[1]), lambda i: (i, 0)),
            pl.BlockSpec((tm, pool.shape[1]), lambda i: (i, 0)),
            pl.BlockSpec(wo.shape, lambda i: (0, 0)),
        ],
        out_specs=pl.BlockSpec((tm, d), lambda i: (i, 0)),
        out_shape=jax.ShapeDtypeStruct((n, d), F32),
        compiler_params=_cparams(("parallel",)),
        name=name,
    )(h, fox, pool, wo)


def _round_up(x, m):
    return (x + m - 1) // m * m


def kernel(x_prompt, x_sample, cache_k, cache_v, cache_logf, state_pool, g_ffn1, w1_gate, w1_up, w1_down, g_mix, w_in, b_f, w_pool, pool_scale, w_o, g_ffn2, w2_gate, w2_up, w2_down, g_final):
    B, S, D = x_prompt.shape
    Bs, Tn, _ = x_sample.shape
    depth, _, P, H, Dh = cache_k.shape
    assert depth == 1 and Dh == HEAD_DIM
    fw = H * Dh
    pw = w_pool.shape[1] * w_pool.shape[2]
    assert w_in.shape[2] == 3 * fw + H + pw and state_pool.shape[2] == POOL_HIST

    row = lambda a: a.reshape(1, -1).astype(F32)
    w1g, w1u, w1d = w1_gate[0].astype(BF16), w1_up[0].astype(BF16), w1_down[0].astype(BF16)
    w2g, w2u, w2d = w2_gate[0].astype(BF16), w2_up[0].astype(BF16), w2_down[0].astype(BF16)
    wi = w_in[0]
    w_qkvp = jnp.concatenate([wi[:, :3 * fw], wi[:, 3 * fw + H:]], axis=1).astype(BF16)
    wf_cols = jnp.tile(wi[:, 3 * fw:3 * fw + H], (1, N_SPLIT))
    wf = jnp.pad(wf_cols, ((0, 0), (0, LANES - N_SPLIT * H))).astype(BF16)
    bf = jnp.pad(jnp.tile(b_f[0].astype(F32), N_SPLIT), (0, LANES - N_SPLIT * H)).reshape(1, LANES)
    wo = w_o[0].astype(BF16)
    wp = w_pool[0].astype(BF16)
    g1, gm, g2, gfin, psc = row(g_ffn1[0]), row(g_mix[0]), row(g_ffn2[0]), row(g_final), row(pool_scale[0])

    def pre(x2d, seq_len, tag):
        h = _ffn(x2d, g1, w1g, w1u, w1d, gfin, final_norm=False, name=f"ffn1_{tag}")
        return (h,) + tuple(_inproj(h, gm, w_qkvp, wf, bf, seq_len=seq_len, name=f"inproj_{tag}"))

    def post(h, fox, pool, tag):
        h2 = _oproj(h, fox, pool, wo, name=f"oproj_{tag}")
        return _ffn(h2, g2, w2g, w2u, w2d, gfin, final_norm=True, name=f"ffn2_{tag}")

    h, q, k, v, kh, vt, p, lf, qn2, kn2 = pre(x_prompt.reshape(B * S, D), S, "prompt")
    qx, kx, ends = _aug(lf.reshape(B, S, LANES), n_heads=H, tile_cap=ATTN_TILE, name="aug_prompt")
    first = _first_needed_chunk(qn2, kn2, ends, H)
    fox = _attn(first, q.reshape(B, S, fw), qx, kh, kx, vt, name="attn_prompt")
    p3 = p.reshape(B, S, pw)
    pool = _pool(p3, jnp.zeros((B, HIST_ROWS, pw), F32), wp, psc, pos0=0, name="pool_prompt")
    y_prompt = post(h, fox.reshape(B * S, fw), pool.reshape(B * S, pw), "prompt").reshape(B, S, D)
    k_prompt = k.reshape(1, B, S, H, Dh)
    v_prompt = v.reshape(1, B, S, H, Dh)
    logf_prompt = lf[:, :H].reshape(1, B, S, H)
    pool_prompt = p3[:, S - POOL_HIST:][None]

    h, q, k, v, kb, vb, p, lf, _, _ = pre(x_sample.reshape(Bs * Tn, D), None, "sample")
    L = _round_up(P + Tn, LANES)
    tail = L - P - Tn
    lf_cache = jnp.pad(jnp.tile(cache_logf[0].astype(F32), (1, 1, N_SPLIT)),
                       ((0, 0), (0, 0), (0, LANES - N_SPLIT * H)))
    lf_all = jnp.concatenate([lf_cache, lf.reshape(Bs, Tn, LANES), jnp.zeros((Bs, tail, LANES), F32)], axis=1)
    qx, kx, _ = _aug(lf_all, n_heads=H, tile_cap=L, name="aug_sample")
    fox = _attn_cached(q, qx, kx, cache_k[0], cache_v[0], kb, vb, n_seq=Bs, past=P, name="attn_sample")
    p3 = p.reshape(Bs, Tn, pw)
    hist = jnp.concatenate([jnp.zeros((Bs, HIST_ROWS - POOL_HIST, pw), F32), state_pool[0].astype(F32)], axis=1)
    pool = _pool(p3, hist, wp, psc, pos0=P, name="pool_sample")
    y_sample = post(h, fox, pool.reshape(Bs * Tn, pw), "sample").reshape(Bs, Tn, D)
    k_sample = k.reshape(1, Bs, Tn, H, Dh)
    v_sample = v.reshape(1, Bs, Tn, H, Dh)
    logf_sample = lf[:, :H].reshape(1, Bs, Tn, H)
    ext_tail = jnp.concatenate([state_pool[0].astype(F32), p3], axis=1)[:, Tn:]
    pool_sample = ext_tail[None]

    return (y_prompt, y_sample, k_prompt, v_prompt, logf_prompt, pool_prompt,
            k_sample, v_sample, logf_sample, pool_sample)
```

```python
import functools

import jax
import jax.numpy as jnp
from jax import lax
from jax.experimental import pallas as pl
from jax.experimental.pallas import tpu as pltpu

F32 = jnp.float32
BF16 = jnp.bfloat16

EPS = 1e-6
HEAD_DIM = 128
LANES = 128
MXU_COLS = 256
POOL_WINDOWS = (2, 4, 8, 16)
POOL_HIST = max(POOL_WINDOWS) - 1
HIST_ROWS = 16
N_SPLIT = 3
VT_ROWS = HEAD_DIM + 16
LOG2E = 1.4426950408889634
VMEM_LIMIT = 56 * 1024 * 1024
FFN_VMEM_LIMIT = 62 * 1024 * 1024
ATTN_UNROLL = 4
ATTN_FLAT = 6
ATTN_TILE = 512
PRUNE_GAP = 170.0
FFN_ROWS = 1024
FFN_COLS = 512
FFN_SLOTS = 2
FFN_NORM_ROWS = 256


def _cparams(sem, vmem_limit=VMEM_LIMIT):
    return pltpu.CompilerParams(dimension_semantics=sem, vmem_limit_bytes=vmem_limit)


def _pick_tile(length, cap, mult):
    best = None
    for t in range(mult, min(cap, length) + 1, mult):
        if length % t == 0:
            best = t
    assert best is not None, (length, cap, mult)
    return best


def _rms(x, g):
    return x * lax.rsqrt(jnp.mean(x * x, axis=-1, keepdims=True) + EPS) * g


def _split3(x):
    hi = x.astype(BF16).astype(F32)
    r1 = x - hi
    mid = r1.astype(BF16).astype(F32)
    lo = (r1 - mid).astype(BF16).astype(F32)
    return hi, mid, lo


def _ffn_kernel(x_ref, g_ref, gf_ref, wg_hbm, wu_hbm, wd_hbm, o_ref, xn_ref, wgu_buf, wd_buf, sem, *,
                final_norm, tf, nf):
    tile = pl.program_id(0)
    flip = (tile * (nf % 2)) % 2

    nblk = tf // MXU_COLS

    def copies(f, flip):
        slot = (f + flip) % 2
        out = []
        for j in range(nblk):
            src = pl.ds(pl.multiple_of(f * tf + j * MXU_COLS, MXU_COLS), MXU_COLS)
            for which, w_hbm in enumerate((wg_hbm, wu_hbm)):
                dst = wgu_buf.at[slot, :, pl.ds((2 * j + which) * MXU_COLS, MXU_COLS)]
                out.append(pltpu.make_async_copy(w_hbm.at[:, src], dst, sem.at[2 * j + which, slot]))
        rows = pl.ds(pl.multiple_of(f * tf, tf), tf)
        out.append(pltpu.make_async_copy(wd_hbm.at[rows, :], wd_buf.at[slot], sem.at[2 * nblk, slot]))
        return out

    def start(f, flip=flip):
        for c in copies(f, flip):
            c.start()

    def wait(f):
        for c in copies(f, flip):
            c.wait()

    def chunk(f, xn):
        slot = (f + flip) % 2
        gu = jnp.dot(xn, wgu_buf[slot], preferred_element_type=F32)
        blocks = [gu[:, j * MXU_COLS:(j + 1) * MXU_COLS] for j in range(2 * nblk)]
        gate = jnp.concatenate(blocks[0::2], axis=1)
        up = jnp.concatenate(blocks[1::2], axis=1)
        a = (gate * jax.nn.sigmoid(gate) * up).astype(BF16)
        return jnp.dot(a, wd_buf[slot], preferred_element_type=F32)

    @pl.when(tile == 0)
    def _():
        start(0)

    wait(0)
    start(1)
    xn = _rms(x_ref[...], g_ref[...]).astype(BF16)
    xn_ref[...] = xn
    o_ref[...] = chunk(0, xn)

    def middle(f, carry):
        wait(f)
        start(f + 1)
        o_ref[...] += chunk(f, xn_ref[...])
        return carry

    lax.fori_loop(1, nf - 1, middle, 0)
    wait(nf - 1)

    @pl.when(tile + 1 < pl.num_programs(0))
    def _():
        start(0, ((tile + 1) * (nf % 2)) % 2)

    o_ref[...] = x_ref[...] + 0.5 * (o_ref[...] + chunk(nf - 1, xn_ref[...]))
    if final_norm:
        rows = min(FFN_NORM_ROWS, o_ref.shape[0])
        for r in range(0, o_ref.shape[0], rows):
            o_ref[r:r + rows, :] = _rms(o_ref[r:r + rows, :], gf_ref[...])


def _ffn(x, g, wg, wu, wd, gf, *, final_norm, name):
    n, d = x.shape
    dff = wg.shape[1]
    tm = min(FFN_ROWS, n)
    tf = FFN_COLS
    nf = dff // tf
    assert n % tm == 0 and dff % tf == 0 and nf >= 3 and tf % MXU_COLS == 0
    return pl.pallas_call(
        functools.partial(_ffn_kernel, final_norm=final_norm, tf=tf, nf=nf),
        grid=(n // tm,),
        in_specs=[
            pl.BlockSpec((tm, d), lambda i: (i, 0)),
            pl.BlockSpec((1, d), lambda i: (0, 0)),
            pl.BlockSpec((1, d), lambda i: (0, 0)),
            pl.BlockSpec(memory_space=pl.ANY),
            pl.BlockSpec(memory_space=pl.ANY),
            pl.BlockSpec(memory_space=pl.ANY),
        ],
        out_specs=pl.BlockSpec((tm, d), lambda i: (i, 0)),
        out_shape=jax.ShapeDtypeStruct((n, d), F32),
        scratch_shapes=[pltpu.VMEM((tm, d), BF16),
                        pltpu.VMEM((FFN_SLOTS, d, 2 * tf), BF16), pltpu.VMEM((FFN_SLOTS, tf, d), BF16),
                        pltpu.SemaphoreType.DMA((2 * (tf // MXU_COLS) + 1, FFN_SLOTS))],
        compiler_params=_cparams(("arbitrary",), FFN_VMEM_LIMIT),
        name=name,
    )(x, g, gf, wg, wu, wd)


def _inproj_kernel(h_ref, g_ref, w_ref, wf_ref, bf_ref,
                   q_ref, k_ref, v_ref, ka_ref, va_ref, p_ref, lf_ref, qn_ref, kn_ref, *, scale, width, head_major):
    hn = _rms(h_ref[...], g_ref[...]).astype(BF16)
    n_heads = width // HEAD_DIM

    def proj(c):
        return jnp.dot(hn, w_ref[:, c * width:(c + 1) * width], preferred_element_type=F32)

    def store_heads(o_ref, u):
        for hd in range(n_heads):
            o_ref[pl.ds(hd, u.shape[0], stride=n_heads), :] = u[:, hd * HEAD_DIM:(hd + 1) * HEAD_DIM]

    def max_sq_norms(xb):
        sub = lax.broadcasted_iota(jnp.int32, (n_heads, LANES), 0)
        out = jnp.zeros((n_heads, LANES), F32)
        for hd in range(n_heads):
            x = xb[:, hd * HEAD_DIM:(hd + 1) * HEAD_DIM].astype(F32)
            top = jnp.max(jnp.sum(x * x, axis=1, keepdims=True), axis=0, keepdims=True)
            out = jnp.where(sub == hd, top, out)
        return out

    lf_ref[...] = jax.nn.log_sigmoid(jnp.dot(hn, wf_ref[...], preferred_element_type=F32) + bf_ref[...])
    qb = (proj(0) * scale).astype(BF16)
    q_ref[...] = qb
    qn_ref[...] = max_sq_norms(qb)
    u = proj(1)
    store_heads(k_ref, u)
    kb = u.astype(BF16)
    kn_ref[...] = max_sq_norms(kb)
    u = proj(2)
    store_heads(v_ref, u)
    vb = u.astype(BF16)
    if head_major:
        sub = lax.broadcasted_iota(jnp.int32, (VT_ROWS - HEAD_DIM, vb.shape[0]), 0)
        tail = jnp.where(sub == 0, 1.0, 0.0).astype(BF16)
        for hd in range(n_heads):
            cols = slice(hd * HEAD_DIM, (hd + 1) * HEAD_DIM)
            ka_ref[0, hd] = kb[:, cols]
            va_ref[0, hd, 0:HEAD_DIM, :] = vb[:, cols].T
            va_ref[0, hd, HEAD_DIM:, :] = tail
    else:
        ka_ref[...] = kb
        va_ref[...] = vb
    p_ref[...] = proj(3)


def _inproj(h, g, w_qkvp, wf, bf, *, seq_len, name):
    n, d = h.shape
    w = w_qkvp.shape[1] // 4
    tm = min(ATTN_TILE, n)
    assert n % tm == 0 and w % HEAD_DIM == 0
    n_heads = w // HEAD_DIM
    assert n_heads == 8
    row = lambda i: (i, 0)
    const = lambda i: (0, 0)
    resident = pl.Buffered(1)
    if seq_len is None:
        ka_spec = va_spec = pl.BlockSpec((tm, w), row)
        ka_shape = va_shape = jax.ShapeDtypeStruct((n, w), BF16)
    else:
        assert seq_len % tm == 0 and n % seq_len == 0
        tps = seq_len // tm
        ka_spec = pl.BlockSpec((1, n_heads, tm, HEAD_DIM), lambda i: (i // tps, 0, i % tps, 0))
        va_spec = pl.BlockSpec((1, n_heads, VT_ROWS, tm), lambda i: (i // tps, 0, 0, i % tps))
        ka_shape = jax.ShapeDtypeStruct((n // seq_len, n_heads, seq_len, HEAD_DIM), BF16)
        va_shape = jax.ShapeDtypeStruct((n // seq_len, n_heads, VT_ROWS, seq_len), BF16)
    return pl.pallas_call(
        functools.partial(_inproj_kernel, scale=HEAD_DIM ** -0.5 * LOG2E, width=w, head_major=seq_len is not None),
        grid=(n // tm,),
        in_specs=[
            pl.BlockSpec((tm, d), row),
            pl.BlockSpec((1, d), const),
            pl.BlockSpec((d, 4 * w), const, pipeline_mode=resident),
            pl.BlockSpec((d, LANES), const),
            pl.BlockSpec((1, LANES), const),
        ],
        out_specs=[
            pl.BlockSpec((tm, w), row),
            pl.BlockSpec((tm * n_heads, HEAD_DIM), row), pl.BlockSpec((tm * n_heads, HEAD_DIM), row),
            ka_spec, va_spec, pl.BlockSpec((tm, w), row),
            pl.BlockSpec((tm, LANES), row),
            pl.BlockSpec((n_heads, LANES), row), pl.BlockSpec((n_heads, LANES), row),
        ],
        out_shape=[
            jax.ShapeDtypeStruct((n, w), BF16),
            jax.ShapeDtypeStruct((n * n_heads, HEAD_DIM), F32),
            jax.ShapeDtypeStruct((n * n_heads, HEAD_DIM), F32),
            ka_shape,
            va_shape,
            jax.ShapeDtypeStruct((n, w), F32),
            jax.ShapeDtypeStruct((n, LANES), F32),
            jax.ShapeDtypeStruct((n // tm * n_heads, LANES), F32),
            jax.ShapeDtypeStruct((n // tm * n_heads, LANES), F32),
        ],
        compiler_params=_cparams(("parallel",)),
        name=name,
    )(h, g, w_qkvp, wf, bf)


def _aug_kernel(lf_ref, tri_ref, selq_ref, selk_ref, qx_ref, kx_ref, ends_ref, carry_ref, *, n_heads):
    t = pl.program_id(1)

    @pl.when(t == 0)
    def _():
        carry_ref[...] = jnp.zeros_like(carry_ref)

    lf = lf_ref[0]
    lane = lax.broadcasted_iota(jnp.int32, lf.shape, 1)

    def pick(parts, extra=None):
        out = jnp.zeros_like(lf) if extra is None else extra
        for j in reversed(range(N_SPLIT)):
            out = jnp.where((lane >= j * n_heads) & (lane < (j + 1) * n_heads), parts[j], out)
        return out

    cs = jnp.dot(tri_ref[...], pick(_split3(lf)).astype(BF16), preferred_element_type=F32)
    c = cs
    for j in range(1, N_SPLIT):
        c = c + pltpu.roll(cs, LANES - j * n_heads, axis=1)
    c = c + carry_ref[...]
    carry_ref[...] = c[c.shape[0] - 1:, :]

    crep = c
    for j in range(1, N_SPLIT):
        crep = jnp.where((lane >= j * n_heads) & (lane < (j + 1) * n_heads),
                         pltpu.roll(c, j * n_heads, axis=1), crep)
    ones = jnp.where(lane == N_SPLIT * n_heads, 1.0, 0.0).astype(F32)
    pieces = pick(_split3(crep * LOG2E), extra=ones).astype(BF16)
    qx_ref[0] = jnp.dot(pieces, selq_ref[...], preferred_element_type=F32).astype(BF16)
    kx = jnp.dot(pieces, selk_ref[...], preferred_element_type=F32).astype(BF16)
    for hd in range(n_heads):
        kx_ref[0, hd] = kx[:, hd * HEAD_DIM:(hd + 1) * HEAD_DIM]
    sub = lax.broadcasted_iota(jnp.int32, (8, LANES), 0)
    ends_ref[0] = jnp.where(sub == 0, c[0:1, :] * LOG2E, jnp.where(sub == 1, c[c.shape[0] - 1:, :] * LOG2E, 0.0))


def _aug_constants(tm, n_heads):
    r = jnp.arange(tm)
    tri = (r[:, None] >= r[None, :]).astype(BF16)
    rows = jnp.arange(LANES)[:, None]
    cols = jnp.arange(n_heads * HEAD_DIM)[None, :]
    head, lane = cols // HEAD_DIM, cols % HEAD_DIM
    one_row = N_SPLIT * n_heads
    is_piece = lambda off: (lane >= off) & (lane < off + N_SPLIT) & (rows == (lane - off) * n_heads + head)
    is_one = lambda off: (lane >= off) & (lane < off + N_SPLIT) & (rows == one_row)
    selq = (is_piece(0) | is_one(N_SPLIT)).astype(BF16)
    selk = is_one(0).astype(BF16) - is_piece(N_SPLIT).astype(BF16)
    return tri, selq, selk


def _aug(lf, *, n_heads, tile_cap, name):
    nseq, length, _ = lf.shape
    tm = _pick_tile(length, tile_cap, 16)
    assert (N_SPLIT * n_heads) < LANES
    w = n_heads * HEAD_DIM
    tri, selq, selk = _aug_constants(tm, n_heads)
    return pl.pallas_call(
        functools.partial(_aug_kernel, n_heads=n_heads),
        grid=(nseq, length // tm),
        in_specs=[
            pl.BlockSpec((1, tm, LANES), lambda b, t: (b, t, 0)),
            pl.BlockSpec((tm, tm), lambda b, t: (0, 0)),
            pl.BlockSpec((LANES, w), lambda b, t: (0, 0)),
            pl.BlockSpec((LANES, w), lambda b, t: (0, 0)),
        ],
        out_specs=[pl.BlockSpec((1, tm, w), lambda b, t: (b, t, 0)),
                   pl.BlockSpec((1, n_heads, tm, HEAD_DIM), lambda b, t: (b, 0, t, 0)),
                   pl.BlockSpec((1, 8, LANES), lambda b, t: (b, t, 0))],
        out_shape=[jax.ShapeDtypeStruct((nseq, length, w), BF16),
                   jax.ShapeDtypeStruct((nseq, n_heads, length, HEAD_DIM), BF16),
                   jax.ShapeDtypeStruct((nseq, length // tm * 8, LANES), F32)],
        scratch_shapes=[pltpu.VMEM((1, LANES), F32)],
        compiler_params=_cparams(("parallel", "arbitrary")),
        name=name,
    )(lf, tri, selq, selk)


def _attn_kernel(first_ref, q_ref, qx_ref, k_ref, kx_ref, vt_ref, o_ref, m_ref, acc_ref, s0_ref, s1_ref, t0_ref, t1_ref,
                 *, t):
    b, hd, qi = pl.program_id(0), pl.program_id(1), pl.program_id(2)
    q = jnp.concatenate([q_ref[0], qx_ref[0]], axis=1)
    n_rest = qi - first_ref[(b * pl.num_programs(1) + hd) * pl.num_programs(2) + qi]
    m_ref[...] = jnp.full_like(m_ref, -jnp.inf)
    acc_ref[...] = jnp.zeros_like(acc_ref)

    def scores(j):
        st = pl.multiple_of(jnp.maximum(j, 0) * t, t)
        k = jnp.concatenate([k_ref[0, 0, pl.ds(st, t), :], kx_ref[0, 0, pl.ds(st, t), :]], axis=1)
        s = lax.dot_general(k, q, (((1,), (1,)), ((), ())), preferred_element_type=F32)
        return s, jnp.max(s, axis=0, keepdims=True)

    def accumulate(j, s, top, masked):
        st = pl.multiple_of(j * t, t)
        if masked:
            key_pos = lax.broadcasted_iota(jnp.int32, (t, t), 0)
            qry_pos = lax.broadcasted_iota(jnp.int32, (t, t), 1)
            s = jnp.where(key_pos <= qry_pos, s, -jnp.inf)
            top = jnp.max(s, axis=0, keepdims=True)
        m_prev = m_ref[...]
        m_new = jnp.maximum(m_prev, top)
        alpha = jnp.exp2(m_prev - m_new)
        p = jnp.exp2(s - m_new).astype(BF16)
        acc_ref[...] = alpha * acc_ref[...] + jnp.dot(vt_ref[0, 0, :, pl.ds(st, t)], p, preferred_element_type=F32)
        m_ref[...] = m_new

    bufs = ((s0_ref, t0_ref), (s1_ref, t1_ref))

    def put(buf, s_top):
        buf[0][...], buf[1][...] = s_top

    def visit(k0, n, prefetch_after):
        for u in range(n):
            k = k0[0] + u
            if u + 1 < n or prefetch_after:
                put(bufs[(k0[1] + u + 1) % 2], scores(qi - k - 1))
            cur = bufs[(k0[1] + u) % 2]
            accumulate(qi - k, cur[0][...], cur[1][...], False)

    def finish():
        acc = acc_ref[...]
        out_t = acc[0:HEAD_DIM] / acc[HEAD_DIM:HEAD_DIM + 1]
        o_ref[0] = out_t.T.astype(o_ref.dtype)

    for r in range(ATTN_FLAT):
        @pl.when(n_rest == r)
        def _(r=r):
            cur = scores(qi)
            for u in range(r + 1):
                nxt = scores(qi - u - 1) if u < r else None
                accumulate(qi - u, cur[0], cur[1], u == 0)
                cur = nxt
            finish()

    @pl.when(n_rest >= ATTN_FLAT)
    def _():
        put(bufs[0], scores(qi))
        put(bufs[1], scores(qi - 1))
        accumulate(qi, s0_ref[...], t0_ref[...], True)

        def group(g, carry):
            visit((1 + ATTN_UNROLL * g, 1), ATTN_UNROLL, True)
            return carry

        lax.fori_loop(0, n_rest // ATTN_UNROLL, group, 0)
        rem = n_rest % ATTN_UNROLL
        for r in range(1, ATTN_UNROLL):
            @pl.when(rem == r)
            def _(r=r):
                visit((n_rest - r + 1, 1), r, False)

        finish()


def _first_needed_chunk(qn2, kn2, ends, n_heads):
    nb, nt = ends.shape[0], ends.shape[1] // 8
    qn = jnp.sqrt(qn2.reshape(nb, nt, n_heads, LANES)[..., 0])
    kn = jnp.sqrt(kn2.reshape(nb, nt, n_heads, LANES)[..., 0])
    ends = ends.reshape(nb, nt, 8, LANES)
    c_first, c_last = ends[:, :, 0, :n_heads], ends[:, :, 1, :n_heads]
    ub = qn[:, :, None, :] * kn[:, None, :, :] + c_first[:, :, None, :] - c_last[:, None, :, :]
    lb = -(qn * kn)[:, :, None, :]
    i = jnp.arange(nt)[None, :, None, None]
    j = jnp.arange(nt)[None, None, :, None]
    needed = (ub - lb >= -PRUNE_GAP) & (j < i)
    first = jnp.min(jnp.where(needed, j, i), axis=2)
    return first.transpose(0, 2, 1).reshape(-1).astype(jnp.int32)


def _attn(first, q, qx, k, kx, vt, *, name):
    nb, length, w = q.shape
    n_heads = w // HEAD_DIM
    t = ATTN_TILE
    assert length % t == 0 and k.shape == (nb, n_heads, length, HEAD_DIM) and vt.shape == (nb, n_heads, VT_ROWS, length)
    blk = lambda b, h, i, first: (b, i, h)
    head = lambda b, h, i, first: (b, h, 0, 0)
    return pl.pallas_call(
        functools.partial(_attn_kernel, t=t),
        grid_spec=pltpu.PrefetchScalarGridSpec(
            num_scalar_prefetch=1,
            grid=(nb, n_heads, length // t),
            in_specs=[pl.BlockSpec((1, t, HEAD_DIM), blk), pl.BlockSpec((1, t, HEAD_DIM), blk),
                      pl.BlockSpec((1, 1, length, HEAD_DIM), head), pl.BlockSpec((1, 1, length, HEAD_DIM), head),
                      pl.BlockSpec((1, 1, VT_ROWS, length), head)],
            out_specs=pl.BlockSpec((1, t, HEAD_DIM), blk),
            scratch_shapes=[pltpu.VMEM((1, t), F32), pltpu.VMEM((VT_ROWS, t), F32),
                            pltpu.VMEM((t, t), F32), pltpu.VMEM((t, t), F32),
                            pltpu.VMEM((1, t), F32), pltpu.VMEM((1, t), F32)],
        ),
        out_shape=jax.ShapeDtypeStruct((nb, length, w), BF16),
        compiler_params=_cparams(("parallel", "parallel", "arbitrary")),
        name=name,
    )(first, q, qx, k, kx, vt)


def _attn_cached_kernel(q_ref, qx_ref, kx_ref, ck_ref, cv_ref, kn_ref, vn_ref, o_ref, *, n_heads, past, tn):
    pad = LANES - tn
    causal = (lax.broadcasted_iota(jnp.int32, (tn, LANES), 1) <= lax.broadcasted_iota(jnp.int32, (tn, LANES), 0))
    nt = (((1,), (1,)), ((), ()))
    for hd in range(n_heads):
        cols = slice(hd * HEAD_DIM, (hd + 1) * HEAD_DIM)
        q = jnp.concatenate([q_ref[:, cols], qx_ref[0, :, cols]], axis=1)
        kc = jnp.concatenate([ck_ref[pl.ds(hd, past, stride=n_heads), :].astype(BF16),
                              kx_ref[0, hd, 0:past, :]], axis=1)
        kn = jnp.concatenate([kn_ref[:, cols], kx_ref[0, hd, past:past + tn, :]], axis=1)
        kn = jnp.concatenate([kn, jnp.zeros((pad, kn.shape[1]), BF16)], axis=0)
        s_c = lax.dot_general(q, kc, nt, preferred_element_type=F32)
        s_n = jnp.where(causal, lax.dot_general(q, kn, nt, preferred_element_type=F32), -jnp.inf)
        m = jnp.maximum(jnp.max(s_c, axis=1, keepdims=True), jnp.max(s_n, axis=1, keepdims=True))
        p_c = jnp.exp2(s_c - m)
        p_n = jnp.exp2(s_n - m)
        denom = jnp.sum(p_c, axis=1, keepdims=True) + jnp.sum(p_n, axis=1, keepdims=True)
        vc = cv_ref[pl.ds(hd, past, stride=n_heads), :].astype(BF16)
        vn = jnp.concatenate([vn_ref[:, cols], jnp.zeros((pad, HEAD_DIM), BF16)], axis=0)
        out = (jnp.dot(p_c.astype(BF16), vc, preferred_element_type=F32)
               + jnp.dot(p_n.astype(BF16), vn, preferred_element_type=F32))
        o_ref[:, cols] = (out / denom).astype(o_ref.dtype)


def _attn_cached(q, qx, kx, cache_k, cache_v, k_new, v_new, *, n_seq, past, name):
    n, w = q.shape
    tn = n // n_seq
    n_heads = w // HEAD_DIM
    assert tn <= LANES and tn % 16 == 0 and past % tn == 0 and kx.shape[2] >= past + tn
    ck = cache_k.reshape(n_seq * past * n_heads, HEAD_DIM)
    cv = cache_v.reshape(n_seq * past * n_heads, HEAD_DIM)
    rows = pl.BlockSpec((tn, w), lambda b: (b, 0))
    cached = pl.BlockSpec((past * n_heads, HEAD_DIM), lambda b: (b, 0))
    return pl.pallas_call(
        functools.partial(_attn_cached_kernel, n_heads=n_heads, past=past, tn=tn),
        grid=(n_seq,),
        in_specs=[
            rows,
            pl.BlockSpec((1, tn, w), lambda b: (b, past // tn, 0)),
            pl.BlockSpec((1,) + kx.shape[1:], lambda b: (b, 0, 0, 0)),
            cached, cached, rows, rows,
        ],
        out_specs=rows,
        out_shape=jax.ShapeDtypeStruct((n, w), BF16),
        compiler_params=_cparams(("parallel",)),
        name=name,
    )(q, qx, kx, ck, cv, k_new, v_new)


def _pool_kernel(p_ref, hist_ref, w_ref, sc_ref, o_ref, ext_ref, *, tp, pos0):
    t = pl.program_id(1)

    @pl.when(t == 0)
    def _():
        ext_ref[0:HIST_ROWS, :] = hist_ref[0]

    ext_ref[HIST_ROWS:, :] = p_ref[0]
    pos = pos0 + t * tp + lax.broadcasted_iota(jnp.int32, (tp, 1), 0)
    gd = w_ref.shape[1]
    for g, win in enumerate(POOL_WINDOWS):
        cols = slice(g * gd, (g + 1) * gd)
        cur = ext_ref[HIST_ROWS:, cols]
        tot = cur
        for back in range(1, win):
            tot = tot + ext_ref[HIST_ROWS - back:HIST_ROWS - back + tp, cols]
        cnt = jnp.minimum(pos + 1, win).astype(F32)
        d = (tot / cnt - cur).astype(BF16)
        y = jnp.dot(d, w_ref[g], preferred_element_type=F32) * sc_ref[:, cols]
        o_ref[0, :, cols] = y.astype(o_ref.dtype)
    ext_ref[0:HIST_ROWS, :] = ext_ref[tp:tp + HIST_ROWS, :]


def _pool(p, hist, w_pool, scale, *, pos0, name):
    nseq, length, w = p.shape
    tp = min(512, length)
    assert length % tp == 0 and tp >= HIST_ROWS
    return pl.pallas_call(
        functools.partial(_pool_kernel, tp=tp, pos0=pos0),
        grid=(nseq, length // tp),
        in_specs=[
            pl.BlockSpec((1, tp, w), lambda b, t: (b, t, 0)),
            pl.BlockSpec((1, HIST_ROWS, w), lambda b, t: (b, 0, 0)),
            pl.BlockSpec(w_pool.shape, lambda b, t: (0, 0, 0)),
            pl.BlockSpec((1, w), lambda b, t: (0, 0)),
        ],
        out_specs=pl.BlockSpec((1, tp, w), lambda b, t: (b, t, 0)),
        out_shape=jax.ShapeDtypeStruct((nseq, length, w), BF16),
        scratch_shapes=[pltpu.VMEM((HIST_ROWS + tp, w), F32)],
        compiler_params=_cparams(("parallel", "arbitrary")),
        name=name,
    )(p, hist, w_pool, scale)


def _oproj_kernel(h_ref, fox_ref, pool_ref, wo_ref, o_ref):
    mix = jnp.concatenate([fox_ref[...], pool_ref[...]], axis=1)
    o_ref[...] = h_ref[...] + jnp.dot(mix, wo_ref[...], preferred_element_type=F32)


def _oproj(h, fox, pool, wo, *, name):
    n, d = h.shape
    tm = min(512, n)
    assert n % tm == 0
    return pl.pallas_call(
        _oproj_kernel,
        grid=(n // tm,),
        in_specs=[
            pl.BlockSpec((tm, d), lambda i: (i, 0)),
            pl.BlockSpec((tm, fox.shape[1]), lambda i: (i, 0)),
            pl.BlockSpec((tm, pool.shape[1]), lambda i: (i, 0)),
            pl.BlockSpec(wo.shape, lambda i: (0, 0)),
        ],
        out_specs=pl.BlockSpec((tm, d), lambda i: (i, 0)),
        out_shape=jax.ShapeDtypeStruct((n, d), F32),
        compiler_params=_cparams(("parallel",)),
        name=name,
    )(h, fox, pool, wo)


def _round_up(x, m):
    return (x + m - 1) // m * m


def kernel(x_prompt, x_sample, cache_k, cache_v, cache_logf, state_pool, g_ffn1, w1_gate, w1_up, w1_down, g_mix, w_in, b_f, w_pool, pool_scale, w_o, g_ffn2, w2_gate, w2_up, w2_down, g_final):
    B, S, D = x_prompt.shape
    Bs, Tn, _ = x_sample.shape
    depth, _, P, H, Dh = cache_k.shape
    assert depth == 1 and Dh == HEAD_DIM
    fw = H * Dh
    pw = w_pool.shape[1] * w_pool.shape[2]
    assert w_in.shape[2] == 3 * fw + H + pw and state_pool.shape[2] == POOL_HIST

    row = lambda a: a.reshape(1, -1).astype(F32)
    w1g, w1u, w1d = w1_gate[0].astype(BF16), w1_up[0].astype(BF16), w1_down[0].astype(BF16)
    w2g, w2u, w2d = w2_gate[0].astype(BF16), w2_up[0].astype(BF16), w2_down[0].astype(BF16)
    wi = w_in[0]
    w_qkvp = jnp.concatenate([wi[:, :3 * fw], wi[:, 3 * fw + H:]], axis=1).astype(BF16)
    wf_cols = jnp.tile(wi[:, 3 * fw:3 * fw + H], (1, N_SPLIT))
    wf = jnp.pad(wf_cols, ((0, 0), (0, LANES - N_SPLIT * H))).astype(BF16)
    bf = jnp.pad(jnp.tile(b_f[0].astype(F32), N_SPLIT), (0, LANES - N_SPLIT * H)).reshape(1, LANES)
    wo = w_o[0].astype(BF16)
    wp = w_pool[0].astype(BF16)
    g1, gm, g2, gfin, psc = row(g_ffn1[0]), row(g_mix[0]), row(g_ffn2[0]), row(g_final), row(pool_scale[0])

    def pre(x2d, seq_len, tag):
        h = _ffn(x2d, g1, w1g, w1u, w1d, gfin, final_norm=False, name=f"ffn1_{tag}")
        return (h,) + tuple(_inproj(h, gm, w_qkvp, wf, bf, seq_len=seq_len, name=f"inproj_{tag}"))

    def post(h, fox, pool, tag):
        h2 = _oproj(h, fox, pool, wo, name=f"oproj_{tag}")
        return _ffn(h2, g2, w2g, w2u, w2d, gfin, final_norm=True, name=f"ffn2_{tag}")

    h, q, k, v, kh, vt, p, lf, qn2, kn2 = pre(x_prompt.reshape(B * S, D), S, "prompt")
    qx, kx, ends = _aug(lf.reshape(B, S, LANES), n_heads=H, tile_cap=ATTN_TILE, name="aug_prompt")
    first = _first_needed_chunk(qn2, kn2, ends, H)
    fox = _attn(first, q.reshape(B, S, fw), qx, kh, kx, vt, name="attn_prompt")
    p3 = p.reshape(B, S, pw)
    pool = _pool(p3, jnp.zeros((B, HIST_ROWS, pw), F32), wp, psc, pos0=0, name="pool_prompt")
    y_prompt = post(h, fox.reshape(B * S, fw), pool.reshape(B * S, pw), "prompt").reshape(B, S, D)
    k_prompt = k.reshape(1, B, S, H, Dh)
    v_prompt = v.reshape(1, B, S, H, Dh)
    logf_prompt = lf[:, :H].reshape(1, B, S, H)
    pool_prompt = p3[:, S - POOL_HIST:][None]

    h, q, k, v, kb, vb, p, lf, _, _ = pre(x_sample.reshape(Bs * Tn, D), None, "sample")
    L = _round_up(P + Tn, LANES)
    tail = L - P - Tn
    lf_cache = jnp.pad(jnp.tile(cache_logf[0].astype(F32), (1, 1, N_SPLIT)),
                       ((0, 0), (0, 0), (0, LANES - N_SPLIT * H)))
    lf_all = jnp.concatenate([lf_cache, lf.reshape(Bs, Tn, LANES), jnp.zeros((Bs, tail, LANES), F32)], axis=1)
    qx, kx, _ = _aug(lf_all, n_heads=H, tile_cap=L, name="aug_sample")
    fox = _attn_cached(q, qx, kx, cache_k[0], cache_v[0], kb, vb, n_seq=Bs, past=P, name="attn_sample")
    p3 = p.reshape(Bs, Tn, pw)
    hist = jnp.concatenate([jnp.zeros((Bs, HIST_ROWS - POOL_HIST, pw), F32), state_pool[0].astype(F32)], axis=1)
    pool = _pool(p3, hist, wp, psc, pos0=P, name="pool_sample")
    y_sample = post(h, fox, pool.reshape(Bs * Tn, pw), "sample").reshape(Bs, Tn, D)
    k_sample = k.reshape(1, Bs, Tn, H, Dh)
    v_sample = v.reshape(1, Bs, Tn, H, Dh)
    logf_sample = lf[:, :H].reshape(1, Bs, Tn, H)
    ext_tail = jnp.concatenate([state_pool[0].astype(F32), p3], axis=1)[:, Tn:]
    pool_sample = ext_tail[None]

    return (y_prompt, y_sample, k_prompt, v_prompt, logf_prompt, pool_prompt,
            k_sample, v_sample, logf_sample, pool_sample)
```

```python
import functools

import jax
import jax.numpy as jnp
from jax import lax
from jax.experimental import pallas as pl
from jax.experimental.pallas import tpu as pltpu

F32 = jnp.float32
BF16 = jnp.bfloat16

EPS = 1e-6
HEAD_DIM = 128
LANES = 128
MXU_COLS = 256
POOL_WINDOWS = (2, 4, 8, 16)
POOL_HIST = max(POOL_WINDOWS) - 1
HIST_ROWS = 16
N_SPLIT = 3
VT_ROWS = HEAD_DIM + 16
LOG2E = 1.4426950408889634
VMEM_LIMIT = 56 * 1024 * 1024
FFN_VMEM_LIMIT = 62 * 1024 * 1024
ATTN_UNROLL = 4
ATTN_FLAT = 6
ATTN_TILE = 512
PRUNE_GAP = 170.0
FFN_ROWS = 1024
FFN_COLS = 512
FFN_SLOTS = 2
FFN_NORM_ROWS = 256


def _cparams(sem, vmem_limit=VMEM_LIMIT):
    return pltpu.CompilerParams(dimension_semantics=sem, vmem_limit_bytes=vmem_limit)


def _pick_tile(length, cap, mult):
    best = None
    for t in range(mult, min(cap, length) + 1, mult):
        if length % t == 0:
            best = t
    assert best is not None, (length, cap, mult)
    return best


def _rms(x, g):
    return x * lax.rsqrt(jnp.mean(x * x, axis=-1, keepdims=True) + EPS) * g


def _split3(x):
    hi = x.astype(BF16).astype(F32)
    r1 = x - hi
    mid = r1.astype(BF16).astype(F32)
    lo = (r1 - mid).astype(BF16).astype(F32)
    return hi, mid, lo


def _ffn_kernel(x_ref, g_ref, gf_ref, wg_hbm, wu_hbm, wd_hbm, o_ref, xn_ref, wgu_buf, wd_buf, sem, *,
                final_norm, tf, nf):
    tile = pl.program_id(0)
    flip = (tile * (nf % 2)) % 2

    nblk = tf // MXU_COLS

    def copies(f, flip):
        slot = (f + flip) % 2
        out = []
        for j in range(nblk):
            src = pl.ds(pl.multiple_of(f * tf + j * MXU_COLS, MXU_COLS), MXU_COLS)
            for which, w_hbm in enumerate((wg_hbm, wu_hbm)):
                dst = wgu_buf.at[slot, :, pl.ds((2 * j + which) * MXU_COLS, MXU_COLS)]
                out.append(pltpu.make_async_copy(w_hbm.at[:, src], dst, sem.at[2 * j + which, slot]))
        rows = pl.ds(pl.multiple_of(f * tf, tf), tf)
        out.append(pltpu.make_async_copy(wd_hbm.at[rows, :], wd_buf.at[slot], sem.at[2 * nblk, slot]))
        return out

    def start(f, flip=flip):
        for c in copies(f, flip):
            c.start()

    def wait(f):
        for c in copies(f, flip):
            c.wait()

    def chunk(f, xn):
        slot = (f + flip) % 2
        gu = jnp.dot(xn, wgu_buf[slot], preferred_element_type=F32)
        blocks = [gu[:, j * MXU_COLS:(j + 1) * MXU_COLS] for j in range(2 * nblk)]
        gate = jnp.concatenate(blocks[0::2], axis=1)
        up = jnp.concatenate(blocks[1::2], axis=1)
        a = (gate * jax.nn.sigmoid(gate) * up).astype(BF16)
        return jnp.dot(a, wd_buf[slot], preferred_element_type=F32)

    @pl.when(tile == 0)
    def _():
        start(0)

    wait(0)
    start(1)
    xn = _rms(x_ref[...], g_ref[...]).astype(BF16)
    xn_ref[...] = xn
    o_ref[...] = chunk(0, xn)

    def middle(f, carry):
        wait(f)
        start(f + 1)
        o_ref[...] += chunk(f, xn_ref[...])
        return carry

    lax.fori_loop(1, nf - 1, middle, 0)
    wait(nf - 1)

    @pl.when(tile + 1 < pl.num_programs(0))
    def _():
        start(0, ((tile + 1) * (nf % 2)) % 2)

    o_ref[...] = x_ref[...] + 0.5 * (o_ref[...] + chunk(nf - 1, xn_ref[...]))
    if final_norm:
        rows = min(FFN_NORM_ROWS, o_ref.shape[0])
        for r in range(0, o_ref.shape[0], rows):
            o_ref[r:r + rows, :] = _rms(o_ref[r:r + rows, :], gf_ref[...])


def _ffn(x, g, wg, wu, wd, gf, *, final_norm, name):
    n, d = x.shape
    dff = wg.shape[1]
    tm = min(FFN_ROWS, n)
    tf = FFN_COLS
    nf = dff // tf
    assert n % tm == 0 and dff % tf == 0 and nf >= 3 and tf % MXU_COLS == 0
    return pl.pallas_call(
        functools.partial(_ffn_kernel, final_norm=final_norm, tf=tf, nf=nf),
        grid=(n // tm,),
        in_specs=[
            pl.BlockSpec((tm, d), lambda i: (i, 0)),
            pl.BlockSpec((1, d), lambda i: (0, 0)),
            pl.BlockSpec((1, d), lambda i: (0, 0)),
            pl.BlockSpec(memory_space=pl.ANY),
            pl.BlockSpec(memory_space=pl.ANY),
            pl.BlockSpec(memory_space=pl.ANY),
        ],
        out_specs=pl.BlockSpec((tm, d), lambda i: (i, 0)),
        out_shape=jax.ShapeDtypeStruct((n, d), F32),
        scratch_shapes=[pltpu.VMEM((tm, d), BF16),
                        pltpu.VMEM((FFN_SLOTS, d, 2 * tf), BF16), pltpu.VMEM((FFN_SLOTS, tf, d), BF16),
                        pltpu.SemaphoreType.DMA((2 * (tf // MXU_COLS) + 1, FFN_SLOTS))],
        compiler_params=_cparams(("arbitrary",), FFN_VMEM_LIMIT),
        name=name,
    )(x, g, gf, wg, wu, wd)


def _inproj_kernel(h_ref, g_ref, w_ref, wf_ref, bf_ref,
                   q_ref, k_ref, v_ref, ka_ref, va_ref, p_ref, lf_ref, qn_ref, kn_ref, *, scale, width, head_major):
    hn = _rms(h_ref[...], g_ref[...]).astype(BF16)
    n_heads = width // HEAD_DIM

    def proj(c):
        return jnp.dot(hn, w_ref[:, c * width:(c + 1) * width], preferred_element_type=F32)

    def store_heads(o_ref, u):
        for hd in range(n_heads):
            o_ref[pl.ds(hd, u.shape[0], stride=n_heads), :] = u[:, hd * HEAD_DIM:(hd + 1) * HEAD_DIM]

    def max_sq_norms(xb):
        sub = lax.broadcasted_iota(jnp.int32, (n_heads, LANES), 0)
        out = jnp.zeros((n_heads, LANES), F32)
        for hd in range(n_heads):
            x = xb[:, hd * HEAD_DIM:(hd + 1) * HEAD_DIM].astype(F32)
            top = jnp.max(jnp.sum(x * x, axis=1, keepdims=True), axis=0, keepdims=True)
            out = jnp.where(sub == hd, top, out)
        return out

    lf_ref[...] = jax.nn.log_sigmoid(jnp.dot(hn, wf_ref[...], preferred_element_type=F32) + bf_ref[...])
    qb = (proj(0) * scale).astype(BF16)
    q_ref[...] = qb
    qn_ref[...] = max_sq_norms(qb)
    u = proj(1)
    store_heads(k_ref, u)
    kb = u.astype(BF16)
    kn_ref[...] = max_sq_norms(kb)
    u = proj(2)
    store_heads(v_ref, u)
    vb = u.astype(BF16)
    if head_major:
        sub = lax.broadcasted_iota(jnp.int32, (VT_ROWS - HEAD_DIM, vb.shape[0]), 0)
        tail = jnp.where(sub == 0, 1.0, 0.0).astype(BF16)
        for hd in range(n_heads):
            cols = slice(hd * HEAD_DIM, (hd + 1) * HEAD_DIM)
            ka_ref[0, hd] = kb[:, cols]
            va_ref[0, hd, 0:HEAD_DIM, :] = vb[:, cols].T
            va_ref[0, hd, HEAD_DIM:, :] = tail
    else:
        ka_ref[...] = kb
        va_ref[...] = vb
    p_ref[...] = proj(3)


def _inproj(h, g, w_qkvp, wf, bf, *, seq_len, name):
    n, d = h.shape
    w = w_qkvp.shape[1] // 4
    tm = min(ATTN_TILE, n)
    assert n % tm == 0 and w % HEAD_DIM == 0
    n_heads = w // HEAD_DIM
    assert n_heads == 8
    row = lambda i: (i, 0)
    const = lambda i: (0, 0)
    resident = pl.Buffered(1)
    if seq_len is None:
        ka_spec = va_spec = pl.BlockSpec((tm, w), row)
        ka_shape = va_shape = jax.ShapeDtypeStruct((n, w), BF16)
    else:
        assert seq_len % tm == 0 and n % seq_len == 0
        tps = seq_len // tm
        ka_spec = pl.BlockSpec((1, n_heads, tm, HEAD_DIM), lambda i: (i // tps, 0, i % tps, 0))
        va_spec = pl.BlockSpec((1, n_heads, VT_ROWS, tm), lambda i: (i // tps, 0, 0, i % tps))
        ka_shape = jax.ShapeDtypeStruct((n // seq_len, n_heads, seq_len, HEAD_DIM), BF16)
        va_shape = jax.ShapeDtypeStruct((n // seq_len, n_heads, VT_ROWS, seq_len), BF16)
    return pl.pallas_call(
        functools.partial(_inproj_kernel, scale=HEAD_DIM ** -0.5 * LOG2E, width=w, head_major=seq_len is not None),
        grid=(n // tm,),
        in_specs=[
            pl.BlockSpec((tm, d), row),
            pl.BlockSpec((1, d), const),
            pl.BlockSpec((d, 4 * w), const, pipeline_mode=resident),
            pl.BlockSpec((d, LANES), const),
            pl.BlockSpec((1, LANES), const),
        ],
        out_specs=[
            pl.BlockSpec((tm, w), row),
            pl.BlockSpec((tm * n_heads, HEAD_DIM), row), pl.BlockSpec((tm * n_heads, HEAD_DIM), row),
            ka_spec, va_spec, pl.BlockSpec((tm, w), row),
            pl.BlockSpec((tm, LANES), row),
            pl.BlockSpec((n_heads, LANES), row), pl.BlockSpec((n_heads, LANES), row),
        ],
        out_shape=[
            jax.ShapeDtypeStruct((n, w), BF16),
            jax.ShapeDtypeStruct((n * n_heads, HEAD_DIM), F32),
            jax.ShapeDtypeStruct((n * n_heads, HEAD_DIM), F32),
            ka_shape,
            va_shape,
            jax.ShapeDtypeStruct((n, w), F32),
            jax.ShapeDtypeStruct((n, LANES), F32),
            jax.ShapeDtypeStruct((n // tm * n_heads, LANES), F32),
            jax.ShapeDtypeStruct((n // tm * n_heads, LANES), F32),
        ],
        compiler_params=_cparams(("parallel",)),
        name=name,
    )(h, g, w_qkvp, wf, bf)


def _aug_kernel(lf_ref, tri_ref, selq_ref, selk_ref, qx_ref, kx_ref, ends_ref, carry_ref, *, n_heads):
    t = pl.program_id(1)

    @pl.when(t == 0)
    def _():
        carry_ref[...] = jnp.zeros_like(carry_ref)

    lf = lf_ref[0]
    lane = lax.broadcasted_iota(jnp.int32, lf.shape, 1)

    def pick(parts, extra=None):
        out = jnp.zeros_like(lf) if extra is None else extra
        for j in reversed(range(N_SPLIT)):
            out = jnp.where((lane >= j * n_heads) & (lane < (j + 1) * n_heads), parts[j], out)
        return out

    cs = jnp.dot(tri_ref[...], pick(_split3(lf)).astype(BF16), preferred_element_type=F32)
    c = cs
    for j in range(1, N_SPLIT):
        c = c + pltpu.roll(cs, LANES - j * n_heads, axis=1)
    c = c + carry_ref[...]
    carry_ref[...] = c[c.shape[0] - 1:, :]

    crep = c
    for j in range(1, N_SPLIT):
        crep = jnp.where((lane >= j * n_heads) & (lane < (j + 1) * n_heads),
                         pltpu.roll(c, j * n_heads, axis=1), crep)
    ones = jnp.where(lane == N_SPLIT * n_heads, 1.0, 0.0).astype(F32)
    pieces = pick(_split3(crep * LOG2E), extra=ones).astype(BF16)
    qx_ref[0] = jnp.dot(pieces, selq_ref[...], preferred_element_type=F32).astype(BF16)
    kx = jnp.dot(pieces, selk_ref[...], preferred_element_type=F32).astype(BF16)
    for hd in range(n_heads):
        kx_ref[0, hd] = kx[:, hd * HEAD_DIM:(hd + 1) * HEAD_DIM]
    sub = lax.broadcasted_iota(jnp.int32, (8, LANES), 0)
    ends_ref[0] = jnp.where(sub == 0, c[0:1, :] * LOG2E, jnp.where(sub == 1, c[c.shape[0] - 1:, :] * LOG2E, 0.0))


def _aug_constants(tm, n_heads):
    r = jnp.arange(tm)
    tri = (r[:, None] >= r[None, :]).astype(BF16)
    rows = jnp.arange(LANES)[:, None]
    cols = jnp.arange(n_heads * HEAD_DIM)[None, :]
    head, lane = cols // HEAD_DIM, cols % HEAD_DIM
    one_row = N_SPLIT * n_heads
    is_piece = lambda off: (lane >= off) & (lane < off + N_SPLIT) & (rows == (lane - off) * n_heads + head)
    is_one = lambda off: (lane >= off) & (lane < off + N_SPLIT) & (rows == one_row)
    selq = (is_piece(0) | is_one(N_SPLIT)).astype(BF16)
    selk = is_one(0).astype(BF16) - is_piece(N_SPLIT).astype(BF16)
    return tri, selq, selk


def _aug(lf, *, n_heads, tile_cap, name):
    nseq, length, _ = lf.shape
    tm = _pick_tile(length, tile_cap, 16)
    assert (N_SPLIT * n_heads) < LANES
    w = n_heads * HEAD_DIM
    tri, selq, selk = _aug_constants(tm, n_heads)
    return pl.pallas_call(
        functools.partial(_aug_kernel, n_heads=n_heads),
        grid=(nseq, length // tm),
        in_specs=[
            pl.BlockSpec((1, tm, LANES), lambda b, t: (b, t, 0)),
            pl.BlockSpec((tm, tm), lambda b, t: (0, 0)),
            pl.BlockSpec((LANES, w), lambda b, t: (0, 0)),
            pl.BlockSpec((LANES, w), lambda b, t: (0, 0)),
        ],
        out_specs=[pl.BlockSpec((1, tm, w), lambda b, t: (b, t, 0)),
                   pl.BlockSpec((1, n_heads, tm, HEAD_DIM), lambda b, t: (b, 0, t, 0)),
                   pl.BlockSpec((1, 8, LANES), lambda b, t: (b, t, 0))],
        out_shape=[jax.ShapeDtypeStruct((nseq, length, w), BF16),
                   jax.ShapeDtypeStruct((nseq, n_heads, length, HEAD_DIM), BF16),
                   jax.ShapeDtypeStruct((nseq, length // tm * 8, LANES), F32)],
        scratch_shapes=[pltpu.VMEM((1, LANES), F32)],
        compiler_params=_cparams(("parallel", "arbitrary")),
        name=name,
    )(lf, tri, selq, selk)


def _attn_kernel(first_ref, q_ref, qx_ref, k_ref, kx_ref, vt_ref, o_ref, m_ref, acc_ref, s0_ref, s1_ref, *, t):
    b, hd, qi = pl.program_id(0), pl.program_id(1), pl.program_id(2)
    q = jnp.concatenate([q_ref[0], qx_ref[0]], axis=1)
    n_rest = qi - first_ref[(b * pl.num_programs(1) + hd) * pl.num_programs(2) + qi]
    m_ref[...] = jnp.full_like(m_ref, -jnp.inf)
    acc_ref[...] = jnp.zeros_like(acc_ref)

    def scores(j):
        st = pl.multiple_of(jnp.maximum(j, 0) * t, t)
        k = jnp.concatenate([k_ref[0, 0, pl.ds(st, t), :], kx_ref[0, 0, pl.ds(st, t), :]], axis=1)
        return lax.dot_general(k, q, (((1,), (1,)), ((), ())), preferred_element_type=F32)

    def accumulate(j, s, masked):
        st = pl.multiple_of(j * t, t)
        if masked:
            key_pos = lax.broadcasted_iota(jnp.int32, (t, t), 0)
            qry_pos = lax.broadcasted_iota(jnp.int32, (t, t), 1)
            s = jnp.where(key_pos <= qry_pos, s, -jnp.inf)
        m_prev = m_ref[...]
        m_new = jnp.maximum(m_prev, jnp.max(s, axis=0, keepdims=True))
        alpha = jnp.exp2(m_prev - m_new)
        p = jnp.exp2(s - m_new).astype(BF16)
        acc_ref[...] = alpha * acc_ref[...] + jnp.dot(vt_ref[0, 0, :, pl.ds(st, t)], p, preferred_element_type=F32)
        m_ref[...] = m_new

    bufs = (s0_ref, s1_ref)

    def visit(k0, n, prefetch_after):
        for u in range(n):
            k = k0[0] + u
            if u + 1 < n or prefetch_after:
                bufs[(k0[1] + u + 1) % 2][...] = scores(qi - k - 1)
            accumulate(qi - k, bufs[(k0[1] + u) % 2][...], False)

    def finish():
        acc = acc_ref[...]
        out_t = acc[0:HEAD_DIM] / acc[HEAD_DIM:HEAD_DIM + 1]
        o_ref[0] = out_t.T.astype(o_ref.dtype)

    for r in range(ATTN_FLAT):
        @pl.when(n_rest == r)
        def _(r=r):
            s_cur = scores(qi)
            for u in range(r + 1):
                s_next = scores(qi - u - 1) if u < r else None
                accumulate(qi - u, s_cur, u == 0)
                s_cur = s_next
            finish()

    @pl.when(n_rest >= ATTN_FLAT)
    def _():
        s0_ref[...] = scores(qi)
        s1_ref[...] = scores(qi - 1)
        accumulate(qi, s0_ref[...], True)

        def group(g, carry):
            visit((1 + ATTN_UNROLL * g, 1), ATTN_UNROLL, True)
            return carry

        lax.fori_loop(0, n_rest // ATTN_UNROLL, group, 0)
        rem = n_rest % ATTN_UNROLL
        for r in range(1, ATTN_UNROLL):
            @pl.when(rem == r)
            def _(r=r):
                visit((n_rest - r + 1, 1), r, False)

        finish()


def _first_needed_chunk(qn2, kn2, ends, n_heads):
    nb, nt = ends.shape[0], ends.shape[1] // 8
    qn = jnp.sqrt(qn2.reshape(nb, nt, n_heads, LANES)[..., 0])
    kn = jnp.sqrt(kn2.reshape(nb, nt, n_heads, LANES)[..., 0])
    ends = ends.reshape(nb, nt, 8, LANES)
    c_first, c_last = ends[:, :, 0, :n_heads], ends[:, :, 1, :n_heads]
    ub = qn[:, :, None, :] * kn[:, None, :, :] + c_first[:, :, None, :] - c_last[:, None, :, :]
    lb = -(qn * kn)[:, :, None, :]
    i = jnp.arange(nt)[None, :, None, None]
    j = jnp.arange(nt)[None, None, :, None]
    needed = (ub - lb >= -PRUNE_GAP) & (j < i)
    first = jnp.min(jnp.where(needed, j, i), axis=2)
    return first.transpose(0, 2, 1).reshape(-1).astype(jnp.int32)


def _attn(first, q, qx, k, kx, vt, *, name):
    nb, length, w = q.shape
    n_heads = w // HEAD_DIM
    t = ATTN_TILE
    assert length % t == 0 and k.shape == (nb, n_heads, length, HEAD_DIM) and vt.shape == (nb, n_heads, VT_ROWS, length)
    blk = lambda b, h, i, first: (b, i, h)
    head = lambda b, h, i, first: (b, h, 0, 0)
    return pl.pallas_call(
        functools.partial(_attn_kernel, t=t),
        grid_spec=pltpu.PrefetchScalarGridSpec(
            num_scalar_prefetch=1,
            grid=(nb, n_heads, length // t),
            in_specs=[pl.BlockSpec((1, t, HEAD_DIM), blk), pl.BlockSpec((1, t, HEAD_DIM), blk),
                      pl.BlockSpec((1, 1, length, HEAD_DIM), head), pl.BlockSpec((1, 1, length, HEAD_DIM), head),
                      pl.BlockSpec((1, 1, VT_ROWS, length), head)],
            out_specs=pl.BlockSpec((1, t, HEAD_DIM), blk),
            scratch_shapes=[pltpu.VMEM((1, t), F32), pltpu.VMEM((VT_ROWS, t), F32),
                            pltpu.VMEM((t, t), F32), pltpu.VMEM((t, t), F32)],
        ),
        out_shape=jax.ShapeDtypeStruct((nb, length, w), BF16),
        compiler_params=_cparams(("parallel", "parallel", "arbitrary")),
        name=name,
    )(first, q, qx, k, kx, vt)


def _attn_cached_kernel(q_ref, qx_ref, kx_ref, ck_ref, cv_ref, kn_ref, vn_ref, o_ref, *, n_heads, past, tn):
    pad = LANES - tn
    causal = (lax.broadcasted_iota(jnp.int32, (tn, LANES), 1) <= lax.broadcasted_iota(jnp.int32, (tn, LANES), 0))
    nt = (((1,), (1,)), ((), ()))
    for hd in range(n_heads):
        cols = slice(hd * HEAD_DIM, (hd + 1) * HEAD_DIM)
        q = jnp.concatenate([q_ref[:, cols], qx_ref[0, :, cols]], axis=1)
        kc = jnp.concatenate([ck_ref[pl.ds(hd, past, stride=n_heads), :].astype(BF16),
                              kx_ref[0, hd, 0:past, :]], axis=1)
        kn = jnp.concatenate([kn_ref[:, cols], kx_ref[0, hd, past:past + tn, :]], axis=1)
        kn = jnp.concatenate([kn, jnp.zeros((pad, kn.shape[1]), BF16)], axis=0)
        s_c = lax.dot_general(q, kc, nt, preferred_element_type=F32)
        s_n = jnp.where(causal, lax.dot_general(q, kn, nt, preferred_element_type=F32), -jnp.inf)
        m = jnp.maximum(jnp.max(s_c, axis=1, keepdims=True), jnp.max(s_n, axis=1, keepdims=True))
        p_c = jnp.exp2(s_c - m)
        p_n = jnp.exp2(s_n - m)
        denom = jnp.sum(p_c, axis=1, keepdims=True) + jnp.sum(p_n, axis=1, keepdims=True)
        vc = cv_ref[pl.ds(hd, past, stride=n_heads), :].astype(BF16)
        vn = jnp.concatenate([vn_ref[:, cols], jnp.zeros((pad, HEAD_DIM), BF16)], axis=0)
        out = (jnp.dot(p_c.astype(BF16), vc, preferred_element_type=F32)
               + jnp.dot(p_n.astype(BF16), vn, preferred_element_type=F32))
        o_ref[:, cols] = (out / denom).astype(o_ref.dtype)


def _attn_cached(q, qx, kx, cache_k, cache_v, k_new, v_new, *, n_seq, past, name):
    n, w = q.shape
    tn = n // n_seq
    n_heads = w // HEAD_DIM
    assert tn <= LANES and tn % 16 == 0 and past % tn == 0 and kx.shape[2] >= past + tn
    ck = cache_k.reshape(n_seq * past * n_heads, HEAD_DIM)
    cv = cache_v.reshape(n_seq * past * n_heads, HEAD_DIM)
    rows = pl.BlockSpec((tn, w), lambda b: (b, 0))
    cached = pl.BlockSpec((past * n_heads, HEAD_DIM), lambda b: (b, 0))
    return pl.pallas_call(
        functools.partial(_attn_cached_kernel, n_heads=n_heads, past=past, tn=tn),
        grid=(n_seq,),
        in_specs=[
            rows,
            pl.BlockSpec((1, tn, w), lambda b: (b, past // tn, 0)),
            pl.BlockSpec((1,) + kx.shape[1:], lambda b: (b, 0, 0, 0)),
            cached, cached, rows, rows,
        ],
        out_specs=rows,
        out_shape=jax.ShapeDtypeStruct((n, w), BF16),
        compiler_params=_cparams(("parallel",)),
        name=name,
    )(q, qx, kx, ck, cv, k_new, v_new)


def _pool_kernel(p_ref, hist_ref, w_ref, sc_ref, o_ref, ext_ref, *, tp, pos0):
    t = pl.program_id(1)

    @pl.when(t == 0)
    def _():
        ext_ref[0:HIST_ROWS, :] = hist_ref[0]

    ext_ref[HIST_ROWS:, :] = p_ref[0]
    pos = pos0 + t * tp + lax.broadcasted_iota(jnp.int32, (tp, 1), 0)
    gd = w_ref.shape[1]
    for g, win in enumerate(POOL_WINDOWS):
        cols = slice(g * gd, (g + 1) * gd)
        cur = ext_ref[HIST_ROWS:, cols]
        tot = cur
        for back in range(1, win):
            tot = tot + ext_ref[HIST_ROWS - back:HIST_ROWS - back + tp, cols]
        cnt = jnp.minimum(pos + 1, win).astype(F32)
        d = (tot / cnt - cur).astype(BF16)
        y = jnp.dot(d, w_ref[g], preferred_element_type=F32) * sc_ref[:, cols]
        o_ref[0, :, cols] = y.astype(o_ref.dtype)
    ext_ref[0:HIST_ROWS, :] = ext_ref[tp:tp + HIST_ROWS, :]


def _pool(p, hist, w_pool, scale, *, pos0, name):
    nseq, length, w = p.shape
    tp = min(512, length)
    assert length % tp == 0 and tp >= HIST_ROWS
    return pl.pallas_call(
        functools.partial(_pool_kernel, tp=tp, pos0=pos0),
        grid=(nseq, length // tp),
        in_specs=[
            pl.BlockSpec((1, tp, w), lambda b, t: (b, t, 0)),
            pl.BlockSpec((1, HIST_ROWS, w), lambda b, t: (b, 0, 0)),
            pl.BlockSpec(w_pool.shape, lambda b, t: (0, 0, 0)),
            pl.BlockSpec((1, w), lambda b, t: (0, 0)),
        ],
        out_specs=pl.BlockSpec((1, tp, w), lambda b, t: (b, t, 0)),
        out_shape=jax.ShapeDtypeStruct((nseq, length, w), BF16),
        scratch_shapes=[pltpu.VMEM((HIST_ROWS + tp, w), F32)],
        compiler_params=_cparams(("parallel", "arbitrary")),
        name=name,
    )(p, hist, w_pool, scale)


def _oproj_kernel(h_ref, fox_ref, pool_ref, wo_ref, o_ref):
    mix = jnp.concatenate([fox_ref[...], pool_ref[...]], axis=1)
    o_ref[...] = h_ref[...] + jnp.dot(mix, wo_ref[...], preferred_element_type=F32)


def _oproj(h, fox, pool, wo, *, name):
    n, d = h.shape
    tm = min(512, n)
    assert n % tm == 0
    return pl.pallas_call(
        _oproj_kernel,
        grid=(n // tm,),
        in_specs=[
            pl.BlockSpec((tm, d), lambda i: (i, 0)),
            pl.BlockSpec((tm, fox.shape[1]), lambda i: (i, 0)),
            pl.BlockSpec((tm, pool.shape[1]), lambda i: (i, 0)),
            pl.BlockSpec(wo.shape, lambda i: (0, 0)),
        ],
        out_specs=pl.BlockSpec((tm, d), lambda i: (i, 0)),
        out_shape=jax.ShapeDtypeStruct((n, d), F32),
        compiler_params=_cparams(("parallel",)),
        name=name,
    )(h, fox, pool, wo)


def _round_up(x, m):
    return (x + m - 1) // m * m


def kernel(x_prompt, x_sample, cache_k, cache_v, cache_logf, state_pool, g_ffn1, w1_gate, w1_up, w1_down, g_mix, w_in, b_f, w_pool, pool_scale, w_o, g_ffn2, w2_gate, w2_up, w2_down, g_final):
    B, S, D = x_prompt.shape
    Bs, Tn, _ = x_sample.shape
    depth, _, P, H, Dh = cache_k.shape
    assert depth == 1 and Dh == HEAD_DIM
    fw = H * Dh
    pw = w_pool.shape[1] * w_pool.shape[2]
    assert w_in.shape[2] == 3 * fw + H + pw and state_pool.shape[2] == POOL_HIST

    row = lambda a: a.reshape(1, -1).astype(F32)
    w1g, w1u, w1d = w1_gate[0].astype(BF16), w1_up[0].astype(BF16), w1_down[0].astype(BF16)
    w2g, w2u, w2d = w2_gate[0].astype(BF16), w2_up[0].astype(BF16), w2_down[0].astype(BF16)
    wi = w_in[0]
    w_qkvp = jnp.concatenate([wi[:, :3 * fw], wi[:, 3 * fw + H:]], axis=1).astype(BF16)
    wf_cols = jnp.tile(wi[:, 3 * fw:3 * fw + H], (1, N_SPLIT))
    wf = jnp.pad(wf_cols, ((0, 0), (0, LANES - N_SPLIT * H))).astype(BF16)
    bf = jnp.pad(jnp.tile(b_f[0].astype(F32), N_SPLIT), (0, LANES - N_SPLIT * H)).reshape(1, LANES)
    wo = w_o[0].astype(BF16)
    wp = w_pool[0].astype(BF16)
    g1, gm, g2, gfin, psc = row(g_ffn1[0]), row(g_mix[0]), row(g_ffn2[0]), row(g_final), row(pool_scale[0])

    def pre(x2d, seq_len, tag):
        h = _ffn(x2d, g1, w1g, w1u, w1d, gfin, final_norm=False, name=f"ffn1_{tag}")
        return (h,) + tuple(_inproj(h, gm, w_qkvp, wf, bf, seq_len=seq_len, name=f"inproj_{tag}"))

    def post(h, fox, pool, tag):
        h2 = _oproj(h, fox, pool, wo, name=f"oproj_{tag}")
        return _ffn(h2, g2, w2g, w2u, w2d, gfin, final_norm=True, name=f"ffn2_{tag}")

    h, q, k, v, kh, vt, p, lf, qn2, kn2 = pre(x_prompt.reshape(B * S, D), S, "prompt")
    qx, kx, ends = _aug(lf.reshape(B, S, LANES), n_heads=H, tile_cap=ATTN_TILE, name="aug_prompt")
    first = _first_needed_chunk(qn2, kn2, ends, H)
    fox = _attn(first, q.reshape(B, S, fw), qx, kh, kx, vt, name="attn_prompt")
    p3 = p.reshape(B, S, pw)
    pool = _pool(p3, jnp.zeros((B, HIST_ROWS, pw), F32), wp, psc, pos0=0, name="pool_prompt")
    y_prompt = post(h, fox.reshape(B * S, fw), pool.reshape(B * S, pw), "prompt").reshape(B, S, D)
    k_prompt = k.reshape(1, B, S, H, Dh)
    v_prompt = v.reshape(1, B, S, H, Dh)
    logf_prompt = lf[:, :H].reshape(1, B, S, H)
    pool_prompt = p3[:, S - POOL_HIST:][None]

    h, q, k, v, kb, vb, p, lf, _, _ = pre(x_sample.reshape(Bs * Tn, D), None, "sample")
    L = _round_up(P + Tn, LANES)
    tail = L - P - Tn
    lf_cache = jnp.pad(jnp.tile(cache_logf[0].astype(F32), (1, 1, N_SPLIT)),
                       ((0, 0), (0, 0), (0, LANES - N_SPLIT * H)))
    lf_all = jnp.concatenate([lf_cache, lf.reshape(Bs, Tn, LANES), jnp.zeros((Bs, tail, LANES), F32)], axis=1)
    qx, kx, _ = _aug(lf_all, n_heads=H, tile_cap=L, name="aug_sample")
    fox = _attn_cached(q, qx, kx, cache_k[0], cache_v[0], kb, vb, n_seq=Bs, past=P, name="attn_sample")
    p3 = p.reshape(Bs, Tn, pw)
    hist = jnp.concatenate([jnp.zeros((Bs, HIST_ROWS - POOL_HIST, pw), F32), state_pool[0].astype(F32)], axis=1)
    pool = _pool(p3, hist, wp, psc, pos0=P, name="pool_sample")
    y_sample = post(h, fox, pool.reshape(Bs * Tn, pw), "sample").reshape(Bs, Tn, D)
    k_sample = k.reshape(1, Bs, Tn, H, Dh)
    v_sample = v.reshape(1, Bs, Tn, H, Dh)
    logf_sample = lf[:, :H].reshape(1, Bs, Tn, H)
    ext_tail = jnp.concatenate([state_pool[0].astype(F32), p3], axis=1)[:, Tn:]
    pool_sample = ext_tail[None]

    return (y_prompt, y_sample, k_prompt, v_prompt, logf_prompt, pool_prompt,
            k_sample, v_sample, logf_sample, pool_sample)
```

```python
import functools

import jax
import jax.numpy as jnp
from jax import lax
from jax.experimental import pallas as pl
from jax.experimental.pallas import tpu as pltpu

F32 = jnp.float32
BF16 = jnp.bfloat16

EPS = 1e-6
HEAD_DIM = 128
LANES = 128
MXU_COLS = 256
POOL_WINDOWS = (2, 4, 8, 16)
POOL_HIST = max(POOL_WINDOWS) - 1
HIST_ROWS = 16
N_SPLIT = 3
VT_ROWS = HEAD_DIM + 16
LOG2E = 1.4426950408889634
VMEM_LIMIT = 56 * 1024 * 1024
FFN_VMEM_LIMIT = 62 * 1024 * 1024
ATTN_UNROLL = 4
ATTN_FLAT = 6
ATTN_TILE = 512
PRUNE_GAP = 170.0
FFN_ROWS = 1024
FFN_COLS = 512
FFN_SLOTS = 2
FFN_NORM_ROWS = 256


def _cparams(sem, vmem_limit=VMEM_LIMIT):
    return pltpu.CompilerParams(dimension_semantics=sem, vmem_limit_bytes=vmem_limit)


def _pick_tile(length, cap, mult):
    best = None
    for t in range(mult, min(cap, length) + 1, mult):
        if length % t == 0:
            best = t
    assert best is not None, (length, cap, mult)
    return best


def _rms(x, g):
    return x * lax.rsqrt(jnp.mean(x * x, axis=-1, keepdims=True) + EPS) * g


def _split3(x):
    hi = x.astype(BF16).astype(F32)
    r1 = x - hi
    mid = r1.astype(BF16).astype(F32)
    lo = (r1 - mid).astype(BF16).astype(F32)
    return hi, mid, lo


def _ffn_kernel(x_ref, g_ref, gf_ref, wg_hbm, wu_hbm, wd_hbm, o_ref, xn_ref, wgu_buf, wd_buf, sem, *,
                final_norm, tf, nf):
    tile = pl.program_id(0)
    flip = (tile * (nf % 2)) % 2

    nblk = tf // MXU_COLS

    def copies(f, flip):
        slot = (f + flip) % 2
        out = []
        for j in range(nblk):
            src = pl.ds(pl.multiple_of(f * tf + j * MXU_COLS, MXU_COLS), MXU_COLS)
            for which, w_hbm in enumerate((wg_hbm, wu_hbm)):
                dst = wgu_buf.at[slot, :, pl.ds((2 * j + which) * MXU_COLS, MXU_COLS)]
                out.append(pltpu.make_async_copy(w_hbm.at[:, src], dst, sem.at[2 * j + which, slot]))
        rows = pl.ds(pl.multiple_of(f * tf, tf), tf)
        out.append(pltpu.make_async_copy(wd_hbm.at[rows, :], wd_buf.at[slot], sem.at[2 * nblk, slot]))
        return out

    def start(f, flip=flip):
        for c in copies(f, flip):
            c.start()

    def wait(f):
        for c in copies(f, flip):
            c.wait()

    def chunk(f, xn):
        slot = (f + flip) % 2
        gu = jnp.dot(xn, wgu_buf[slot], preferred_element_type=F32)
        blocks = [gu[:, j * MXU_COLS:(j + 1) * MXU_COLS] for j in range(2 * nblk)]
        gate = jnp.concatenate(blocks[0::2], axis=1)
        up = jnp.concatenate(blocks[1::2], axis=1)
        a = (gate * jax.nn.sigmoid(gate) * up).astype(BF16)
        return jnp.dot(a, wd_buf[slot], preferred_element_type=F32)

    @pl.when(tile == 0)
    def _():
        start(0)

    wait(0)
    start(1)
    xn = _rms(x_ref[...], g_ref[...]).astype(BF16)
    xn_ref[...] = xn
    o_ref[...] = chunk(0, xn)

    def middle(f, carry):
        wait(f)
        start(f + 1)
        o_ref[...] += chunk(f, xn_ref[...])
        return carry

    lax.fori_loop(1, nf - 1, middle, 0)
    wait(nf - 1)

    @pl.when(tile + 1 < pl.num_programs(0))
    def _():
        start(0, ((tile + 1) * (nf % 2)) % 2)

    o_ref[...] = x_ref[...] + 0.5 * (o_ref[...] + chunk(nf - 1, xn_ref[...]))
    if final_norm:
        rows = min(FFN_NORM_ROWS, o_ref.shape[0])
        for r in range(0, o_ref.shape[0], rows):
            o_ref[r:r + rows, :] = _rms(o_ref[r:r + rows, :], gf_ref[...])


def _ffn(x, g, wg, wu, wd, gf, *, final_norm, name):
    n, d = x.shape
    dff = wg.shape[1]
    tm = min(FFN_ROWS, n)
    tf = FFN_COLS
    nf = dff // tf
    assert n % tm == 0 and dff % tf == 0 and nf >= 3 and tf % MXU_COLS == 0
    return pl.pallas_call(
        functools.partial(_ffn_kernel, final_norm=final_norm, tf=tf, nf=nf),
        grid=(n // tm,),
        in_specs=[
            pl.BlockSpec((tm, d), lambda i: (i, 0)),
            pl.BlockSpec((1, d), lambda i: (0, 0)),
            pl.BlockSpec((1, d), lambda i: (0, 0)),
            pl.BlockSpec(memory_space=pl.ANY),
            pl.BlockSpec(memory_space=pl.ANY),
            pl.BlockSpec(memory_space=pl.ANY),
        ],
        out_specs=pl.BlockSpec((tm, d), lambda i: (i, 0)),
        out_shape=jax.ShapeDtypeStruct((n, d), F32),
        scratch_shapes=[pltpu.VMEM((tm, d), BF16),
                        pltpu.VMEM((FFN_SLOTS, d, 2 * tf), BF16), pltpu.VMEM((FFN_SLOTS, tf, d), BF16),
                        pltpu.SemaphoreType.DMA((2 * (tf // MXU_COLS) + 1, FFN_SLOTS))],
        compiler_params=_cparams(("arbitrary",), FFN_VMEM_LIMIT),
        name=name,
    )(x, g, gf, wg, wu, wd)


def _inproj_kernel(h_ref, g_ref, w_ref, wf_ref, bf_ref,
                   q_ref, k_ref, v_ref, ka_ref, va_ref, p_ref, lf_ref, qn_ref, kn_ref, *, scale, width, head_major):
    hn = _rms(h_ref[...], g_ref[...]).astype(BF16)
    n_heads = width // HEAD_DIM

    def proj(c):
        return jnp.dot(hn, w_ref[:, c * width:(c + 1) * width], preferred_element_type=F32)

    def store_heads(o_ref, u):
        for hd in range(n_heads):
            o_ref[pl.ds(hd, u.shape[0], stride=n_heads), :] = u[:, hd * HEAD_DIM:(hd + 1) * HEAD_DIM]

    def max_sq_norms(xb):
        sub = lax.broadcasted_iota(jnp.int32, (n_heads, LANES), 0)
        out = jnp.zeros((n_heads, LANES), F32)
        for hd in range(n_heads):
            x = xb[:, hd * HEAD_DIM:(hd + 1) * HEAD_DIM].astype(F32)
            top = jnp.max(jnp.sum(x * x, axis=1, keepdims=True), axis=0, keepdims=True)
            out = jnp.where(sub == hd, top, out)
        return out

    lf_ref[...] = jax.nn.log_sigmoid(jnp.dot(hn, wf_ref[...], preferred_element_type=F32) + bf_ref[...])
    qb = (proj(0) * scale).astype(BF16)
    q_ref[...] = qb
    qn_ref[...] = max_sq_norms(qb)
    u = proj(1)
    store_heads(k_ref, u)
    kb = u.astype(BF16)
    kn_ref[...] = max_sq_norms(kb)
    u = proj(2)
    store_heads(v_ref, u)
    vb = u.astype(BF16)
    if head_major:
        sub = lax.broadcasted_iota(jnp.int32, (VT_ROWS - HEAD_DIM, vb.shape[0]), 0)
        tail = jnp.where(sub == 0, 1.0, 0.0).astype(BF16)
        for hd in range(n_heads):
            cols = slice(hd * HEAD_DIM, (hd + 1) * HEAD_DIM)
            ka_ref[0, hd] = kb[:, cols]
            va_ref[0, hd, 0:HEAD_DIM, :] = vb[:, cols].T
            va_ref[0, hd, HEAD_DIM:, :] = tail
    else:
        ka_ref[...] = kb
        va_ref[...] = vb
    p_ref[...] = proj(3)


def _inproj(h, g, w_qkvp, wf, bf, *, seq_len, name):
    n, d = h.shape
    w = w_qkvp.shape[1] // 4
    tm = min(ATTN_TILE, n)
    assert n % tm == 0 and w % HEAD_DIM == 0
    n_heads = w // HEAD_DIM
    assert n_heads == 8
    row = lambda i: (i, 0)
    const = lambda i: (0, 0)
    resident = pl.Buffered(1)
    if seq_len is None:
        ka_spec = va_spec = pl.BlockSpec((tm, w), row)
        ka_shape = va_shape = jax.ShapeDtypeStruct((n, w), BF16)
    else:
        assert seq_len % tm == 0 and n % seq_len == 0
        tps = seq_len // tm
        ka_spec = pl.BlockSpec((1, n_heads, tm, HEAD_DIM), lambda i: (i // tps, 0, i % tps, 0))
        va_spec = pl.BlockSpec((1, n_heads, VT_ROWS, tm), lambda i: (i // tps, 0, 0, i % tps))
        ka_shape = jax.ShapeDtypeStruct((n // seq_len, n_heads, seq_len, HEAD_DIM), BF16)
        va_shape = jax.ShapeDtypeStruct((n // seq_len, n_heads, VT_ROWS, seq_len), BF16)
    return pl.pallas_call(
        functools.partial(_inproj_kernel, scale=HEAD_DIM ** -0.5 * LOG2E, width=w, head_major=seq_len is not None),
        grid=(n // tm,),
        in_specs=[
            pl.BlockSpec((tm, d), row),
            pl.BlockSpec((1, d), const),
            pl.BlockSpec((d, 4 * w), const, pipeline_mode=resident),
            pl.BlockSpec((d, LANES), const),
            pl.BlockSpec((1, LANES), const),
        ],
        out_specs=[
            pl.BlockSpec((tm, w), row),
            pl.BlockSpec((tm * n_heads, HEAD_DIM), row), pl.BlockSpec((tm * n_heads, HEAD_DIM), row),
            ka_spec, va_spec, pl.BlockSpec((tm, w), row),
            pl.BlockSpec((tm, LANES), row),
            pl.BlockSpec((n_heads, LANES), row), pl.BlockSpec((n_heads, LANES), row),
        ],
        out_shape=[
            jax.ShapeDtypeStruct((n, w), BF16),
            jax.ShapeDtypeStruct((n * n_heads, HEAD_DIM), F32),
            jax.ShapeDtypeStruct((n * n_heads, HEAD_DIM), F32),
            ka_shape,
            va_shape,
            jax.ShapeDtypeStruct((n, w), F32),
            jax.ShapeDtypeStruct((n, LANES), F32),
            jax.ShapeDtypeStruct((n // tm * n_heads, LANES), F32),
            jax.ShapeDtypeStruct((n // tm * n_heads, LANES), F32),
        ],
        compiler_params=_cparams(("parallel",)),
        name=name,
    )(h, g, w_qkvp, wf, bf)


def _aug_kernel(lf_ref, tri_ref, selq_ref, selk_ref, qx_ref, kx_ref, ends_ref, carry_ref, *, n_heads):
    t = pl.program_id(1)

    @pl.when(t == 0)
    def _():
        carry_ref[...] = jnp.zeros_like(carry_ref)

    lf = lf_ref[0]
    lane = lax.broadcasted_iota(jnp.int32, lf.shape, 1)

    def pick(parts, extra=None):
        out = jnp.zeros_like(lf) if extra is None else extra
        for j in reversed(range(N_SPLIT)):
            out = jnp.where((lane >= j * n_heads) & (lane < (j + 1) * n_heads), parts[j], out)
        return out

    cs = jnp.dot(tri_ref[...], pick(_split3(lf)).astype(BF16), preferred_element_type=F32)
    c = cs
    for j in range(1, N_SPLIT):
        c = c + pltpu.roll(cs, LANES - j * n_heads, axis=1)
    c = c + carry_ref[...]
    carry_ref[...] = c[c.shape[0] - 1:, :]

    crep = c
    for j in range(1, N_SPLIT):
        crep = jnp.where((lane >= j * n_heads) & (lane < (j + 1) * n_heads),
                         pltpu.roll(c, j * n_heads, axis=1), crep)
    ones = jnp.where(lane == N_SPLIT * n_heads, 1.0, 0.0).astype(F32)
    pieces = pick(_split3(crep * LOG2E), extra=ones).astype(BF16)
    qx_ref[0] = jnp.dot(pieces, selq_ref[...], preferred_element_type=F32).astype(BF16)
    kx = jnp.dot(pieces, selk_ref[...], preferred_element_type=F32).astype(BF16)
    for hd in range(n_heads):
        kx_ref[0, hd] = kx[:, hd * HEAD_DIM:(hd + 1) * HEAD_DIM]
    sub = lax.broadcasted_iota(jnp.int32, (8, LANES), 0)
    ends_ref[0] = jnp.where(sub == 0, c[0:1, :] * LOG2E, jnp.where(sub == 1, c[c.shape[0] - 1:, :] * LOG2E, 0.0))


def _aug_constants(tm, n_heads):
    r = jnp.arange(tm)
    tri = (r[:, None] >= r[None, :]).astype(BF16)
    rows = jnp.arange(LANES)[:, None]
    cols = jnp.arange(n_heads * HEAD_DIM)[None, :]
    head, lane = cols // HEAD_DIM, cols % HEAD_DIM
    one_row = N_SPLIT * n_heads
    is_piece = lambda off: (lane >= off) & (lane < off + N_SPLIT) & (rows == (lane - off) * n_heads + head)
    is_one = lambda off: (lane >= off) & (lane < off + N_SPLIT) & (rows == one_row)
    selq = (is_piece(0) | is_one(N_SPLIT)).astype(BF16)
    selk = is_one(0).astype(BF16) - is_piece(N_SPLIT).astype(BF16)
    return tri, selq, selk


def _aug(lf, *, n_heads, tile_cap, name):
    nseq, length, _ = lf.shape
    tm = _pick_tile(length, tile_cap, 16)
    assert (N_SPLIT * n_heads) < LANES
    w = n_heads * HEAD_DIM
    tri, selq, selk = _aug_constants(tm, n_heads)
    return pl.pallas_call(
        functools.partial(_aug_kernel, n_heads=n_heads),
        grid=(nseq, length // tm),
        in_specs=[
            pl.BlockSpec((1, tm, LANES), lambda b, t: (b, t, 0)),
            pl.BlockSpec((tm, tm), lambda b, t: (0, 0)),
            pl.BlockSpec((LANES, w), lambda b, t: (0, 0)),
            pl.BlockSpec((LANES, w), lambda b, t: (0, 0)),
        ],
        out_specs=[pl.BlockSpec((1, tm, w), lambda b, t: (b, t, 0)),
                   pl.BlockSpec((1, n_heads, tm, HEAD_DIM), lambda b, t: (b, 0, t, 0)),
                   pl.BlockSpec((1, 8, LANES), lambda b, t: (b, t, 0))],
        out_shape=[jax.ShapeDtypeStruct((nseq, length, w), BF16),
                   jax.ShapeDtypeStruct((nseq, n_heads, length, HEAD_DIM), BF16),
                   jax.ShapeDtypeStruct((nseq, length // tm * 8, LANES), F32)],
        scratch_shapes=[pltpu.VMEM((1, LANES), F32)],
        compiler_params=_cparams(("parallel", "arbitrary")),
        name=name,
    )(lf, tri, selq, selk)


def _attn_kernel(first_ref, q_ref, qx_ref, k_ref, kx_ref, vt_ref, o_ref, m_ref, acc_ref, s0_ref, s1_ref, *, t):
    b, hd, qi = pl.program_id(0), pl.program_id(1), pl.program_id(2)
    q = jnp.concatenate([q_ref[0], qx_ref[0]], axis=1)
    n_rest = qi - first_ref[(b * pl.num_programs(1) + hd) * pl.num_programs(2) + qi]
    m_ref[...] = jnp.full_like(m_ref, -jnp.inf)
    acc_ref[...] = jnp.zeros_like(acc_ref)

    def scores(j):
        st = pl.multiple_of(jnp.maximum(j, 0) * t, t)
        k = jnp.concatenate([k_ref[0, 0, pl.ds(st, t), :], kx_ref[0, 0, pl.ds(st, t), :]], axis=1)
        return lax.dot_general(k, q, (((1,), (1,)), ((), ())), preferred_element_type=F32)

    def accumulate(j, s, masked):
        st = pl.multiple_of(j * t, t)
        if masked:
            key_pos = lax.broadcasted_iota(jnp.int32, (t, t), 0)
            qry_pos = lax.broadcasted_iota(jnp.int32, (t, t), 1)
            s = jnp.where(key_pos <= qry_pos, s, -jnp.inf)
        m_prev = m_ref[...]
        m_new = jnp.maximum(m_prev, jnp.max(s, axis=0, keepdims=True))
        alpha = jnp.exp2(m_prev - m_new)
        p = jnp.exp2(s - m_new).astype(BF16)
        acc_ref[...] = alpha * acc_ref[...] + jnp.dot(vt_ref[0, 0, :, pl.ds(st, t)], p, preferred_element_type=F32)
        m_ref[...] = m_new

    bufs = (s0_ref, s1_ref)

    def visit(k0, n, prefetch_after):
        for u in range(n):
            k = k0[0] + u
            if u + 1 < n or prefetch_after:
                bufs[(k0[1] + u + 1) % 2][...] = scores(qi - k - 1)
            accumulate(qi - k, bufs[(k0[1] + u) % 2][...], False)

    def finish():
        acc = acc_ref[...]
        out_t = acc[0:HEAD_DIM] / acc[HEAD_DIM:HEAD_DIM + 1]
        o_ref[0] = out_t.T.astype(o_ref.dtype)

    for r in range(ATTN_FLAT):
        @pl.when(n_rest == r)
        def _(r=r):
            ahead = [scores(qi - u) for u in range(min(2, r + 1))]
            for u in range(r + 1):
                if u + 2 <= r:
                    ahead.append(scores(qi - u - 2))
                accumulate(qi - u, ahead[u], u == 0)
            finish()

    @pl.when(n_rest >= ATTN_FLAT)
    def _():
        s0_ref[...] = scores(qi)
        s1_ref[...] = scores(qi - 1)
        accumulate(qi, s0_ref[...], True)

        def group(g, carry):
            visit((1 + ATTN_UNROLL * g, 1), ATTN_UNROLL, True)
            return carry

        lax.fori_loop(0, n_rest // ATTN_UNROLL, group, 0)
        rem = n_rest % ATTN_UNROLL
        for r in range(1, ATTN_UNROLL):
            @pl.when(rem == r)
            def _(r=r):
                visit((n_rest - r + 1, 1), r, False)

        finish()


def _first_needed_chunk(qn2, kn2, ends, n_heads):
    nb, nt = ends.shape[0], ends.shape[1] // 8
    qn = jnp.sqrt(qn2.reshape(nb, nt, n_heads, LANES)[..., 0])
    kn = jnp.sqrt(kn2.reshape(nb, nt, n_heads, LANES)[..., 0])
    ends = ends.reshape(nb, nt, 8, LANES)
    c_first, c_last = ends[:, :, 0, :n_heads], ends[:, :, 1, :n_heads]
    ub = qn[:, :, None, :] * kn[:, None, :, :] + c_first[:, :, None, :] - c_last[:, None, :, :]
    lb = -(qn * kn)[:, :, None, :]
    i = jnp.arange(nt)[None, :, None, None]
    j = jnp.arange(nt)[None, None, :, None]
    needed = (ub - lb >= -PRUNE_GAP) & (j < i)
    first = jnp.min(jnp.where(needed, j, i), axis=2)
    return first.transpose(0, 2, 1).reshape(-1).astype(jnp.int32)


def _attn(first, q, qx, k, kx, vt, *, name):
    nb, length, w = q.shape
    n_heads = w // HEAD_DIM
    t = ATTN_TILE
    assert length % t == 0 and k.shape == (nb, n_heads, length, HEAD_DIM) and vt.shape == (nb, n_heads, VT_ROWS, length)
    blk = lambda b, h, i, first: (b, i, h)
    head = lambda b, h, i, first: (b, h, 0, 0)
    return pl.pallas_call(
        functools.partial(_attn_kernel, t=t),
        grid_spec=pltpu.PrefetchScalarGridSpec(
            num_scalar_prefetch=1,
            grid=(nb, n_heads, length // t),
            in_specs=[pl.BlockSpec((1, t, HEAD_DIM), blk), pl.BlockSpec((1, t, HEAD_DIM), blk),
                      pl.BlockSpec((1, 1, length, HEAD_DIM), head), pl.BlockSpec((1, 1, length, HEAD_DIM), head),
                      pl.BlockSpec((1, 1, VT_ROWS, length), head)],
            out_specs=pl.BlockSpec((1, t, HEAD_DIM), blk),
            scratch_shapes=[pltpu.VMEM((1, t), F32), pltpu.VMEM((VT_ROWS, t), F32),
                            pltpu.VMEM((t, t), F32), pltpu.VMEM((t, t), F32)],
        ),
        out_shape=jax.ShapeDtypeStruct((nb, length, w), BF16),
        compiler_params=_cparams(("parallel", "parallel", "arbitrary")),
        name=name,
    )(first, q, qx, k, kx, vt)


def _attn_cached_kernel(q_ref, qx_ref, kx_ref, ck_ref, cv_ref, kn_ref, vn_ref, o_ref, *, n_heads, past, tn):
    pad = LANES - tn
    causal = (lax.broadcasted_iota(jnp.int32, (tn, LANES), 1) <= lax.broadcasted_iota(jnp.int32, (tn, LANES), 0))
    nt = (((1,), (1,)), ((), ()))
    for hd in range(n_heads):
        cols = slice(hd * HEAD_DIM, (hd + 1) * HEAD_DIM)
        q = jnp.concatenate([q_ref[:, cols], qx_ref[0, :, cols]], axis=1)
        kc = jnp.concatenate([ck_ref[pl.ds(hd, past, stride=n_heads), :].astype(BF16),
                              kx_ref[0, hd, 0:past, :]], axis=1)
        kn = jnp.concatenate([kn_ref[:, cols], kx_ref[0, hd, past:past + tn, :]], axis=1)
        kn = jnp.concatenate([kn, jnp.zeros((pad, kn.shape[1]), BF16)], axis=0)
        s_c = lax.dot_general(q, kc, nt, preferred_element_type=F32)
        s_n = jnp.where(causal, lax.dot_general(q, kn, nt, preferred_element_type=F32), -jnp.inf)
        m = jnp.maximum(jnp.max(s_c, axis=1, keepdims=True), jnp.max(s_n, axis=1, keepdims=True))
        p_c = jnp.exp2(s_c - m)
        p_n = jnp.exp2(s_n - m)
        denom = jnp.sum(p_c, axis=1, keepdims=True) + jnp.sum(p_n, axis=1, keepdims=True)
        vc = cv_ref[pl.ds(hd, past, stride=n_heads), :].astype(BF16)
        vn = jnp.concatenate([vn_ref[:, cols], jnp.zeros((pad, HEAD_DIM), BF16)], axis=0)
        out = (jnp.dot(p_c.astype(BF16), vc, preferred_element_type=F32)
               + jnp.dot(p_n.astype(BF16), vn, preferred_element_type=F32))
        o_ref[:, cols] = (out / denom).astype(o_ref.dtype)


def _attn_cached(q, qx, kx, cache_k, cache_v, k_new, v_new, *, n_seq, past, name):
    n, w = q.shape
    tn = n // n_seq
    n_heads = w // HEAD_DIM
    assert tn <= LANES and tn % 16 == 0 and past % tn == 0 and kx.shape[2] >= past + tn
    ck = cache_k.reshape(n_seq * past * n_heads, HEAD_DIM)
    cv = cache_v.reshape(n_seq * past * n_heads, HEAD_DIM)
    rows = pl.BlockSpec((tn, w), lambda b: (b, 0))
    cached = pl.BlockSpec((past * n_heads, HEAD_DIM), lambda b: (b, 0))
    return pl.pallas_call(
        functools.partial(_attn_cached_kernel, n_heads=n_heads, past=past, tn=tn),
        grid=(n_seq,),
        in_specs=[
            rows,
            pl.BlockSpec((1, tn, w), lambda b: (b, past // tn, 0)),
            pl.BlockSpec((1,) + kx.shape[1:], lambda b: (b, 0, 0, 0)),
            cached, cached, rows, rows,
        ],
        out_specs=rows,
        out_shape=jax.ShapeDtypeStruct((n, w), BF16),
        compiler_params=_cparams(("parallel",)),
        name=name,
    )(q, qx, kx, ck, cv, k_new, v_new)


def _pool_kernel(p_ref, hist_ref, w_ref, sc_ref, o_ref, ext_ref, *, tp, pos0):
    t = pl.program_id(1)

    @pl.when(t == 0)
    def _():
        ext_ref[0:HIST_ROWS, :] = hist_ref[0]

    ext_ref[HIST_ROWS:, :] = p_ref[0]
    pos = pos0 + t * tp + lax.broadcasted_iota(jnp.int32, (tp, 1), 0)
    gd = w_ref.shape[1]
    for g, win in enumerate(POOL_WINDOWS):
        cols = slice(g * gd, (g + 1) * gd)
        cur = ext_ref[HIST_ROWS:, cols]
        tot = cur
        for back in range(1, win):
            tot = tot + ext_ref[HIST_ROWS - back:HIST_ROWS - back + tp, cols]
        cnt = jnp.minimum(pos + 1, win).astype(F32)
        d = (tot / cnt - cur).astype(BF16)
        y = jnp.dot(d, w_ref[g], preferred_element_type=F32) * sc_ref[:, cols]
        o_ref[0, :, cols] = y.astype(o_ref.dtype)
    ext_ref[0:HIST_ROWS, :] = ext_ref[tp:tp + HIST_ROWS, :]


def _pool(p, hist, w_pool, scale, *, pos0, name):
    nseq, length, w = p.shape
    tp = min(512, length)
    assert length % tp == 0 and tp >= HIST_ROWS
    return pl.pallas_call(
        functools.partial(_pool_kernel, tp=tp, pos0=pos0),
        grid=(nseq, length // tp),
        in_specs=[
            pl.BlockSpec((1, tp, w), lambda b, t: (b, t, 0)),
            pl.BlockSpec((1, HIST_ROWS, w), lambda b, t: (b, 0, 0)),
            pl.BlockSpec(w_pool.shape, lambda b, t: (0, 0, 0)),
            pl.BlockSpec((1, w), lambda b, t: (0, 0)),
        ],
        out_specs=pl.BlockSpec((1, tp, w), lambda b, t: (b, t, 0)),
        out_shape=jax.ShapeDtypeStruct((nseq, length, w), BF16),
        scratch_shapes=[pltpu.VMEM((HIST_ROWS + tp, w), F32)],
        compiler_params=_cparams(("parallel", "arbitrary")),
        name=name,
    )(p, hist, w_pool, scale)


def _oproj_kernel(h_ref, fox_ref, pool_ref, wo_ref, o_ref):
    mix = jnp.concatenate([fox_ref[...], pool_ref[...]], axis=1)
    o_ref[...] = h_ref[...] + jnp.dot(mix, wo_ref[...], preferred_element_type=F32)


def _oproj(h, fox, pool, wo, *, name):
    n, d = h.shape
    tm = min(512, n)
    assert n % tm == 0
    return pl.pallas_call(
        _oproj_kernel,
        grid=(n // tm,),
        in_specs=[
            pl.BlockSpec((tm, d), lambda i: (i, 0)),
            pl.BlockSpec((tm, fox.shape[1]), lambda i: (i, 0)),
            pl.BlockSpec((tm, pool.shape[1]), lambda i: (i, 0)),
            pl.BlockSpec(wo.shape, lambda i: (0, 0)),
        ],
        out_specs=pl.BlockSpec((tm, d), lambda i: (i, 0)),
        out_shape=jax.ShapeDtypeStruct((n, d), F32),
        compiler_params=_cparams(("parallel",)),
        name=name,
    )(h, fox, pool, wo)


def _round_up(x, m):
    return (x + m - 1) // m * m


def kernel(x_prompt, x_sample, cache_k, cache_v, cache_logf, state_pool, g_ffn1, w1_gate, w1_up, w1_down, g_mix, w_in, b_f, w_pool, pool_scale, w_o, g_ffn2, w2_gate, w2_up, w2_down, g_final):
    B, S, D = x_prompt.shape
    Bs, Tn, _ = x_sample.shape
    depth, _, P, H, Dh = cache_k.shape
    assert depth == 1 and Dh == HEAD_DIM
    fw = H * Dh
    pw = w_pool.shape[1] * w_pool.shape[2]
    assert w_in.shape[2] == 3 * fw + H + pw and state_pool.shape[2] == POOL_HIST

    row = lambda a: a.reshape(1, -1).astype(F32)
    w1g, w1u, w1d = w1_gate[0].astype(BF16), w1_up[0].astype(BF16), w1_down[0].astype(BF16)
    w2g, w2u, w2d = w2_gate[0].astype(BF16), w2_up[0].astype(BF16), w2_down[0].astype(BF16)
    wi = w_in[0]
    w_qkvp = jnp.concatenate([wi[:, :3 * fw], wi[:, 3 * fw + H:]], axis=1).astype(BF16)
    wf_cols = jnp.tile(wi[:, 3 * fw:3 * fw + H], (1, N_SPLIT))
    wf = jnp.pad(wf_cols, ((0, 0), (0, LANES - N_SPLIT * H))).astype(BF16)
    bf = jnp.pad(jnp.tile(b_f[0].astype(F32), N_SPLIT), (0, LANES - N_SPLIT * H)).reshape(1, LANES)
    wo = w_o[0].astype(BF16)
    wp = w_pool[0].astype(BF16)
    g1, gm, g2, gfin, psc = row(g_ffn1[0]), row(g_mix[0]), row(g_ffn2[0]), row(g_final), row(pool_scale[0])

    def pre(x2d, seq_len, tag):
        h = _ffn(x2d, g1, w1g, w1u, w1d, gfin, final_norm=False, name=f"ffn1_{tag}")
        return (h,) + tuple(_inproj(h, gm, w_qkvp, wf, bf, seq_len=seq_len, name=f"inproj_{tag}"))

    def post(h, fox, pool, tag):
        h2 = _oproj(h, fox, pool, wo, name=f"oproj_{tag}")
        return _ffn(h2, g2, w2g, w2u, w2d, gfin, final_norm=True, name=f"ffn2_{tag}")

    h, q, k, v, kh, vt, p, lf, qn2, kn2 = pre(x_prompt.reshape(B * S, D), S, "prompt")
    qx, kx, ends = _aug(lf.reshape(B, S, LANES), n_heads=H, tile_cap=ATTN_TILE, name="aug_prompt")
    first = _first_needed_chunk(qn2, kn2, ends, H)
    fox = _attn(first, q.reshape(B, S, fw), qx, kh, kx, vt, name="attn_prompt")
    p3 = p.reshape(B, S, pw)
    pool = _pool(p3, jnp.zeros((B, HIST_ROWS, pw), F32), wp, psc, pos0=0, name="pool_prompt")
    y_prompt = post(h, fox.reshape(B * S, fw), pool.reshape(B * S, pw), "prompt").reshape(B, S, D)
    k_prompt = k.reshape(1, B, S, H, Dh)
    v_prompt = v.reshape(1, B, S, H, Dh)
    logf_prompt = lf[:, :H].reshape(1, B, S, H)
    pool_prompt = p3[:, S - POOL_HIST:][None]

    h, q, k, v, kb, vb, p, lf, _, _ = pre(x_sample.reshape(Bs * Tn, D), None, "sample")
    L = _round_up(P + Tn, LANES)
    tail = L - P - Tn
    lf_cache = jnp.pad(jnp.tile(cache_logf[0].astype(F32), (1, 1, N_SPLIT)),
                       ((0, 0), (0, 0), (0, LANES - N_SPLIT * H)))
    lf_all = jnp.concatenate([lf_cache, lf.reshape(Bs, Tn, LANES), jnp.zeros((Bs, tail, LANES), F32)], axis=1)
    qx, kx, _ = _aug(lf_all, n_heads=H, tile_cap=L, name="aug_sample")
    fox = _attn_cached(q, qx, kx, cache_k[0], cache_v[0], kb, vb, n_seq=Bs, past=P, name="attn_sample")
    p3 = p.reshape(Bs, Tn, pw)
    hist = jnp.concatenate([jnp.zeros((Bs, HIST_ROWS - POOL_HIST, pw), F32), state_pool[0].astype(F32)], axis=1)
    pool = _pool(p3, hist, wp, psc, pos0=P, name="pool_sample")
    y_sample = post(h, fox, pool.reshape(Bs * Tn, pw), "sample").reshape(Bs, Tn, D)
    k_sample = k.reshape(1, Bs, Tn, H, Dh)
    v_sample = v.reshape(1, Bs, Tn, H, Dh)
    logf_sample = lf[:, :H].reshape(1, Bs, Tn, H)
    ext_tail = jnp.concatenate([state_pool[0].astype(F32), p3], axis=1)[:, Tn:]
    pool_sample = ext_tail[None]

    return (y_prompt, y_sample, k_prompt, v_prompt, logf_prompt, pool_prompt,
            k_sample, v_sample, logf_sample, pool_sample)
```
